```python
import math
import jax
import jax.numpy as jnp
from jax import lax
import numpy as np

D_MODEL = 2048
BATCH = 4
SEQ = 4096
DEPTH = 1
DEC_BATCH = 8
DEC_SEQ = 32
PAST_LEN = 4096

CHUNK = 64
H_A = 8
DK_A = 128
DV_A = 128
W_A = H_A * DV_A
CONV_W = 4
CONV_CH = 2 * H_A * DK_A + W_A
H_B = 16
N_B = 64
W_B = H_B * N_B
LORA_W = 96
LORA_A = 96
LORA_G = 256
SHIFT_W = 3 * W_B + LORA_W + LORA_A + LORA_G
D_FF = 4 * D_MODEL
D_PLE = 256
N_IN = CONV_CH + 2 * H_A + W_A + SHIFT_W + 2 * D_MODEL
EPS = 1e-6
LNX_EPS = 64e-5

kernel_name = 'hybrid_gdn_rwkv7_streaming_step'

F32 = jnp.float32


def _split(x, sizes):
    idx = [int(s) for s in np.cumsum(sizes)[:-1]]
    return jnp.split(x, idx, axis=-1)


def rmsnorm(x, g, eps=EPS):
    xf = x.astype(F32)
    y = xf * lax.rsqrt(jnp.mean(xf * xf, axis=-1, keepdims=True) + eps)
    return (y * g.astype(F32)).astype(x.dtype)


def l2norm(x):
    xf = x.astype(F32)
    return xf * lax.rsqrt(jnp.sum(xf * xf, axis=-1, keepdims=True) + EPS)


def causal_conv_silu(u, buf, w):
    T = u.shape[1]
    full = jnp.concatenate([buf.astype(u.dtype), u], axis=1)
    out = full[:, 0:T] * w[0]
    for j in range(1, CONV_W):
        out = out + full[:, j:j + T] * w[j]
    return jax.nn.silu(out), full[:, T:]


def token_shift(u, prev):
    shifted = jnp.concatenate([prev[:, None].astype(u.dtype), u[:, :-1]], axis=1)
    return shifted, u[:, -1]


def gated_delta_rule(q, k, v, g, beta, s0):
    B, T, H, _ = q.shape
    c = min(CHUNK, T)
    n = -(-T // c)
    pad = n * c - T

    def blocks(a):
        a = a.astype(F32)
        a = jnp.pad(a, [(0, 0), (0, pad)] + [(0, 0)] * (a.ndim - 2))
        a = a.reshape((B, n, c) + a.shape[2:])
        return jnp.moveaxis(a, 3, 1)

    qb, kb, vb, gb, bb = blocks(q), blocks(k), blocks(v), blocks(g), blocks(beta)
    G = jnp.cumsum(gb, axis=-1)
    idx = jnp.arange(c)
    incl = idx[:, None] >= idx[None, :]
    strict = idx[:, None] > idx[None, :]
    decay = jnp.exp(jnp.where(incl, G[..., :, None] - G[..., None, :], -jnp.inf))
    kbeta = kb * bb[..., None]
    a_mat = jnp.where(strict, jnp.einsum('bhnik,bhnjk->bhnij', kbeta, kb) * decay, 0.0)
    eye = jnp.eye(c, dtype=F32)
    t_inv = lax.linalg.triangular_solve(eye + a_mat, jnp.broadcast_to(eye, a_mat.shape),
                                        left_side=True, lower=True)
    eG = jnp.exp(G)[..., None]
    u = jnp.einsum('bhnij,bhnjd->bhnid', t_inv, vb * bb[..., None])
    w = jnp.einsum('bhnij,bhnjk->bhnik', t_inv, kbeta * eG)
    qk = jnp.where(incl, jnp.einsum('bhnik,bhnjk->bhnij', qb, kb) * decay, 0.0)
    q_dec = qb * eG
    g_last = G[..., -1]
    k_dec = kb * jnp.exp(g_last[..., None] - G)[..., None]

    def step(s, xs):
        u_c, w_c, qk_c, q_c, k_c, gl = xs
        v_new = u_c - jnp.einsum('bhik,bhkd->bhid', w_c, s)
        o = jnp.einsum('bhik,bhkd->bhid', q_c, s) + jnp.einsum('bhij,bhjd->bhid', qk_c, v_new)
        s = s * jnp.exp(gl)[..., None, None] + jnp.einsum('bhik,bhid->bhkd', k_c, v_new)
        return s, o

    xs = tuple(jnp.moveaxis(a, 2, 0) for a in (u, w, qk, q_dec, k_dec, g_last))
    s, o = lax.scan(step, s0.astype(F32), xs)
    o = jnp.transpose(o, (1, 0, 3, 2, 4)).reshape(B, n * c, H, -1)[:, :T]
    return o, s


def rwkv7_recurrence(r, decay, k, v, kk, a, s0):
    def step(s, xs):
        r_t, w_t, k_t, v_t, kk_t, a_t = xs
        sa = jnp.einsum('bhvk,bhk->bhv', s, -kk_t)
        s = (s * w_t[:, :, None, :] + sa[..., None] * (kk_t * a_t)[:, :, None, :]
             + v_t[..., None] * k_t[:, :, None, :])
        return s, jnp.einsum('bhvk,bhk->bhv', s, r_t)

    xs = tuple(jnp.moveaxis(t.astype(F32), 1, 0) for t in (r, decay, k, v, kk, a))
    s, y = lax.scan(step, s0.astype(F32), xs)
    return jnp.moveaxis(y, 0, 1), s


def gdn_branch(conv_in, alpha_logit, beta_logit, gate, buf, s0, conv_w, a_log, dt_bias, norm_g):
    B, T, _ = conv_in.shape
    c_out, new_buf = causal_conv_silu(conv_in, buf, conv_w)
    q, k, v = _split(c_out, [H_A * DK_A, H_A * DK_A, W_A])
    q = l2norm(q.reshape(B, T, H_A, DK_A)) * (DK_A ** -0.5)
    k = l2norm(k.reshape(B, T, H_A, DK_A))
    v = v.reshape(B, T, H_A, DV_A)
    beta = jax.nn.sigmoid(beta_logit.astype(F32))
    g = -jnp.exp(a_log.astype(F32)) * jax.nn.softplus(alpha_logit.astype(F32) + dt_bias.astype(F32))
    o, s = gated_delta_rule(q, k, v, g, beta, s0)
    o = rmsnorm(o, norm_g) * jax.nn.silu(gate.astype(F32)).reshape(B, T, H_A, DV_A)
    return o.reshape(B, T, W_A).astype(conv_in.dtype), s, new_buf


def rwkv_branch(xb, prev, s0, mu, w0, w2, a0, a2, g2, k_k, k_a, r_k, lnx_g, lnx_b):
    B, T, _ = xb.shape
    shifted, last = token_shift(xb, prev)
    xm = xb + (shifted - xb) * mu
    r, k, v, xw, xa, xg = _split(xm, [W_B, W_B, W_B, LORA_W, LORA_A, LORA_G])
    w_log = -jax.nn.softplus(-(w0 + jnp.tanh(xw) @ w2).astype(F32)) - 0.5
    decay = jnp.exp(-jnp.exp(w_log))
    a = jax.nn.sigmoid((a0 + xa @ a2).astype(F32))
    gate = (jax.nn.sigmoid(xg) @ g2).astype(F32)

    def heads(t):
        return t.astype(F32).reshape(B, T, H_B, N_B)

    rh, vh, ah = heads(r), heads(v), heads(a)
    kk = l2norm(heads(k * k_k))
    kh = heads(k.astype(F32) * (1.0 + (a - 1.0) * k_a.astype(F32)))
    y, s = rwkv7_recurrence(rh, heads(decay), kh, vh, kk, ah, s0)
    mean = jnp.mean(y, axis=-1, keepdims=True)
    yc = y - mean
    y = yc * lax.rsqrt(jnp.mean(yc * yc, axis=-1, keepdims=True) + LNX_EPS)
    y = y.reshape(B, T, W_B) * lnx_g.astype(F32) + lnx_b.astype(F32)
    bonus = jnp.sum(rh * kh * r_k.astype(F32), axis=-1, keepdims=True) * vh
    y = (y + bonus.reshape(B, T, W_B)) * gate
    return y.astype(xb.dtype), s, last


def hybrid_layer(x, ple, s_gdn, buf_gdn, s_rwkv, shift_rwkv, lw):
    (g_mix, w_in, conv_w, a_log, dt_bias, gdn_norm_g, mu_shift, w0, w2, a0, a2, g2,
     k_k, k_a, r_k, lnx_g, lnx_b, w_up_a, w_up_b, w_o, g_mlp, w_ff1, w_ff2,
     g_ple, w_ple_gate, w_ple) = lw
    h = rmsnorm(x, g_mix)
    proj = h @ w_in
    conv_in, alpha_logit, beta_logit, gate_a, xb, gate_ma, gate_mb = _split(
        proj, [CONV_CH, H_A, H_A, W_A, SHIFT_W, D_MODEL, D_MODEL])
    o_a, s_gdn, buf_gdn = gdn_branch(conv_in, alpha_logit, beta_logit, gate_a, buf_gdn, s_gdn,
                                     conv_w, a_log, dt_bias, gdn_norm_g)
    o_b, s_rwkv, shift_rwkv = rwkv_branch(xb, shift_rwkv, s_rwkv, mu_shift, w0, w2, a0, a2, g2,
                                          k_k, k_a, r_k, lnx_g, lnx_b)
    merged = jax.nn.sigmoid(gate_ma) * (o_a @ w_up_a) + jax.nn.sigmoid(gate_mb) * (o_b @ w_up_b)
    x = x + merged @ w_o
    h = rmsnorm(x, g_mlp)
    x = x + jnp.square(jax.nn.relu(h @ w_ff1)) @ w_ff2
    x = x + jax.nn.sigmoid(rmsnorm(x, g_ple) @ w_ple_gate) * (ple @ w_ple)
    return x, s_gdn, buf_gdn, s_rwkv, shift_rwkv


def run_group(x, p, s_gdn, buf_gdn, s_rwkv, shift_rwkv, params, g_final):
    out_sg, out_buf, out_sr, out_sh = [], [], [], []
    for i in range(DEPTH):
        lw = tuple(w[i] for w in params)
        x, sg, bf, sr, sh = hybrid_layer(x, p[i], s_gdn[i], buf_gdn[i], s_rwkv[i], shift_rwkv[i], lw)
        out_sg.append(sg.astype(s_gdn.dtype))
        out_buf.append(bf.astype(buf_gdn.dtype))
        out_sr.append(sr.astype(s_rwkv.dtype))
        out_sh.append(sh.astype(shift_rwkv.dtype))
    y = rmsnorm(x, g_final)
    return y, jnp.stack(out_sg), jnp.stack(out_buf), jnp.stack(out_sr), jnp.stack(out_sh)


def setup_inputs(seed: int = 0) -> dict:
    key = jax.random.key(seed)
    ks = jax.random.split(key, 40)
    L = DEPTH

    def nrm(k, shape, scale):
        return jax.random.normal(k, shape, F32) * scale

    dt = jnp.exp(jax.random.uniform(ks[10], (L, H_A), F32, math.log(1e-3), math.log(1e-1)))
    return {
        'x_prompt': nrm(ks[0], (BATCH, SEQ, D_MODEL), 1.0),
        'x_sample': nrm(ks[1], (DEC_BATCH, DEC_SEQ, D_MODEL), 1.0),
        'p_prompt': nrm(ks[2], (DEPTH, BATCH, SEQ, D_PLE), 1.0),
        'p_sample': nrm(ks[3], (DEPTH, DEC_BATCH, DEC_SEQ, D_PLE), 1.0),
        'state_gdn': nrm(ks[4], (L, DEC_BATCH, H_A, DK_A, DV_A), 0.1),
        'cache_gdn_conv': nrm(ks[5], (L, DEC_BATCH, CONV_W - 1, CONV_CH), 1.0),
        'state_rwkv': nrm(ks[6], (L, DEC_BATCH, H_B, N_B, N_B), 0.1),
        'cache_rwkv_shift': nrm(ks[7], (L, DEC_BATCH, SHIFT_W), 1.0),
        'g_mix': 1.0 + nrm(ks[8], (L, D_MODEL), 0.02),
        'w_in': nrm(ks[9], (L, D_MODEL, N_IN), D_MODEL ** -0.5),
        'conv_w': nrm(ks[11], (L, CONV_W, CONV_CH), CONV_W ** -0.5),
        'a_log': jnp.log(jax.random.uniform(ks[12], (L, H_A), F32, 1.0, 16.0)),
        'dt_bias': dt + jnp.log(-jnp.expm1(-dt)),
        'gdn_norm_g': 1.0 + nrm(ks[13], (L, DV_A), 0.02),
        'mu_shift': jax.random.uniform(ks[14], (L, SHIFT_W), F32),
        'w0': jax.random.uniform(ks[15], (L, W_B), F32, -6.0, -1.0),
        'w2': nrm(ks[16], (L, LORA_W, W_B), 0.5 * LORA_W ** -0.5),
        'a0': nrm(ks[17], (L, W_B), 0.1),
        'a2': nrm(ks[18], (L, LORA_A, W_B), 0.5 * LORA_A ** -0.5),
        'g2': nrm(ks[19], (L, LORA_G, W_B), LORA_G ** -0.5),
        'k_k': 0.85 + nrm(ks[20], (L, W_B), 0.02),
        'k_a': 1.0 + nrm(ks[21], (L, W_B), 0.02),
        'r_k': nrm(ks[22], (L, H_B, N_B), 0.1),
        'lnx_g': 1.0 + nrm(ks[23], (L, W_B), 0.02),
        'lnx_b': nrm(ks[24], (L, W_B), 0.02),
        'w_up_a': nrm(ks[25], (L, W_A, D_MODEL), W_A ** -0.5),
        'w_up_b': nrm(ks[26], (L, W_B, D_MODEL), W_B ** -0.5),
        'w_o': nrm(ks[27], (L, D_MODEL, D_MODEL), D_MODEL ** -0.5),
        'g_mlp': 1.0 + nrm(ks[28], (L, D_MODEL), 0.02),
        'w_ff1': nrm(ks[29], (L, D_MODEL, D_FF), D_MODEL ** -0.5),
        'w_ff2': nrm(ks[30], (L, D_FF, D_MODEL), D_FF ** -0.5),
        'g_ple': 1.0 + nrm(ks[31], (L, D_MODEL), 0.02),
        'w_ple_gate': nrm(ks[32], (L, D_MODEL, D_MODEL), D_MODEL ** -0.5),
        'w_ple': nrm(ks[33], (L, D_PLE, D_MODEL), D_PLE ** -0.5),
        'g_final': 1.0 + nrm(ks[34], (D_MODEL,), 0.02),
    }


def reference(x_prompt, x_sample, p_prompt, p_sample, state_gdn, cache_gdn_conv, state_rwkv,
              cache_rwkv_shift, g_mix, w_in, conv_w, a_log, dt_bias, gdn_norm_g, mu_shift, w0, w2,
              a0, a2, g2, k_k, k_a, r_k, lnx_g, lnx_b, w_up_a, w_up_b, w_o, g_mlp, w_ff1, w_ff2,
              g_ple, w_ple_gate, w_ple, g_final):
    params = (g_mix, w_in, conv_w, a_log, dt_bias, gdn_norm_g, mu_shift, w0, w2, a0, a2, g2,
              k_k, k_a, r_k, lnx_g, lnx_b, w_up_a, w_up_b, w_o, g_mlp, w_ff1, w_ff2,
              g_ple, w_ple_gate, w_ple)
    bp = x_prompt.shape[0]
    dt_p = x_prompt.dtype
    z_sg = jnp.zeros((DEPTH, bp, H_A, DK_A, DV_A), dt_p)
    z_buf = jnp.zeros((DEPTH, bp, CONV_W - 1, CONV_CH), dt_p)
    z_sr = jnp.zeros((DEPTH, bp, H_B, N_B, N_B), dt_p)
    z_sh = jnp.zeros((DEPTH, bp, SHIFT_W), dt_p)
    y_prompt, sg_p, buf_p, sr_p, sh_p = run_group(x_prompt, p_prompt, z_sg, z_buf, z_sr, z_sh,
                                                  params, g_final)
    y_sample, sg_s, buf_s, sr_s, sh_s = run_group(x_sample, p_sample, state_gdn, cache_gdn_conv,
                                                  state_rwkv, cache_rwkv_shift, params, g_final)
    return (y_prompt, y_sample, sg_p, buf_p, sr_p, sh_p, sg_s, buf_s, sr_s, sh_s)
```

```python
import functools
import math

import jax
import jax.numpy as jnp
from jax import lax
from jax.experimental import pallas as pl
from jax.experimental.pallas import tpu as pltpu

F32 = jnp.float32
BF16 = jnp.bfloat16

D_MODEL = 2048
H_A, DK_A, DV_A = 8, 128, 128
W_A = H_A * DV_A
CONV_W = 4
CONV_CH = 3 * W_A
H_B, N_B = 16, 64
W_B = H_B * N_B
LORA_W, LORA_A, LORA_G = 96, 96, 256
SHIFT_W = 3 * W_B + LORA_W + LORA_A + LORA_G
D_FF = 4 * D_MODEL
D_PLE = 256
EPS = 1e-6
LNX_EPS = 64e-5
CHUNK = 64

LANES = 128
SUBLANES = 8
VMEM_LIMIT = 52 * 2 ** 20

XB_PAD = 3 * W_B + 2 * LANES + LORA_G
AB_BLOCK = 3 * W_B // LANES
ALPHA_LANE = LORA_W
BETA_LANE = LORA_W + H_A

NN = ((1,), (0,))
NT = ((1,), (1,))
TN = ((0,), (0,))


def _dg(a, b, dims):
    return lax.dot_general(a, b, (dims, ((), ())), preferred_element_type=F32)


def _dot1(a, b, dims=NN):
    return _dg(a.astype(BF16), b.astype(BF16), dims)


def _split2(a):
    hi = a.astype(BF16)
    lo = (a - hi.astype(F32)).astype(BF16)
    return hi, lo


def _dot3(a, b, dims=NN):
    ah, al = _split2(a)
    bh, bl = _split2(b)
    return _dg(ah, bh, dims) + (_dg(ah, bl, dims) + _dg(al, bh, dims))


def _cumsum_rows(x, c):
    i = lax.broadcasted_iota(jnp.int32, (c, c), 0)
    j = lax.broadcasted_iota(jnp.int32, (c, c), 1)
    tri = jnp.where(i >= j, 1.0, 0.0).astype(BF16)
    h1 = x.astype(BF16)
    r1 = x - h1.astype(F32)
    h2 = r1.astype(BF16)
    h3 = (r1 - h2.astype(F32)).astype(BF16)
    return _dg(tri, h1, NN) + (_dg(tri, h2, NN) + _dg(tri, h3, NN))


def _tri_inv(a, c):
    i = lax.broadcasted_iota(jnp.int32, (c, c), 0)
    j = lax.broadcasted_iota(jnp.int32, (c, c), 1)
    x = jnp.where(i == j, 1.0, 0.0) - jnp.where((i ^ j) == 1, a, 0.0)
    s, ls = 2, 1
    while s < c:
        l_s = jnp.where(((i ^ j) >> ls) == 1, a, 0.0)
        x = x - _dot3(_dot3(x, l_s), x)
        s, ls = 2 * s, ls + 1
    return x


def _softplus(x):
    return jnp.maximum(x, 0.0) + jnp.log1p(jnp.exp(-jnp.abs(x)))


def _sigmoid(x):
    return jax.nn.sigmoid(x)


def _rms(x, g):
    return x * lax.rsqrt(jnp.mean(x * x, axis=-1, keepdims=True) + EPS) * g


def _params(*sem):
    return pltpu.CompilerParams(dimension_semantics=sem, vmem_limit_bytes=VMEM_LIMIT)


def _norm_matmul_kernel(x_ref, g_ref, w_ref, o_ref, h_scr):
    @pl.when(pl.program_id(1) == 0)
    def _():
        h_scr[...] = _rms(x_ref[...], g_ref[...]).astype(BF16)

    o_ref[...] = _dg(h_scr[...], w_ref[...], NN).astype(o_ref.dtype)


def norm_matmul(x, g, w, tm, tn, out_dtype=F32):
    m, k = x.shape
    n = w.shape[1]
    return pl.pallas_call(
        _norm_matmul_kernel,
        grid=(m // tm, n // tn),
        in_specs=[pl.BlockSpec((tm, k), lambda i, j: (i, 0)),
                  pl.BlockSpec((1, k), lambda i, j: (0, 0)),
                  pl.BlockSpec((k, tn), lambda i, j: (0, j))],
        out_specs=pl.BlockSpec((tm, tn), lambda i, j: (i, j)),
        out_shape=jax.ShapeDtypeStruct((m, n), out_dtype),
        scratch_shapes=[pltpu.VMEM((tm, k), BF16)],
        compiler_params=_params("parallel", "arbitrary"),
        name="norm_matmul",
    )(x, g, w)


def _merge_kernel(x_ref, oa_ref, ob_ref, gma_ref, gmb_ref, wa_ref, wb_ref, wo_ref, o_ref, acc):
    n = pl.program_id(1)
    merged = (_sigmoid(gma_ref[...]) * _dg(oa_ref[...], wa_ref[...], NN)
              + _sigmoid(gmb_ref[...]) * _dg(ob_ref[...], wb_ref[...], NN))
    part = _dg(merged.astype(BF16), wo_ref[...], NN)

    @pl.when(n == 0)
    def _():
        acc[...] = part

    @pl.when(n > 0)
    def _():
        acc[...] += part

    @pl.when(n == pl.num_programs(1) - 1)
    def _():
        o_ref[...] = x_ref[...] + acc[...]


def merge_out(x, o_a, o_b, p2, w_up_a, w_up_b, w_o, tm, tn):
    m = x.shape[0]
    nt = D_MODEL // tn
    return pl.pallas_call(
        _merge_kernel,
        grid=(m // tm, nt),
        in_specs=[pl.BlockSpec((tm, D_MODEL), lambda i, n: (i, 0)),
                  pl.BlockSpec((tm, W_A), lambda i, n: (i, 0)),
                  pl.BlockSpec((tm, W_B), lambda i, n: (i, 0)),
                  pl.BlockSpec((tm, tn), lambda i, n: (i, n)),
                  pl.BlockSpec((tm, tn), lambda i, n: (i, nt + n)),
                  pl.BlockSpec((W_A, tn), lambda i, n: (0, n)),
                  pl.BlockSpec((W_B, tn), lambda i, n: (0, n)),
                  pl.BlockSpec((tn, D_MODEL), lambda i, n: (n, 0))],
        out_specs=pl.BlockSpec((tm, D_MODEL), lambda i, n: (i, 0)),
        out_shape=jax.ShapeDtypeStruct((m, D_MODEL), F32),
        scratch_shapes=[pltpu.VMEM((tm, D_MODEL), F32)],
        compiler_params=_params("parallel", "arbitrary"),
        name="merge_out",
    )(x, o_a, o_b, p2, p2, w_up_a, w_up_b, w_o)


def _mlp_kernel(x_ref, g_ref, w1_ref, w2_ref, o_ref, h_scr, acc):
    f = pl.program_id(1)

    @pl.when(f == 0)
    def _():
        h_scr[...] = _rms(x_ref[...], g_ref[...]).astype(BF16)

    a = jnp.maximum(_dg(h_scr[...], w1_ref[...], NN), 0.0)
    part = _dg((a * a).astype(BF16), w2_ref[...], NN)

    @pl.when(f == 0)
    def _():
        acc[...] = part

    @pl.when(f > 0)
    def _():
        acc[...] += part

    @pl.when(f == pl.num_programs(1) - 1)
    def _():
        o_ref[...] = x_ref[...] + acc[...]


def mlp(x, g, w1, w2, tm, tf):
    m = x.shape[0]
    return pl.pallas_call(
        _mlp_kernel,
        grid=(m // tm, D_FF // tf),
        in_specs=[pl.BlockSpec((tm, D_MODEL), lambda i, f: (i, 0)),
                  pl.BlockSpec((1, D_MODEL), lambda i, f: (0, 0)),
                  pl.BlockSpec((D_MODEL, tf), lambda i, f: (0, f)),
                  pl.BlockSpec((tf, D_MODEL), lambda i, f: (f, 0))],
        out_specs=pl.BlockSpec((tm, D_MODEL), lambda i, f: (i, 0)),
        out_shape=jax.ShapeDtypeStruct((m, D_MODEL), F32),
        scratch_shapes=[pltpu.VMEM((tm, D_MODEL), BF16), pltpu.VMEM((tm, D_MODEL), F32)],
        compiler_params=_params("parallel", "arbitrary"),
        name="mlp",
    )(x, g, w1, w2)


def _ple_final_kernel(x_ref, p_ref, gp_ref, wg_ref, wp_ref, gf_ref, o_ref):
    x = x_ref[...]
    gate = _sigmoid(_dg(_rms(x, gp_ref[...]).astype(BF16), wg_ref[...], NN))
    x = x + gate * _dg(p_ref[...].astype(BF16), wp_ref[...], NN)
    o_ref[...] = _rms(x, gf_ref[...])


def ple_final(x, ple, g_ple, w_gate, w_ple, g_final, tm):
    m = x.shape[0]
    return pl.pallas_call(
        _ple_final_kernel,
        grid=(m // tm,),
        in_specs=[pl.BlockSpec((tm, D_MODEL), lambda i: (i, 0)),
                  pl.BlockSpec((tm, D_PLE), lambda i: (i, 0)),
                  pl.BlockSpec((1, D_MODEL), lambda i: (0, 0)),
                  pl.BlockSpec((D_MODEL, D_MODEL), lambda i: (0, 0)),
                  pl.BlockSpec((D_PLE, D_MODEL), lambda i: (0, 0)),
                  pl.BlockSpec((1, D_MODEL), lambda i: (0, 0))],
        out_specs=pl.BlockSpec((tm, D_MODEL), lambda i: (i, 0)),
        out_shape=jax.ShapeDtypeStruct((m, D_MODEL), F32),
        compiler_params=_params("parallel"),
        name="ple_final",
    )(x, ple, g_ple, w_gate, w_ple, g_final)


def _gdn_kernel(q_ref, k_ref, v_ref, gate_ref, ab_ref, hq_ref, hk_ref, hv_ref,
                cq_ref, ck_ref, cv_ref, alog_ref, dt_ref, ng_ref, s0_ref,
                o_ref, sout_ref, qs, ks, vs, s_scr, *, c, hb):
    g_idx = pl.program_id(1)
    t_idx = pl.program_id(2)
    hist = SUBLANES

    @pl.when(t_idx == 0)
    def _():
        s_scr[...] = s0_ref[0]
        qs[0:hist, :] = hq_ref[0]
        ks[0:hist, :] = hk_ref[0]
        vs[0:hist, :] = hv_ref[0]

    def conv_silu(scr, x_ref, w_ref):
        scr[hist:hist + c, :] = x_ref[...]
        out = scr[hist - 3:hist - 3 + c, :] * w_ref[0:1, :]
        for j in range(1, CONV_W):
            out = out + scr[hist - 3 + j:hist - 3 + j + c, :] * w_ref[j:j + 1, :]
        scr[0:hist, :] = scr[c:c + hist, :]
        return out * _sigmoid(out)

    qc = conv_silu(qs, q_ref, cq_ref)
    kc = conv_silu(ks, k_ref, ck_ref)
    vc = conv_silu(vs, v_ref, cv_ref)

    ab = ab_ref[...]
    g_all = -jnp.exp(alog_ref[...]) * _softplus(ab + dt_ref[...])
    beta_all = _sigmoid(ab)
    gcum = _cumsum_rows(g_all, c)
    gcum_t = gcum.T
    lane = lax.broadcasted_iota(jnp.int32, (c, LANES), 1)
    sub_t = lax.broadcasted_iota(jnp.int32, (LANES, c), 0)
    i = lax.broadcasted_iota(jnp.int32, (c, c), 0)
    j = lax.broadcasted_iota(jnp.int32, (c, c), 1)
    incl = i >= j
    strict = i > j
    last_row = lax.broadcasted_iota(jnp.int32, (c, 1), 0) == c - 1
    gate = gate_ref[...]
    ng = ng_ref[...]

    for h in range(hb):
        head = g_idx * hb + h
        sl = slice(h * DK_A, (h + 1) * DK_A)
        gc = jnp.sum(jnp.where(lane == ALPHA_LANE + head, gcum, 0.0), axis=1, keepdims=True)
        gr = jnp.sum(jnp.where(sub_t == ALPHA_LANE + head, gcum_t, 0.0), axis=0, keepdims=True)
        bc = jnp.sum(jnp.where(lane == BETA_LANE + head, beta_all, 0.0), axis=1, keepdims=True)
        qh = qc[:, sl]
        kh = kc[:, sl]
        vh = vc[:, sl]
        qh = qh * lax.rsqrt(jnp.sum(qh * qh, axis=-1, keepdims=True) + EPS) * (DK_A ** -0.5)
        kh = kh * lax.rsqrt(jnp.sum(kh * kh, axis=-1, keepdims=True) + EPS)

        dec = jnp.exp(jnp.where(incl, gc - gr, -jnp.inf))
        kb = kh * bc
        a_mat = jnp.where(strict, _dot3(kb, kh, NT) * dec, 0.0)
        t_inv = _tri_inv(a_mat, c)
        eg = jnp.exp(gc)
        uw = _dot1(t_inv, jnp.concatenate([vh * bc, kb * eg], axis=1))
        u = uw[:, :DV_A]
        w = uw[:, DV_A:]
        qk = jnp.where(incl, _dot1(qh, kh, NT) * dec, 0.0)
        g_last = jnp.sum(jnp.where(last_row, gc, 0.0), axis=0, keepdims=True)
        kd = kh * jnp.exp(g_last - gc)
        s = s_scr[h]
        ws_qs = _dot1(jnp.concatenate([w, qh * eg], axis=0), s)
        v_new = u - ws_qs[:c]
        o = ws_qs[c:] + _dot1(qk, v_new)
        s_scr[h] = s * jnp.exp(g_last) + _dot1(kd, v_new, TN)

        o = _rms(o, ng)
        gh = gate[:, sl]
        o_ref[:, sl] = (o * (gh * _sigmoid(gh))).astype(o_ref.dtype)

    @pl.when(t_idx == pl.num_programs(2) - 1)
    def _():
        sout_ref[0] = s_scr[...]


def gdn_branch(p1, p3, hist8, conv_w, alog_vec, dt_vec, norm_g, s0, b, t, c, hb):
    nc = t // c
    ng = H_A // hb
    gw = hb * DK_A
    row = lambda bi, g, ti: bi * nc + ti
    seg = lambda k: pl.BlockSpec((c, gw), lambda bi, g, ti: (row(bi, g, ti), k * ng + g))
    hist = lambda k: pl.BlockSpec((1, SUBLANES, gw), lambda bi, g, ti: (bi, 0, k * ng + g))
    cw = lambda k: pl.BlockSpec((CONV_W, gw), lambda bi, g, ti: (0, k * ng + g))
    vec = pl.BlockSpec((1, LANES), lambda bi, g, ti: (0, 0))
    state = pl.BlockSpec((1, hb, DK_A, DV_A), lambda bi, g, ti: (bi, g, 0, 0))
    return pl.pallas_call(
        functools.partial(_gdn_kernel, c=c, hb=hb),
        grid=(b, ng, nc),
        in_specs=[seg(0), seg(1), seg(2), seg(3),
                  pl.BlockSpec((c, LANES), lambda bi, g, ti: (row(bi, g, ti), AB_BLOCK)),
                  hist(0), hist(1), hist(2), cw(0), cw(1), cw(2), vec, vec, vec, state],
        out_specs=[pl.BlockSpec((c, gw), lambda bi, g, ti: (row(bi, g, ti), g)), state],
        out_shape=[jax.ShapeDtypeStruct((b * t, W_A), BF16),
                   jax.ShapeDtypeStruct((b, H_A, DK_A, DV_A), F32)],
        scratch_shapes=[pltpu.VMEM((c + SUBLANES, gw), F32)] * 3 + [pltpu.VMEM((hb, DK_A, DV_A), F32)],
        compiler_params=_params("parallel", "parallel", "arbitrary"),
        name="gdn_branch",
    )(p1, p1, p1, p1, p3, hist8, hist8, hist8, conv_w, conv_w, conv_w, alog_vec, dt_vec, norm_g, s0)


def _rwkv_kernel(r_ref, k_ref, v_ref, tail_ref, pr_ref, pk_ref, pv_ref, pt_ref,
                 mr_ref, mk_ref, mv_ref, mt_ref, w0_ref, a0_ref, kk_ref, ka_ref, rk_ref,
                 lg_ref, lb_ref, w2_ref, a2_ref, g2_ref, s0_ref,
                 o_ref, sout_ref, prev_r, prev_k, prev_v, prev_t, s_scr, *, c, pg):
    t_idx = pl.program_id(2)

    @pl.when(t_idx == 0)
    def _():
        prev_r[...] = pr_ref[0]
        prev_k[...] = pk_ref[0]
        prev_v[...] = pv_ref[0]
        prev_t[...] = pt_ref[0]
        z = jnp.zeros((N_B, N_B), F32)
        for p in range(pg):
            s_a = s0_ref[0, 2 * p]
            s_b = s0_ref[0, 2 * p + 1]
            s_scr[p] = jnp.concatenate([jnp.concatenate([s_a, z], axis=1),
                                        jnp.concatenate([z, s_b], axis=1)], axis=0)

    def mix(x_ref, prev, mu_ref):
        x = x_ref[...]
        row = lax.broadcasted_iota(jnp.int32, x.shape, 0)
        shifted = jnp.where(row == 0, prev[...], pltpu.roll(x, 1, axis=0))
        prev[...] = x_ref[c - 1:c, :]
        return x + (shifted - x) * mu_ref[...]

    r = mix(r_ref, prev_r, mr_ref)
    k = mix(k_ref, prev_k, mk_ref)
    v = mix(v_ref, prev_v, mv_ref)
    tail = mix(tail_ref, prev_t, mt_ref)
    xw = tail[:, 0:LANES]
    xa = tail[:, LANES:2 * LANES]
    xg = tail[:, 2 * LANES:]

    w_log = -_softplus(-(w0_ref[...] + _dot1(jnp.tanh(xw), w2_ref[...]))) - 0.5
    lw = -jnp.exp(w_log)
    a = _sigmoid(a0_ref[...] + _dot1(xa, a2_ref[...]))
    gate = _dot1(_sigmoid(xg), g2_ref[...])

    gw = pg * LANES
    fp = (lax.broadcasted_iota(jnp.int32, (c, LANES), 1) & N_B) == 0

    def seg_sum(x):
        outs = []
        for p in range(pg):
            xp = x[:, p * LANES:(p + 1) * LANES]
            s_a =jnp.sum(jnp.where(fp, xp, 0.0), axis=-1, keepdims=True)
            s_b = jnp.sum(jnp.where(fp, 0.0, xp), axis=-1, keepdims=True)
            outs.append(jnp.where(fp, s_a, s_b))
        return jnp.concatenate(outs, axis=1) if pg > 1 else outs[0]

    kkx = k * kk_ref[...]
    kk = kkx * lax.rsqrt(seg_sum(kkx * kkx) + EPS)
    kh = k * (1.0 + (a - 1.0) * ka_ref[...])
    bb = kk * a
    cw = _cumsum_rows(lw, c)
    e_inc = jnp.exp(cw)
    e_neg = jnp.exp(-cw)
    row_g = lax.broadcasted_iota(jnp.int32, (c, gw), 0)
    cw_last = jnp.sum(jnp.where(row_g == c - 1, cw, 0.0), axis=0, keepdims=True)
    e_last = jnp.exp(cw_last - cw)
    rt = r * e_inc
    kt = kh * e_neg
    bt = bb * e_neg
    at = kk * jnp.exp(cw - lw)
    k_hat = kh * e_last
    b_hat = bb * e_last
    e_end = jnp.exp(cw_last)

    i = lax.broadcasted_iota(jnp.int32, (c, c), 0)
    j = lax.broadcasted_iota(jnp.int32, (c, c), 1)
    incl = i >= j
    strict = i > j
    bi = lax.broadcasted_iota(jnp.int32, (LANES, LANES), 0)
    bj = lax.broadcasted_iota(jnp.int32, (LANES, LANES), 1)
    same_head = ((bi ^ bj) & N_B) == 0

    ys = []
    for p in range(pg):
        sl = slice(p * LANES, (p + 1) * LANES)
        s = s_scr[p]
        at_p, rt_p, kt_p, bt_p, v_p = at[:, sl], rt[:, sl], kt[:, sl], bt[:, sl], v[:, sl]
        a_s = _dot1(at_p, s, NT)
        r_s = _dot1(rt_p, s, NT)
        pz, yz = [], []
        mats = []
        for e in range(2):
            m = fp if e == 0 else jnp.logical_not(fp)
            lhs = jnp.concatenate([jnp.where(m, at_p, 0.0), jnp.where(m, rt_p, 0.0)], axis=0)
            xb_ = _dot3(lhs, bt_p, NT)
            xk_ = _dot3(lhs, kt_p, NT)
            l_ab = jnp.where(strict, xb_[:c], 0.0)
            l_ak = jnp.where(strict, xk_[:c], 0.0)
            r_b = jnp.where(incl, xb_[c:], 0.0)
            r_k = jnp.where(incl, xk_[c:], 0.0)
            t_inv = _tri_inv(l_ab, c)
            pz.append(_dot1(t_inv, a_s + _dot1(l_ak, v_p)))
            mats.append((r_k, r_b))
        p_all = jnp.where(fp, pz[0], pz[1])
        for e in range(2):
            r_k, r_b = mats[e]
            yz.append(r_s + _dot1(r_k, v_p) - _dot1(r_b, p_all))
        ys.append(jnp.where(fp, yz[0], yz[1]))
        upd = _dot1(jnp.concatenate([v_p, p_all], axis=0),
                    jnp.concatenate([k_hat[:, sl], -b_hat[:, sl]], axis=0), TN)
        s_scr[p] = s * e_end[:, sl] + jnp.where(same_head, upd, 0.0)

    y = jnp.concatenate(ys, axis=1) if pg > 1 else ys[0]
    inv_n = 1.0 / N_B
    yc = y - seg_sum(y) * inv_n
    yn = yc * lax.rsqrt(seg_sum(yc * yc) * inv_n + LNX_EPS)
    yn = yn * lg_ref[...] + lb_ref[...]
    bonus = seg_sum(r * kh * rk_ref[...]) * v
    o_ref[...] = ((yn + bonus) * gate).astype(o_ref.dtype)

    @pl.when(t_idx == pl.num_programs(2) - 1)
    def _():
        for p in range(pg):
            s = s_scr[p]
            sout_ref[0, 2 * p] = s[:N_B, :N_B]
            sout_ref[0, 2 * p + 1] = s[N_B:, N_B:]


def rwkv_branch(p3, prev, mu, w0, a0, k_k, k_a, r_k, lnx_g, lnx_b, w2p, a2p, g2, s0, b, t, c, pg):
    nc = t // c
    gw = pg * LANES
    ng = W_B // gw
    tw = XB_PAD - 3 * W_B
    tail_blk = 3 * W_B // tw
    row = lambda bi, g, ti: bi * nc + ti
    seg = lambda k: pl.BlockSpec((c, gw), lambda bi, g, ti: (row(bi, g, ti), k * ng + g))
    pseg = lambda k: pl.BlockSpec((1, 1, gw), lambda bi, g, ti: (bi, 0, k * ng + g))
    mseg = lambda k: pl.BlockSpec((1, gw), lambda bi, g, ti: (0, k * ng + g))
    chan = pl.BlockSpec((1, gw), lambda bi, g, ti: (0, g))
    state = pl.BlockSpec((1, 2 * pg, N_B, N_B), lambda bi, g, ti: (bi, g, 0, 0))
    return pl.pallas_call(
        functools.partial(_rwkv_kernel, c=c, pg=pg),
        grid=(b, ng, nc),
        in_specs=[seg(0), seg(1), seg(2),
                  pl.BlockSpec((c, tw), lambda bi, g, ti: (row(bi, g, ti), tail_blk)),
                  pseg(0), pseg(1), pseg(2),
                  pl.BlockSpec((1, 1, tw), lambda bi, g, ti: (bi, 0, tail_blk)),
                  mseg(0), mseg(1), mseg(2),
                  pl.BlockSpec((1, tw), lambda bi, g, ti: (0, tail_blk)),
                  chan, chan, chan, chan, chan, chan, chan,
                  pl.BlockSpec((LANES, gw), lambda bi, g, ti: (0, g)),
                  pl.BlockSpec((LANES, gw), lambda bi, g, ti: (0, g)),
                  pl.BlockSpec((LORA_G, gw), lambda bi, g, ti: (0, g)),
                  state],
        out_specs=[pl.BlockSpec((c, gw), lambda bi, g, ti: (row(bi, g, ti), g)), state],
        out_shape=[jax.ShapeDtypeStruct((b * t, W_B), BF16),
                   jax.ShapeDtypeStruct((b, H_B, N_B, N_B), F32)],
        scratch_shapes=[pltpu.VMEM((1, gw), F32)] * 3 + [pltpu.VMEM((1, tw), F32),
                                                        pltpu.VMEM((pg, LANES, LANES), F32)],
        compiler_params=_params("parallel", "parallel", "arbitrary"),
        name="rwkv_branch",
    )(p3, p3, p3, p3, prev, prev, prev, prev, mu, mu, mu, mu,
      w0, a0, k_k, k_a, r_k, lnx_g, lnx_b, w2p, a2p, g2, s0)


def _pad_xb(a, ab=None):
    lead = a.shape[:-1]
    z = lambda n: jnp.zeros(lead + (n,), a.dtype)
    o_w, o_a, o_g = 3 * W_B, 3 * W_B + LORA_W, 3 * W_B + LORA_W + LORA_A
    mid = z(2 * H_A) if ab is None else ab
    return jnp.concatenate([a[..., :o_a], mid, z(LANES - LORA_W - 2 * H_A),
                            a[..., o_a:o_g], z(LANES - LORA_A), a[..., o_g:]], axis=-1)


def _unpad_xb(a):
    o_a = 3 * W_B + LORA_W
    return jnp.concatenate([a[..., :o_a], a[..., 3 * W_B + LANES:3 * W_B + LANES + LORA_A],
                            a[..., 3 * W_B + 2 * LANES:]], axis=-1)


def _pad_rows(w, n):
    return jnp.concatenate([w, jnp.zeros((n - w.shape[0],) + w.shape[1:], w.dtype)], axis=0)


def _lane_vec(v, start):
    return jnp.zeros((1, LANES), F32).at[0, start:start + v.shape[0]].set(v)


def _prepare(lw):
    (g_mix, w_in, conv_w, a_log, dt_bias, gdn_norm_g, mu_shift, w0, w2, a0, a2, g2,
     k_k, k_a, r_k, lnx_g, lnx_b, w_up_a, w_up_b, w_o, g_mlp, w_ff1, w_ff2,
     g_ple, w_ple_gate, w_ple) = lw
    o_alpha = CONV_CH
    o_gate_a = CONV_CH + 2 * H_A
    o_xb = o_gate_a + W_A
    o_gm = o_xb + SHIFT_W
    row = lambda v: v.reshape(1, -1)
    return dict(
        g_mix=row(g_mix),
        w1=jnp.concatenate([w_in[:, :CONV_CH], w_in[:, o_gate_a:o_xb]], axis=1).astype(BF16),
        w2g=w_in[:, o_gm:].astype(BF16),
        w3=_pad_xb(w_in[:, o_xb:o_gm], w_in[:, o_alpha:o_gate_a]).astype(BF16),
        conv_w=conv_w,
        alog_vec=_lane_vec(a_log, ALPHA_LANE), dt_vec=_lane_vec(dt_bias, ALPHA_LANE),
        gdn_norm_g=row(gdn_norm_g),
        mu=row(_pad_xb(mu_shift)),
        w0=row(w0), a0=row(a0), k_k=row(k_k), k_a=row(k_a), r_k=row(r_k.reshape(-1)),
        lnx_g=row(lnx_g), lnx_b=row(lnx_b),
        w2p=_pad_rows(w2, LANES).astype(BF16), a2p=_pad_rows(a2, LANES).astype(BF16), g2=g2.astype(BF16),
        w_up_a=w_up_a.astype(BF16), w_up_b=w_up_b.astype(BF16), w_o=w_o.astype(BF16),
        g_mlp=row(g_mlp), w_ff1=w_ff1.astype(BF16), w_ff2=w_ff2.astype(BF16),
        g_ple=row(g_ple), w_ple_gate=w_ple_gate.astype(BF16), w_ple=w_ple.astype(BF16),
    )


def _tiles(m):
    tm_proj = min(m, 1024)
    tm = min(m, 512)
    return tm_proj, tm


def _run_layer(x, ple, s_gdn, buf_gdn, s_rwkv, shift_rwkv, pw, g_final):
    b, t, _ = x.shape
    m = b * t
    c = min(CHUNK, t)
    assert t % c == 0 and c % SUBLANES == 0 and t >= CONV_W - 1
    tm_proj, tm = _tiles(m)
    x2 = x.reshape(m, D_MODEL)

    p1 = norm_matmul(x2, pw["g_mix"], pw["w1"], tm_proj, 512)
    p2 = norm_matmul(x2, pw["g_mix"], pw["w2g"], tm_proj, 512)
    p3 = norm_matmul(x2, pw["g_mix"], pw["w3"], tm_proj, 512)

    hist8 = jnp.concatenate([jnp.zeros((b, SUBLANES - (CONV_W - 1), CONV_CH), F32), buf_gdn], axis=1)
    o_a, s_gdn_new = gdn_branch(p1, p3, hist8, pw["conv_w"], pw["alog_vec"], pw["dt_vec"],
                                pw["gdn_norm_g"], s_gdn, b, t, c, hb=2)
    prev = _pad_xb(shift_rwkv).reshape(b, 1, XB_PAD)
    o_b, s_rwkv_new = rwkv_branch(p3, prev, pw["mu"], pw["w0"], pw["a0"], pw["k_k"], pw["k_a"], pw["r_k"],
                                  pw["lnx_g"], pw["lnx_b"], pw["w2p"], pw["a2p"], pw["g2"], s_rwkv,
                                  b, t, c, pg=2)

    x1 = merge_out(x2, o_a, o_b, p2, pw["w_up_a"], pw["w_up_b"], pw["w_o"], tm, 512)
    x3 = mlp(x1, pw["g_mlp"], pw["w_ff1"], pw["w_ff2"], tm, 512)
    y = ple_final(x3, ple.reshape(m, D_PLE), pw["g_ple"], pw["w_ple_gate"], pw["w_ple"], g_final, min(m, 256))

    new_buf = p1.reshape(b, t, 4 * W_A)[:, t - (CONV_W - 1):, :CONV_CH]
    new_shift = _unpad_xb(p3.reshape(b, t, XB_PAD)[:, t - 1])
    return y.reshape(b, t, D_MODEL), s_gdn_new, new_buf, s_rwkv_new, new_shift


def kernel(x_prompt, x_sample, p_prompt, p_sample, state_gdn, cache_gdn_conv, state_rwkv, cache_rwkv_shift, g_mix, w_in, conv_w, a_log, dt_bias, gdn_norm_g, mu_shift, w0, w2, a0, a2, g2, k_k, k_a, r_k, lnx_g, lnx_b, w_up_a, w_up_b, w_o, g_mlp, w_ff1, w_ff2, g_ple, w_ple_gate, w_ple, g_final):
    params = (g_mix, w_in, conv_w, a_log, dt_bias, gdn_norm_g, mu_shift, w0, w2, a0, a2, g2,
              k_k, k_a, r_k, lnx_g, lnx_b, w_up_a, w_up_b, w_o, g_mlp, w_ff1, w_ff2,
              g_ple, w_ple_gate, w_ple)
    depth = w_in.shape[0]
    assert depth == 1
    pw = _prepare(tuple(w[0] for w in params))
    gf = g_final.reshape(1, -1)
    bp = x_prompt.shape[0]
    z_sg = jnp.zeros((bp, H_A, DK_A, DV_A), F32)
    z_buf = jnp.zeros((bp, CONV_W - 1, CONV_CH), F32)
    z_sr = jnp.zeros((bp, H_B, N_B, N_B), F32)
    z_sh = jnp.zeros((bp, SHIFT_W), F32)
    y_p, sg_p, buf_p, sr_p, sh_p = _run_layer(x_prompt, p_prompt[0], z_sg, z_buf, z_sr, z_sh, pw, gf)
    y_s, sg_s, buf_s, sr_s, sh_s = _run_layer(x_sample, p_sample[0], state_gdn[0], cache_gdn_conv[0],
                                              state_rwkv[0], cache_rwkv_shift[0], pw, gf)
    st = lambda a: a[None]
    return (y_p, y_s, st(sg_p), st(buf_p), st(sr_p), st(sh_p), st(sg_s), st(buf_s), st(sr_s), st(sh_s))
```

```python
import functools
import math

import jax
import jax.numpy as jnp
from jax import lax
from jax.experimental import pallas as pl
from jax.experimental.pallas import tpu as pltpu

F32 = jnp.float32
BF16 = jnp.bfloat16

D_MODEL = 2048
H_A, DK_A, DV_A = 8, 128, 128
W_A = H_A * DV_A
CONV_W = 4
CONV_CH = 3 * W_A
H_B, N_B = 16, 64
W_B = H_B * N_B
LORA_W, LORA_A, LORA_G = 96, 96, 256
SHIFT_W = 3 * W_B + LORA_W + LORA_A + LORA_G
D_FF = 4 * D_MODEL
D_PLE = 256
EPS = 1e-6
LNX_EPS = 64e-5
CHUNK = 64
GDN_HB = 8
RWKV_PG = 8

LANES = 128
SUBLANES = 8
VMEM_LIMIT = 52 * 2 ** 20

XB_PAD = 3 * W_B + 2 * LANES + LORA_G
AB_BLOCK = 3 * W_B // LANES
ALPHA_LANE = LORA_W
BETA_LANE = LORA_W + H_A

NN = ((1,), (0,))
NT = ((1,), (1,))
TN = ((0,), (0,))


def _dg(a, b, dims):
    return lax.dot_general(a, b, (dims, ((), ())), preferred_element_type=F32)


def _dot1(a, b, dims=NN):
    return _dg(a.astype(BF16), b.astype(BF16), dims)


def _split2(a):
    hi = a.astype(BF16)
    lo = (a - hi.astype(F32)).astype(BF16)
    return hi, lo


def _dot3_many(a_list, b_list, dims=NN):
    sa = [_split2(a) for a in a_list]
    sb = [_split2(b) for b in b_list]
    hh = [_dg(x[0], y[0], dims) for x, y in zip(sa, sb)]
    hl = [_dg(x[0], y[1], dims) for x, y in zip(sa, sb)]
    lh = [_dg(x[1], y[0], dims) for x, y in zip(sa, sb)]
    return [p + (q + r) for p, q, r in zip(hh, hl, lh)]


def _cumsum_rows(x, c):
    i = lax.broadcasted_iota(jnp.int32, (c, c), 0)
    j = lax.broadcasted_iota(jnp.int32, (c, c), 1)
    tri = jnp.where(i >= j, 1.0, 0.0).astype(BF16)
    h1 = x.astype(BF16)
    r1 = x - h1.astype(F32)
    h2 = r1.astype(BF16)
    h3 = (r1 - h2.astype(F32)).astype(BF16)
    return _dg(tri, h1, NN) + (_dg(tri, h2, NN) + _dg(tri, h3, NN))


def _tri_inv_many(a_list, c):
    i = lax.broadcasted_iota(jnp.int32, (c, c), 0)
    j = lax.broadcasted_iota(jnp.int32, (c, c), 1)
    eye = jnp.where(i == j, 1.0, 0.0)
    xs = [eye - jnp.where((i ^ j) == 1, a, 0.0) for a in a_list]
    s, ls = 2, 1
    while s < c:
        mask = ((i ^ j) >> ls) == 1
        l_s = [jnp.where(mask, a, 0.0) for a in a_list]
        xlx = _dot3_many(_dot3_many(xs, l_s), xs)
        xs = [x - y for x, y in zip(xs, xlx)]
        s, ls = 2 * s, ls + 1
    return xs


def _softplus(x):
    return jnp.maximum(x, 0.0) + jnp.log1p(jnp.exp(-jnp.abs(x)))


def _sigmoid(x):
    return jax.nn.sigmoid(x)


def _rms(x, g):
    return x * lax.rsqrt(jnp.mean(x * x, axis=-1, keepdims=True) + EPS) * g


def _params(*sem):
    return pltpu.CompilerParams(dimension_semantics=sem, vmem_limit_bytes=VMEM_LIMIT)


def _norm_matmul_kernel(x_ref, g_ref, w_ref, o_ref, h_scr):
    @pl.when(pl.program_id(1) == 0)
    def _():
        h_scr[...] = _rms(x_ref[...], g_ref[...]).astype(BF16)

    o_ref[...] = _dg(h_scr[...], w_ref[...], NN).astype(o_ref.dtype)


def norm_matmul(x, g, w, tm, tn, out_dtype=F32):
    m, k = x.shape
    n = w.shape[1]
    return pl.pallas_call(
        _norm_matmul_kernel,
        grid=(m // tm, n // tn),
        in_specs=[pl.BlockSpec((tm, k), lambda i, j: (i, 0)),
                  pl.BlockSpec((1, k), lambda i, j: (0, 0)),
                  pl.BlockSpec((k, tn), lambda i, j: (0, j))],
        out_specs=pl.BlockSpec((tm, tn), lambda i, j: (i, j)),
        out_shape=jax.ShapeDtypeStruct((m, n), out_dtype),
        scratch_shapes=[pltpu.VMEM((tm, k), BF16)],
        compiler_params=_params("parallel", "arbitrary"),
        name="norm_matmul",
    )(x, g, w)


def _merge_kernel(x_ref, oa_ref, ob_ref, gma_ref, gmb_ref, wa_ref, wb_ref, wo_ref, o_ref, acc):
    n = pl.program_id(1)
    merged = (_sigmoid(gma_ref[...]) * _dg(oa_ref[...], wa_ref[...], NN)
              + _sigmoid(gmb_ref[...]) * _dg(ob_ref[...], wb_ref[...], NN))
    part = _dg(merged.astype(BF16), wo_ref[...], NN)

    @pl.when(n == 0)
    def _():
        acc[...] = part

    @pl.when(n > 0)
    def _():
        acc[...] += part

    @pl.when(n == pl.num_programs(1) - 1)
    def _():
        o_ref[...] = x_ref[...] + acc[...]


def merge_out(x, o_a, o_b, p2, w_up_a, w_up_b, w_o, tm, tn):
    m = x.shape[0]
    nt = D_MODEL // tn
    return pl.pallas_call(
        _merge_kernel,
        grid=(m // tm, nt),
        in_specs=[pl.BlockSpec((tm, D_MODEL), lambda i, n: (i, 0)),
                  pl.BlockSpec((tm, W_A), lambda i, n: (i, 0)),
                  pl.BlockSpec((tm, W_B), lambda i, n: (i, 0)),
                  pl.BlockSpec((tm, tn), lambda i, n: (i, n)),
                  pl.BlockSpec((tm, tn), lambda i, n: (i, nt + n)),
                  pl.BlockSpec((W_A, tn), lambda i, n: (0, n)),
                  pl.BlockSpec((W_B, tn), lambda i, n: (0, n)),
                  pl.BlockSpec((tn, D_MODEL), lambda i, n: (n, 0))],
        out_specs=pl.BlockSpec((tm, D_MODEL), lambda i, n: (i, 0)),
        out_shape=jax.ShapeDtypeStruct((m, D_MODEL), F32),
        scratch_shapes=[pltpu.VMEM((tm, D_MODEL), F32)],
        compiler_params=_params("parallel", "arbitrary"),
        name="merge_out",
    )(x, o_a, o_b, p2, p2, w_up_a, w_up_b, w_o)


def _mlp_kernel(x_ref, g_ref, w1_ref, w2_ref, o_ref, h_scr, acc):
    f = pl.program_id(1)

    @pl.when(f == 0)
    def _():
        h_scr[...] = _rms(x_ref[...], g_ref[...]).astype(BF16)

    a = jnp.maximum(_dg(h_scr[...], w1_ref[...], NN), 0.0)
    part = _dg((a * a).astype(BF16), w2_ref[...], NN)

    @pl.when(f == 0)
    def _():
        acc[...] = part

    @pl.when(f > 0)
    def _():
        acc[...] += part

    @pl.when(f == pl.num_programs(1) - 1)
    def _():
        o_ref[...] = x_ref[...] + acc[...]


def mlp(x, g, w1, w2, tm, tf):
    m = x.shape[0]
    return pl.pallas_call(
        _mlp_kernel,
        grid=(m // tm, D_FF // tf),
        in_specs=[pl.BlockSpec((tm, D_MODEL), lambda i, f: (i, 0)),
                  pl.BlockSpec((1, D_MODEL), lambda i, f: (0, 0)),
                  pl.BlockSpec((D_MODEL, tf), lambda i, f: (0, f)),
                  pl.BlockSpec((tf, D_MODEL), lambda i, f: (f, 0))],
        out_specs=pl.BlockSpec((tm, D_MODEL), lambda i, f: (i, 0)),
        out_shape=jax.ShapeDtypeStruct((m, D_MODEL), F32),
        scratch_shapes=[pltpu.VMEM((tm, D_MODEL), BF16), pltpu.VMEM((tm, D_MODEL), F32)],
        compiler_params=_params("parallel", "arbitrary"),
        name="mlp",
    )(x, g, w1, w2)


def _ple_final_kernel(x_ref, p_ref, gp_ref, wg_ref, wp_ref, gf_ref, o_ref):
    x = x_ref[...]
    gate = _sigmoid(_dg(_rms(x, gp_ref[...]).astype(BF16), wg_ref[...], NN))
    x = x + gate * _dg(p_ref[...].astype(BF16), wp_ref[...], NN)
    o_ref[...] = _rms(x, gf_ref[...])


def ple_final(x, ple, g_ple, w_gate, w_ple, g_final, tm):
    m = x.shape[0]
    return pl.pallas_call(
        _ple_final_kernel,
        grid=(m // tm,),
        in_specs=[pl.BlockSpec((tm, D_MODEL), lambda i: (i, 0)),
                  pl.BlockSpec((tm, D_PLE), lambda i: (i, 0)),
                  pl.BlockSpec((1, D_MODEL), lambda i: (0, 0)),
                  pl.BlockSpec((D_MODEL, D_MODEL), lambda i: (0, 0)),
                  pl.BlockSpec((D_PLE, D_MODEL), lambda i: (0, 0)),
                  pl.BlockSpec((1, D_MODEL), lambda i: (0, 0))],
        out_specs=pl.BlockSpec((tm, D_MODEL), lambda i: (i, 0)),
        out_shape=jax.ShapeDtypeStruct((m, D_MODEL), F32),
        compiler_params=_params("parallel"),
        name="ple_final",
    )(x, ple, g_ple, w_gate, w_ple, g_final)


def _gdn_kernel(q_ref, k_ref, v_ref, gate_ref, ab_ref, hq_ref, hk_ref, hv_ref,
                cq_ref, ck_ref, cv_ref, alog_ref, dt_ref, ng_ref, s0_ref,
                o_ref, sout_ref, qs, ks, vs, s_scr, *, c, hb):
    g_idx = pl.program_id(1)
    t_idx = pl.program_id(2)
    hist = SUBLANES

    @pl.when(t_idx == 0)
    def _():
        s_scr[...] = s0_ref[0]
        qs[0:hist, :] = hq_ref[0]
        ks[0:hist, :] = hk_ref[0]
        vs[0:hist, :] = hv_ref[0]

    def conv_silu(scr, x_ref, w_ref):
        scr[hist:hist + c, :] = x_ref[...]
        out = scr[hist - 3:hist - 3 + c, :] * w_ref[0:1, :]
        for j in range(1, CONV_W):
            out = out + scr[hist - 3 + j:hist - 3 + j + c, :] * w_ref[j:j + 1, :]
        scr[0:hist, :] = scr[c:c + hist, :]
        return out * _sigmoid(out)

    qc = conv_silu(qs, q_ref, cq_ref)
    kc = conv_silu(ks, k_ref, ck_ref)
    vc = conv_silu(vs, v_ref, cv_ref)

    ab = ab_ref[...]
    g_all = -jnp.exp(alog_ref[...]) * _softplus(ab + dt_ref[...])
    beta_all = _sigmoid(ab)
    gcum = _cumsum_rows(g_all, c)
    gcum_t = gcum.T
    lane = lax.broadcasted_iota(jnp.int32, (c, LANES), 1)
    sub_t = lax.broadcasted_iota(jnp.int32, (LANES, c), 0)
    i = lax.broadcasted_iota(jnp.int32, (c, c), 0)
    j = lax.broadcasted_iota(jnp.int32, (c, c), 1)
    incl = i >= j
    strict = i > j
    last_row = lax.broadcasted_iota(jnp.int32, (c, 1), 0) == c - 1
    gate = gate_ref[...]
    ng = ng_ref[...]

    hs = range(hb)
    sls = [slice(h * DK_A, (h + 1) * DK_A) for h in hs]
    heads = [g_idx * hb + h for h in hs]
    gc = [jnp.sum(jnp.where(lane == ALPHA_LANE + hd, gcum, 0.0), axis=1, keepdims=True) for hd in heads]
    gr = [jnp.sum(jnp.where(sub_t == ALPHA_LANE + hd, gcum_t, 0.0), axis=0, keepdims=True) for hd in heads]
    bc = [jnp.sum(jnp.where(lane == BETA_LANE + hd, beta_all, 0.0), axis=1, keepdims=True) for hd in heads]
    qh = [qc[:, sl] for sl in sls]
    kh = [kc[:, sl] for sl in sls]
    vh = [vc[:, sl] for sl in sls]
    qh = [x * lax.rsqrt(jnp.sum(x * x, axis=-1, keepdims=True) + EPS) * (DK_A ** -0.5) for x in qh]
    kh = [x * lax.rsqrt(jnp.sum(x * x, axis=-1, keepdims=True) + EPS) for x in kh]
    dec = [jnp.exp(jnp.where(incl, gc[h] - gr[h], -jnp.inf)) for h in hs]
    kb = [kh[h] * bc[h] for h in hs]
    kk = _dot3_many(kb, kh, NT)
    a_mat = [jnp.where(strict, kk[h] * dec[h], 0.0) for h in hs]
    t_inv = _tri_inv_many(a_mat, c)
    eg = [jnp.exp(gc[h]) for h in hs]
    uw = [_dot1(t_inv[h], jnp.concatenate([vh[h] * bc[h], kb[h] * eg[h]], axis=1)) for h in hs]
    qk = [jnp.where(incl, _dot1(qh[h], kh[h], NT) * dec[h], 0.0) for h in hs]
    g_last = [jnp.sum(jnp.where(last_row, gc[h], 0.0), axis=0, keepdims=True) for h in hs]
    kd = [kh[h] * jnp.exp(g_last[h] - gc[h]) for h in hs]
    s_old = [s_scr[h] for h in hs]
    ws_qs = [_dot1(jnp.concatenate([uw[h][:, DV_A:], qh[h] * eg[h]], axis=0), s_old[h]) for h in hs]
    v_new = [uw[h][:, :DV_A] - ws_qs[h][:c] for h in hs]
    o = [ws_qs[h][c:] + _dot1(qk[h], v_new[h]) for h in hs]
    upd = [_dot1(kd[h], v_new[h], TN) for h in hs]
    for h in hs:
        s_scr[h] = s_old[h] * jnp.exp(g_last[h]) + upd[h]
    for h in hs:
        gh = gate[:, sls[h]]
        o_ref[:, sls[h]] = (_rms(o[h], ng) * (gh * _sigmoid(gh))).astype(o_ref.dtype)

    @pl.when(t_idx == pl.num_programs(2) - 1)
    def _():
        sout_ref[0] = s_scr[...]


def gdn_branch(p1, p3, hist8, conv_w, alog_vec, dt_vec, norm_g, s0, b, t, c, hb):
    nc = t // c
    ng = H_A // hb
    gw = hb * DK_A
    row = lambda bi, g, ti: bi * nc + ti
    seg = lambda k: pl.BlockSpec((c, gw), lambda bi, g, ti: (row(bi, g, ti), k * ng + g))
    hist = lambda k: pl.BlockSpec((1, SUBLANES, gw), lambda bi, g, ti: (bi, 0, k * ng + g))
    cw = lambda k: pl.BlockSpec((CONV_W, gw), lambda bi, g, ti: (0, k * ng + g))
    vec = pl.BlockSpec((1, LANES), lambda bi, g, ti: (0, 0))
    state = pl.BlockSpec((1, hb, DK_A, DV_A), lambda bi, g, ti: (bi, g, 0, 0))
    return pl.pallas_call(
        functools.partial(_gdn_kernel, c=c, hb=hb),
        grid=(b, ng, nc),
        in_specs=[seg(0), seg(1), seg(2), seg(3),
                  pl.BlockSpec((c, LANES), lambda bi, g, ti: (row(bi, g, ti), AB_BLOCK)),
                  hist(0), hist(1), hist(2), cw(0), cw(1), cw(2), vec, vec, vec, state],
        out_specs=[pl.BlockSpec((c, gw), lambda bi, g, ti: (row(bi, g, ti), g)), state],
        out_shape=[jax.ShapeDtypeStruct((b * t, W_A), BF16),
                   jax.ShapeDtypeStruct((b, H_A, DK_A, DV_A), F32)],
        scratch_shapes=[pltpu.VMEM((c + SUBLANES, gw), F32)] * 3 + [pltpu.VMEM((hb, DK_A, DV_A), F32)],
        compiler_params=_params("parallel", "parallel", "arbitrary"),
        name="gdn_branch",
    )(p1, p1, p1, p1, p3, hist8, hist8, hist8, conv_w, conv_w, conv_w, alog_vec, dt_vec, norm_g, s0)


def _rwkv_kernel(r_ref, k_ref, v_ref, tail_ref, pr_ref, pk_ref, pv_ref, pt_ref,
                 mr_ref, mk_ref, mv_ref, mt_ref, w0_ref, a0_ref, kk_ref, ka_ref, rk_ref,
                 lg_ref, lb_ref, w2_ref, a2_ref, g2_ref, s0_ref,
                 o_ref, sout_ref, prev_r, prev_k, prev_v, prev_t, s_scr, *, c, pg):
    t_idx = pl.program_id(2)

    @pl.when(t_idx == 0)
    def _():
        prev_r[...] = pr_ref[0]
        prev_k[...] = pk_ref[0]
        prev_v[...] = pv_ref[0]
        prev_t[...] = pt_ref[0]
        z = jnp.zeros((N_B, N_B), F32)
        for p in range(pg):
            s_a = s0_ref[0, 2 * p]
            s_b = s0_ref[0, 2 * p + 1]
            s_scr[p] = jnp.concatenate([jnp.concatenate([s_a, z], axis=1),
                                        jnp.concatenate([z, s_b], axis=1)], axis=0)

    def mix(x_ref, prev, mu_ref):
        x = x_ref[...]
        row = lax.broadcasted_iota(jnp.int32, x.shape, 0)
        shifted = jnp.where(row == 0, prev[...], pltpu.roll(x, 1, axis=0))
        prev[...] = x_ref[c - 1:c, :]
        return x + (shifted - x) * mu_ref[...]

    r = mix(r_ref, prev_r, mr_ref)
    k = mix(k_ref, prev_k, mk_ref)
    v = mix(v_ref, prev_v, mv_ref)
    tail = mix(tail_ref, prev_t, mt_ref)
    xw = tail[:, 0:LANES]
    xa = tail[:, LANES:2 * LANES]
    xg = tail[:, 2 * LANES:]

    w_log = -_softplus(-(w0_ref[...] + _dot1(jnp.tanh(xw), w2_ref[...]))) - 0.5
    lw = -jnp.exp(w_log)
    a = _sigmoid(a0_ref[...] + _dot1(xa, a2_ref[...]))
    gate = _dot1(_sigmoid(xg), g2_ref[...])

    gw = pg * LANES
    fp = (lax.broadcasted_iota(jnp.int32, (c, LANES), 1) & N_B) == 0

    def seg_sum(x):
        outs = []
        for p in range(pg):
            xp = x[:, p * LANES:(p + 1) * LANES]
            s_a =jnp.sum(jnp.where(fp, xp, 0.0), axis=-1, keepdims=True)
            s_b = jnp.sum(jnp.where(fp, 0.0, xp), axis=-1, keepdims=True)
            outs.append(jnp.where(fp, s_a, s_b))
        return jnp.concatenate(outs, axis=1) if pg > 1 else outs[0]

    kkx = k * kk_ref[...]
    kk = kkx * lax.rsqrt(seg_sum(kkx * kkx) + EPS)
    kh = k * (1.0 + (a - 1.0) * ka_ref[...])
    bb = kk * a
    cw = _cumsum_rows(lw, c)
    e_inc = jnp.exp(cw)
    e_neg = jnp.exp(-cw)
    row_g = lax.broadcasted_iota(jnp.int32, (c, gw), 0)
    cw_last = jnp.sum(jnp.where(row_g == c - 1, cw, 0.0), axis=0, keepdims=True)
    e_last = jnp.exp(cw_last - cw)
    rt = r * e_inc
    kt = kh * e_neg
    bt = bb * e_neg
    at = kk * jnp.exp(cw - lw)
    k_hat = kh * e_last
    b_hat = bb * e_last
    e_end = jnp.exp(cw_last)

    i = lax.broadcasted_iota(jnp.int32, (c, c), 0)
    j = lax.broadcasted_iota(jnp.int32, (c, c), 1)
    incl = i >= j
    strict = i > j
    bi = lax.broadcasted_iota(jnp.int32, (LANES, LANES), 0)
    bj = lax.broadcasted_iota(jnp.int32, (LANES, LANES), 1)
    same_head = ((bi ^ bj) & N_B) == 0

    ps = range(pg)
    hds = [(p, e) for p in ps for e in range(2)]
    sls = [slice(p * LANES, (p + 1) * LANES) for p in ps]
    s_old = [s_scr[p] for p in ps]
    at_p = [at[:, sl] for sl in sls]
    rt_p = [rt[:, sl] for sl in sls]
    kt_p = [kt[:, sl] for sl in sls]
    bt_p = [bt[:, sl] for sl in sls]
    v_p = [v[:, sl] for sl in sls]
    a_s = [_dot1(at_p[p], s_old[p], NT) for p in ps]
    r_s = [_dot1(rt_p[p], s_old[p], NT) for p in ps]
    not_fp = jnp.logical_not(fp)
    lhs = [jnp.concatenate([jnp.where(fp if e == 0 else not_fp, at_p[p], 0.0),
                            jnp.where(fp if e == 0 else not_fp, rt_p[p], 0.0)], axis=0) for p, e in hds]
    xb_ = _dot3_many(lhs, [bt_p[p] for p, e in hds], NT)
    xk_ = _dot3_many(lhs, [kt_p[p] for p, e in hds], NT)
    l_ab = [jnp.where(strict, x[:c], 0.0) for x in xb_]
    l_ak = [jnp.where(strict, x[:c], 0.0) for x in xk_]
    r_b = [jnp.where(incl, x[c:], 0.0) for x in xb_]
    r_k = [jnp.where(incl, x[c:], 0.0) for x in xk_]
    t_inv = _tri_inv_many(l_ab, c)
    rhs = [a_s[p] + _dot1(l_ak[n], v_p[p]) for n, (p, e) in enumerate(hds)]
    pz = [_dot1(t_inv[n], rhs[n]) for n in range(len(hds))]
    p_all = [jnp.where(fp, pz[2 * p], pz[2 * p + 1]) for p in ps]
    yz = [r_s[p] + _dot1(r_k[n], v_p[p]) - _dot1(r_b[n], p_all[p]) for n, (p, e) in enumerate(hds)]
    ys = [jnp.where(fp, yz[2 * p], yz[2 * p + 1]) for p in ps]
    upd = [_dot1(jnp.concatenate([v_p[p], p_all[p]], axis=0),
                 jnp.concatenate([k_hat[:, sls[p]], -b_hat[:, sls[p]]], axis=0), TN) for p in ps]
    for p in ps:
        s_scr[p] = s_old[p] * e_end[:, sls[p]] + jnp.where(same_head, upd[p], 0.0)

    y = jnp.concatenate(ys, axis=1) if pg > 1 else ys[0]
    inv_n = 1.0 / N_B
    yc = y - seg_sum(y) * inv_n
    yn = yc * lax.rsqrt(seg_sum(yc * yc) * inv_n + LNX_EPS)
    yn = yn * lg_ref[...] + lb_ref[...]
    bonus = seg_sum(r * kh * rk_ref[...]) * v
    o_ref[...] = ((yn + bonus) * gate).astype(o_ref.dtype)

    @pl.when(t_idx == pl.num_programs(2) - 1)
    def _():
        for p in range(pg):
            s = s_scr[p]
            sout_ref[0, 2 * p] = s[:N_B, :N_B]
            sout_ref[0, 2 * p + 1] = s[N_B:, N_B:]


def rwkv_branch(p3, prev, mu, w0, a0, k_k, k_a, r_k, lnx_g, lnx_b, w2p, a2p, g2, s0, b, t, c, pg):
    nc = t // c
    gw = pg * LANES
    ng = W_B // gw
    tw = XB_PAD - 3 * W_B
    tail_blk = 3 * W_B // tw
    row = lambda bi, g, ti: bi * nc + ti
    seg = lambda k: pl.BlockSpec((c, gw), lambda bi, g, ti: (row(bi, g, ti), k * ng + g))
    pseg = lambda k: pl.BlockSpec((1, 1, gw), lambda bi, g, ti: (bi, 0, k * ng + g))
    mseg = lambda k: pl.BlockSpec((1, gw), lambda bi, g, ti: (0, k * ng + g))
    chan = pl.BlockSpec((1, gw), lambda bi, g, ti: (0, g))
    state = pl.BlockSpec((1, 2 * pg, N_B, N_B), lambda bi, g, ti: (bi, g, 0, 0))
    return pl.pallas_call(
        functools.partial(_rwkv_kernel, c=c, pg=pg),
        grid=(b, ng, nc),
        in_specs=[seg(0), seg(1), seg(2),
                  pl.BlockSpec((c, tw), lambda bi, g, ti: (row(bi, g, ti), tail_blk)),
                  pseg(0), pseg(1), pseg(2),
                  pl.BlockSpec((1, 1, tw), lambda bi, g, ti: (bi, 0, tail_blk)),
                  mseg(0), mseg(1), mseg(2),
                  pl.BlockSpec((1, tw), lambda bi, g, ti: (0, tail_blk)),
                  chan, chan, chan, chan, chan, chan, chan,
                  pl.BlockSpec((LANES, gw), lambda bi, g, ti: (0, g)),
                  pl.BlockSpec((LANES, gw), lambda bi, g, ti: (0, g)),
                  pl.BlockSpec((LORA_G, gw), lambda bi, g, ti: (0, g)),
                  state],
        out_specs=[pl.BlockSpec((c, gw), lambda bi, g, ti: (row(bi, g, ti), g)), state],
        out_shape=[jax.ShapeDtypeStruct((b * t, W_B), BF16),
                   jax.ShapeDtypeStruct((b, H_B, N_B, N_B), F32)],
        scratch_shapes=[pltpu.VMEM((1, gw), F32)] * 3 + [pltpu.VMEM((1, tw), F32),
                                                        pltpu.VMEM((pg, LANES, LANES), F32)],
        compiler_params=_params("parallel", "parallel", "arbitrary"),
        name="rwkv_branch",
    )(p3, p3, p3, p3, prev, prev, prev, prev, mu, mu, mu, mu,
      w0, a0, k_k, k_a, r_k, lnx_g, lnx_b, w2p, a2p, g2, s0)


def _pad_xb(a, ab=None):
    lead = a.shape[:-1]
    z = lambda n: jnp.zeros(lead + (n,), a.dtype)
    o_w, o_a, o_g = 3 * W_B, 3 * W_B + LORA_W, 3 * W_B + LORA_W + LORA_A
    mid = z(2 * H_A) if ab is None else ab
    return jnp.concatenate([a[..., :o_a], mid, z(LANES - LORA_W - 2 * H_A),
                            a[..., o_a:o_g], z(LANES - LORA_A), a[..., o_g:]], axis=-1)


def _unpad_xb(a):
    o_a = 3 * W_B + LORA_W
    return jnp.concatenate([a[..., :o_a], a[..., 3 * W_B + LANES:3 * W_B + LANES + LORA_A],
                            a[..., 3 * W_B + 2 * LANES:]], axis=-1)


def _pad_rows(w, n):
    return jnp.concatenate([w, jnp.zeros((n - w.shape[0],) + w.shape[1:], w.dtype)], axis=0)


def _lane_vec(v, start):
    return jnp.zeros((1, LANES), F32).at[0, start:start + v.shape[0]].set(v)


def _prepare(lw):
    (g_mix, w_in, conv_w, a_log, dt_bias, gdn_norm_g, mu_shift, w0, w2, a0, a2, g2,
     k_k, k_a, r_k, lnx_g, lnx_b, w_up_a, w_up_b, w_o, g_mlp, w_ff1, w_ff2,
     g_ple, w_ple_gate, w_ple) = lw
    o_alpha = CONV_CH
    o_gate_a = CONV_CH + 2 * H_A
    o_xb = o_gate_a + W_A
    o_gm = o_xb + SHIFT_W
    row = lambda v: v.reshape(1, -1)
    return dict(
        g_mix=row(g_mix),
        w1=jnp.concatenate([w_in[:, :CONV_CH], w_in[:, o_gate_a:o_xb]], axis=1).astype(BF16),
        w2g=w_in[:, o_gm:].astype(BF16),
        w3=_pad_xb(w_in[:, o_xb:o_gm], w_in[:, o_alpha:o_gate_a]).astype(BF16),
        conv_w=conv_w,
        alog_vec=_lane_vec(a_log, ALPHA_LANE), dt_vec=_lane_vec(dt_bias, ALPHA_LANE),
        gdn_norm_g=row(gdn_norm_g),
        mu=row(_pad_xb(mu_shift)),
        w0=row(w0), a0=row(a0), k_k=row(k_k), k_a=row(k_a), r_k=row(r_k.reshape(-1)),
        lnx_g=row(lnx_g), lnx_b=row(lnx_b),
        w2p=_pad_rows(w2, LANES).astype(BF16), a2p=_pad_rows(a2, LANES).astype(BF16), g2=g2.astype(BF16),
        w_up_a=w_up_a.astype(BF16), w_up_b=w_up_b.astype(BF16), w_o=w_o.astype(BF16),
        g_mlp=row(g_mlp), w_ff1=w_ff1.astype(BF16), w_ff2=w_ff2.astype(BF16),
        g_ple=row(g_ple), w_ple_gate=w_ple_gate.astype(BF16), w_ple=w_ple.astype(BF16),
    )


def _tiles(m):
    tm_proj = min(m, 1024)
    tm = min(m, 512)
    return tm_proj, tm


def _run_layer(x, ple, s_gdn, buf_gdn, s_rwkv, shift_rwkv, pw, g_final):
    b, t, _ = x.shape
    m = b * t
    c = min(CHUNK, t)
    assert t % c == 0 and c % SUBLANES == 0 and t >= CONV_W - 1
    tm_proj, tm = _tiles(m)
    x2 = x.reshape(m, D_MODEL)

    p1 = norm_matmul(x2, pw["g_mix"], pw["w1"], tm_proj, 512)
    p2 = norm_matmul(x2, pw["g_mix"], pw["w2g"], tm_proj, 512)
    p3 = norm_matmul(x2, pw["g_mix"], pw["w3"], tm_proj, 512)

    hist8 = jnp.concatenate([jnp.zeros((b, SUBLANES - (CONV_W - 1), CONV_CH), F32), buf_gdn], axis=1)
    o_a, s_gdn_new = gdn_branch(p1, p3, hist8, pw["conv_w"], pw["alog_vec"], pw["dt_vec"],
                                pw["gdn_norm_g"], s_gdn, b, t, c, hb=GDN_HB)
    prev = _pad_xb(shift_rwkv).reshape(b, 1, XB_PAD)
    o_b, s_rwkv_new = rwkv_branch(p3, prev, pw["mu"], pw["w0"], pw["a0"], pw["k_k"], pw["k_a"], pw["r_k"],
                                  pw["lnx_g"], pw["lnx_b"], pw["w2p"], pw["a2p"], pw["g2"], s_rwkv,
                                  b, t, c, pg=RWKV_PG)

    x1 = merge_out(x2, o_a, o_b, p2, pw["w_up_a"], pw["w_up_b"], pw["w_o"], tm, 512)
    x3 = mlp(x1, pw["g_mlp"], pw["w_ff1"], pw["w_ff2"], tm, 512)
    y = ple_final(x3, ple.reshape(m, D_PLE), pw["g_ple"], pw["w_ple_gate"], pw["w_ple"], g_final, min(m, 256))

    new_buf = p1.reshape(b, t, 4 * W_A)[:, t - (CONV_W - 1):, :CONV_CH]
    new_shift = _unpad_xb(p3.reshape(b, t, XB_PAD)[:, t - 1])
    return y.reshape(b, t, D_MODEL), s_gdn_new, new_buf, s_rwkv_new, new_shift


def kernel(x_prompt, x_sample, p_prompt, p_sample, state_gdn, cache_gdn_conv, state_rwkv, cache_rwkv_shift, g_mix, w_in, conv_w, a_log, dt_bias, gdn_norm_g, mu_shift, w0, w2, a0, a2, g2, k_k, k_a, r_k, lnx_g, lnx_b, w_up_a, w_up_b, w_o, g_mlp, w_ff1, w_ff2, g_ple, w_ple_gate, w_ple, g_final):
    params = (g_mix, w_in, conv_w, a_log, dt_bias, gdn_norm_g, mu_shift, w0, w2, a0, a2, g2,
              k_k, k_a, r_k, lnx_g, lnx_b, w_up_a, w_up_b, w_o, g_mlp, w_ff1, w_ff2,
              g_ple, w_ple_gate, w_ple)
    depth = w_in.shape[0]
    assert depth == 1
    pw = _prepare(tuple(w[0] for w in params))
    gf = g_final.reshape(1, -1)
    bp = x_prompt.shape[0]
    z_sg = jnp.zeros((bp, H_A, DK_A, DV_A), F32)
    z_buf = jnp.zeros((bp, CONV_W - 1, CONV_CH), F32)
    z_sr = jnp.zeros((bp, H_B, N_B, N_B), F32)
    z_sh = jnp.zeros((bp, SHIFT_W), F32)
    y_p, sg_p, buf_p, sr_p, sh_p = _run_layer(x_prompt, p_prompt[0], z_sg, z_buf, z_sr, z_sh, pw, gf)
    y_s, sg_s, buf_s, sr_s, sh_s = _run_layer(x_sample, p_sample[0], state_gdn[0], cache_gdn_conv[0],
                                              state_rwkv[0], cache_rwkv_shift[0], pw, gf)
    st = lambda a: a[None]
    return (y_p, y_s, st(sg_p), st(buf_p), st(sr_p), st(sh_p), st(sg_s), st(buf_s), st(sr_s), st(sh_s))
```

```python
import functools
import math

import jax
import jax.numpy as jnp
from jax import lax
from jax.experimental import pallas as pl
from jax.experimental.pallas import tpu as pltpu

F32 = jnp.float32
BF16 = jnp.bfloat16

D_MODEL = 2048
H_A, DK_A, DV_A = 8, 128, 128
W_A = H_A * DV_A
CONV_W = 4
CONV_CH = 3 * W_A
H_B, N_B = 16, 64
W_B = H_B * N_B
LORA_W, LORA_A, LORA_G = 96, 96, 256
SHIFT_W = 3 * W_B + LORA_W + LORA_A + LORA_G
D_FF = 4 * D_MODEL
D_PLE = 256
EPS = 1e-6
LNX_EPS = 64e-5
CHUNK = 64
GDN_HB = 8
RWKV_PG = 8

LANES = 128
SUBLANES = 8
VMEM_LIMIT = 52 * 2 ** 20

XB_PAD = 3 * W_B + 2 * LANES + LORA_G
AB_BLOCK = 3 * W_B // LANES
ALPHA_LANE = LORA_W
BETA_LANE = LORA_W + H_A

P1_OFF, P2_OFF, P3_OFF = 0, 4 * W_A, 4 * W_A + 2 * D_MODEL
N_PROJ = P3_OFF + XB_PAD

NN = ((1,), (0,))
NT = ((1,), (1,))
TN = ((0,), (0,))


def _dg(a, b, dims):
    return lax.dot_general(a, b, (dims, ((), ())), preferred_element_type=F32)


def _dot1(a, b, dims=NN):
    return _dg(a.astype(BF16), b.astype(BF16), dims)


def _cumsum_rows(x, c):
    i = lax.broadcasted_iota(jnp.int32, (c, c), 0)
    j = lax.broadcasted_iota(jnp.int32, (c, c), 1)
    tri = jnp.where(i >= j, 1.0, 0.0).astype(BF16)
    h1 = x.astype(BF16)
    r1 = x - h1.astype(F32)
    h2 = r1.astype(BF16)
    h3 = (r1 - h2.astype(F32)).astype(BF16)
    return _dg(tri, h1, NN) + (_dg(tri, h2, NN) + _dg(tri, h3, NN))


def _tri_inv_many(a_list, c):
    i = lax.broadcasted_iota(jnp.int32, (c, c), 0)
    j = lax.broadcasted_iota(jnp.int32, (c, c), 1)
    ns = [-jnp.where((i ^ j) == 1, a, 0.0) for a in a_list]
    s, ls = 2, 1
    while s < c:
        mask = ((i ^ j) >> ls) == 1
        l_s = [jnp.where(mask, a, 0.0) for a in a_list]
        nl = [_dot1(n, l) for n, l in zip(ns, l_s)]
        ys = [l + x for l, x in zip(l_s, nl)]
        yn = [_dot1(y, n) for y, n in zip(ys, ns)]
        ns = [n - (y + z) for n, y, z in zip(ns, ys, yn)]
        s, ls = 2 * s, ls + 1
    eye = jnp.where(i == j, 1.0, 0.0)
    return [eye + n for n in ns]


def _softplus(x):
    return jnp.maximum(x, 0.0) + jnp.log1p(jnp.exp(-jnp.abs(x)))


def _sigmoid(x):
    return jax.nn.sigmoid(x)


def _rms(x, g):
    return x * lax.rsqrt(jnp.mean(x * x, axis=-1, keepdims=True) + EPS) * g


def _params(*sem):
    return pltpu.CompilerParams(dimension_semantics=sem, vmem_limit_bytes=VMEM_LIMIT)


def _norm_matmul_kernel(x_ref, g_ref, w_ref, o_ref, h_scr):
    @pl.when(pl.program_id(1) == 0)
    def _():
        h_scr[...] = _rms(x_ref[...], g_ref[...]).astype(BF16)

    o_ref[...] = _dg(h_scr[...], w_ref[...], NN).astype(o_ref.dtype)


def norm_matmul(x, g, w, tm, tn, out_dtype=F32):
    m, k = x.shape
    n = w.shape[1]
    return pl.pallas_call(
        _norm_matmul_kernel,
        grid=(m // tm, n // tn),
        in_specs=[pl.BlockSpec((tm, k), lambda i, j: (i, 0)),
                  pl.BlockSpec((1, k), lambda i, j: (0, 0)),
                  pl.BlockSpec((k, tn), lambda i, j: (0, j))],
        out_specs=pl.BlockSpec((tm, tn), lambda i, j: (i, j)),
        out_shape=jax.ShapeDtypeStruct((m, n), out_dtype),
        scratch_shapes=[pltpu.VMEM((tm, k), BF16)],
        compiler_params=_params("parallel", "arbitrary"),
        name="norm_matmul",
    )(x, g, w)


def _merge_kernel(x_ref, oa_ref, ob_ref, gma_ref, gmb_ref, wa_ref, wb_ref, wo_ref, g_ref, o_ref, h_ref, *, tn):
    oa = oa_ref[...]
    ob = ob_ref[...]
    acc = x_ref[...]
    for n in range(D_MODEL // tn):
        cs = slice(n * tn, (n + 1) * tn)
        merged = (_sigmoid(gma_ref[:, cs]) * _dg(oa, wa_ref[:, cs], NN)
                  + _sigmoid(gmb_ref[:, cs]) * _dg(ob, wb_ref[:, cs], NN))
        acc = acc + _dg(merged.astype(BF16), wo_ref[cs, :], NN)
    o_ref[...] = acc
    h_ref[...] = _rms(acc, g_ref[...]).astype(BF16)


def merge_out(x, o_a, o_b, p, gate_blk, w_up_a, w_up_b, w_o, g_mlp, tm, tn):
    m = x.shape[0]
    resident = lambda shape: pl.BlockSpec(shape, lambda i: (0, 0), pipeline_mode=pl.Buffered(1))
    return pl.pallas_call(
        functools.partial(_merge_kernel, tn=tn),
        grid=(m // tm,),
        in_specs=[pl.BlockSpec((tm, D_MODEL), lambda i: (i, 0)),
                  pl.BlockSpec((tm, W_A), lambda i: (i, 0)),
                  pl.BlockSpec((tm, W_B), lambda i: (i, 0)),
                  pl.BlockSpec((tm, D_MODEL), lambda i: (i, gate_blk)),
                  pl.BlockSpec((tm, D_MODEL), lambda i: (i, gate_blk + 1)),
                  resident((W_A, D_MODEL)), resident((W_B, D_MODEL)), resident((D_MODEL, D_MODEL)),
                  pl.BlockSpec((1, D_MODEL), lambda i: (0, 0))],
        out_specs=[pl.BlockSpec((tm, D_MODEL), lambda i: (i, 0)),
                   pl.BlockSpec((tm, D_MODEL), lambda i: (i, 0))],
        out_shape=[jax.ShapeDtypeStruct((m, D_MODEL), F32), jax.ShapeDtypeStruct((m, D_MODEL), BF16)],
        compiler_params=_params("parallel"),
        name="merge_out",
    )(x, o_a, o_b, p, p, w_up_a, w_up_b, w_o, g_mlp)


def _mlp_kernel(h_ref, w1_ref, w2_ref, o_ref, *, tc):
    f = pl.program_id(1)

    @pl.when(f == 0)
    def _():
        o_ref[...] = jnp.zeros_like(o_ref)

    h = h_ref[...]
    for fc in range(w1_ref.shape[1] // tc):
        fs = slice(fc * tc, (fc + 1) * tc)
        a = jnp.maximum(_dg(h, w1_ref[:, fs], NN), 0.0)
        a = (a * a).astype(BF16)
        for nc in range(D_MODEL // tc):
            ns = slice(nc * tc, (nc + 1) * tc)
            o_ref[:, ns] += _dg(a, w2_ref[fs, ns], NN)


def mlp(h, w1, w2, tm, tf, tc):
    m = h.shape[0]
    return pl.pallas_call(
        functools.partial(_mlp_kernel, tc=tc),
        grid=(m // tm, D_FF // tf),
        in_specs=[pl.BlockSpec((tm, D_MODEL), lambda i, f: (i, 0)),
                  pl.BlockSpec((D_MODEL, tf), lambda i, f: (0, f)),
                  pl.BlockSpec((tf, D_MODEL), lambda i, f: (f, 0))],
        out_specs=pl.BlockSpec((tm, D_MODEL), lambda i, f: (i, 0)),
        out_shape=jax.ShapeDtypeStruct((m, D_MODEL), F32),
        compiler_params=_params("parallel", "arbitrary"),
        name="mlp",
    )(h, w1, w2)


def _ple_final_kernel(x_ref, d_ref, p_ref, gp_ref, wg_ref, wp_ref, gf_ref, o_ref):
    x = x_ref[...] + d_ref[...]
    gate = _sigmoid(_dg(_rms(x, gp_ref[...]).astype(BF16), wg_ref[...], NN))
    x = x + gate * _dg(p_ref[...].astype(BF16), wp_ref[...], NN)
    o_ref[...] = _rms(x, gf_ref[...])


def ple_final(x, delta, ple, g_ple, w_gate, w_ple, g_final, tm):
    m = x.shape[0]
    resident = lambda shape: pl.BlockSpec(shape, lambda i: (0, 0), pipeline_mode=pl.Buffered(1))
    return pl.pallas_call(
        _ple_final_kernel,
        grid=(m // tm,),
        in_specs=[pl.BlockSpec((tm, D_MODEL), lambda i: (i, 0)),
                  pl.BlockSpec((tm, D_MODEL), lambda i: (i, 0)),
                  pl.BlockSpec((tm, D_PLE), lambda i: (i, 0)),
                  pl.BlockSpec((1, D_MODEL), lambda i: (0, 0)),
                  resident((D_MODEL, D_MODEL)), resident((D_PLE, D_MODEL)),
                  pl.BlockSpec((1, D_MODEL), lambda i: (0, 0))],
        out_specs=pl.BlockSpec((tm, D_MODEL), lambda i: (i, 0)),
        out_shape=jax.ShapeDtypeStruct((m, D_MODEL), F32),
        compiler_params=_params("parallel"),
        name="ple_final",
    )(x, delta, ple, g_ple, w_gate, w_ple, g_final)


def _gdn_kernel(q_ref, k_ref, v_ref, gate_ref, ab_ref, hq_ref, hk_ref, hv_ref,
                cq_ref, ck_ref, cv_ref, alog_ref, dt_ref, ng_ref, s0_ref,
                o_ref, sout_ref, qs, ks, vs, s_scr, *, c, hb):
    g_idx = pl.program_id(1)
    t_idx = pl.program_id(2)
    hist = SUBLANES

    @pl.when(t_idx == 0)
    def _():
        s_scr[...] = s0_ref[0]
        qs[0:hist, :] = hq_ref[0]
        ks[0:hist, :] = hk_ref[0]
        vs[0:hist, :] = hv_ref[0]

    def conv_silu(scr, x_ref, w_ref):
        scr[hist:hist + c, :] = x_ref[...]
        out = scr[hist - 3:hist - 3 + c, :] * w_ref[0:1, :]
        for j in range(1, CONV_W):
            out = out + scr[hist - 3 + j:hist - 3 + j + c, :] * w_ref[j:j + 1, :]
        scr[0:hist, :] = scr[c:c + hist, :]
        return out * _sigmoid(out)

    qc = conv_silu(qs, q_ref, cq_ref)
    kc = conv_silu(ks, k_ref, ck_ref)
    vc = conv_silu(vs, v_ref, cv_ref)

    ab = ab_ref[...]
    g_all = -jnp.exp(alog_ref[...]) * _softplus(ab + dt_ref[...])
    beta_all = _sigmoid(ab)
    gcum = _cumsum_rows(g_all, c)
    gcum_t = gcum.T
    lane = lax.broadcasted_iota(jnp.int32, (c, LANES), 1)
    sub_t = lax.broadcasted_iota(jnp.int32, (LANES, c), 0)
    i = lax.broadcasted_iota(jnp.int32, (c, c), 0)
    j = lax.broadcasted_iota(jnp.int32, (c, c), 1)
    incl = i >= j
    strict = i > j
    last_row = lax.broadcasted_iota(jnp.int32, (c, 1), 0) == c - 1
    gate = gate_ref[...]
    ng = ng_ref[...]

    hs = range(hb)
    sls = [slice(h * DK_A, (h + 1) * DK_A) for h in hs]
    heads = [g_idx * hb + h for h in hs]
    gc = [jnp.sum(jnp.where(lane == ALPHA_LANE + hd, gcum, 0.0), axis=1, keepdims=True) for hd in heads]
    gr = [jnp.sum(jnp.where(sub_t == ALPHA_LANE + hd, gcum_t, 0.0), axis=0, keepdims=True) for hd in heads]
    bc = [jnp.sum(jnp.where(lane == BETA_LANE + hd, beta_all, 0.0), axis=1, keepdims=True) for hd in heads]
    qh = [qc[:, sl] for sl in sls]
    kh = [kc[:, sl] for sl in sls]
    vh = [vc[:, sl] for sl in sls]
    qh = [x * lax.rsqrt(jnp.sum(x * x, axis=-1, keepdims=True) + EPS) * (DK_A ** -0.5) for x in qh]
    kh = [x * lax.rsqrt(jnp.sum(x * x, axis=-1, keepdims=True) + EPS) for x in kh]
    dec = [jnp.exp(jnp.where(incl, gc[h] - gr[h], -jnp.inf)) for h in hs]
    kb = [kh[h] * bc[h] for h in hs]
    kk = [_dot1(kb[h], kh[h], NT) for h in hs]
    a_mat = [jnp.where(strict, kk[h] * dec[h], 0.0) for h in hs]
    t_inv = _tri_inv_many(a_mat, c)
    eg = [jnp.exp(gc[h]) for h in hs]
    uw = [_dot1(t_inv[h], jnp.concatenate([vh[h] * bc[h], kb[h] * eg[h]], axis=1)) for h in hs]
    qk = [jnp.where(incl, _dot1(qh[h], kh[h], NT) * dec[h], 0.0) for h in hs]
    g_last = [jnp.sum(jnp.where(last_row, gc[h], 0.0), axis=0, keepdims=True) for h in hs]
    kd = [kh[h] * jnp.exp(g_last[h] - gc[h]) for h in hs]
    s_old = [s_scr[h] for h in hs]
    ws_qs = [_dot1(jnp.concatenate([uw[h][:, DV_A:], qh[h] * eg[h]], axis=0), s_old[h]) for h in hs]
    v_new = [uw[h][:, :DV_A] - ws_qs[h][:c] for h in hs]
    o = [ws_qs[h][c:] + _dot1(qk[h], v_new[h]) for h in hs]
    upd = [_dot1(kd[h], v_new[h], TN) for h in hs]
    for h in hs:
        s_scr[h] = s_old[h] * jnp.exp(g_last[h]) + upd[h]
    for h in hs:
        gh = gate[:, sls[h]]
        o_ref[:, sls[h]] = (_rms(o[h], ng) * (gh * _sigmoid(gh))).astype(o_ref.dtype)

    @pl.when(t_idx == pl.num_programs(2) - 1)
    def _():
        sout_ref[0] = s_scr[...]


def gdn_branch(p, hist8, conv_w, alog_vec, dt_vec, norm_g, s0, b, t, c, hb):
    nc = t // c
    ng = H_A // hb
    gw = hb * DK_A
    row = lambda bi, g, ti: bi * nc + ti
    seg = lambda k: pl.BlockSpec((c, gw), lambda bi, g, ti: (row(bi, g, ti), P1_OFF // gw + k * ng + g))
    hist = lambda k: pl.BlockSpec((1, SUBLANES, gw), lambda bi, g, ti: (bi, 0, k * ng + g))
    cw = lambda k: pl.BlockSpec((CONV_W, gw), lambda bi, g, ti: (0, k * ng + g))
    vec = pl.BlockSpec((1, LANES), lambda bi, g, ti: (0, 0))
    state = pl.BlockSpec((1, hb, DK_A, DV_A), lambda bi, g, ti: (bi, g, 0, 0))
    return pl.pallas_call(
        functools.partial(_gdn_kernel, c=c, hb=hb),
        grid=(b, ng, nc),
        in_specs=[seg(0), seg(1), seg(2), seg(3),
                  pl.BlockSpec((c, LANES), lambda bi, g, ti: (row(bi, g, ti), P3_OFF // LANES + AB_BLOCK)),
                  hist(0), hist(1), hist(2), cw(0), cw(1), cw(2), vec, vec, vec, state],
        out_specs=[pl.BlockSpec((c, gw), lambda bi, g, ti: (row(bi, g, ti), g)), state],
        out_shape=[jax.ShapeDtypeStruct((b * t, W_A), BF16),
                   jax.ShapeDtypeStruct((b, H_A, DK_A, DV_A), F32)],
        scratch_shapes=[pltpu.VMEM((c + SUBLANES, gw), F32)] * 3 + [pltpu.VMEM((hb, DK_A, DV_A), F32)],
        compiler_params=_params("parallel", "parallel", "arbitrary"),
        name="gdn_branch",
    )(p, p, p, p, p, hist8, hist8, hist8, conv_w, conv_w, conv_w, alog_vec, dt_vec, norm_g, s0)


def _rwkv_kernel(r_ref, k_ref, v_ref, tail_ref, pr_ref, pk_ref, pv_ref, pt_ref,
                 mr_ref, mk_ref, mv_ref, mt_ref, w0_ref, a0_ref, kk_ref, ka_ref, rk_ref,
                 lg_ref, lb_ref, w2_ref, a2_ref, g2_ref, s0_ref,
                 o_ref, sout_ref, prev_r, prev_k, prev_v, prev_t, s_scr, *, c, pg):
    t_idx = pl.program_id(2)

    @pl.when(t_idx == 0)
    def _():
        prev_r[...] = pr_ref[0]
        prev_k[...] = pk_ref[0]
        prev_v[...] = pv_ref[0]
        prev_t[...] = pt_ref[0]
        z = jnp.zeros((N_B, N_B), F32)
        for p in range(pg):
            s_a = s0_ref[0, 2 * p]
            s_b = s0_ref[0, 2 * p + 1]
            s_scr[p] = jnp.concatenate([jnp.concatenate([s_a, z], axis=1),
                                        jnp.concatenate([z, s_b], axis=1)], axis=0)

    def mix(x_ref, prev, mu_ref):
        x = x_ref[...]
        row = lax.broadcasted_iota(jnp.int32, x.shape, 0)
        shifted = jnp.where(row == 0, prev[...], pltpu.roll(x, 1, axis=0))
        prev[...] = x_ref[c - 1:c, :]
        return x + (shifted - x) * mu_ref[...]

    r = mix(r_ref, prev_r, mr_ref)
    k = mix(k_ref, prev_k, mk_ref)
    v = mix(v_ref, prev_v, mv_ref)
    tail = mix(tail_ref, prev_t, mt_ref)
    xw = tail[:, 0:LANES]
    xa = tail[:, LANES:2 * LANES]
    xg = tail[:, 2 * LANES:]

    w_log = -_softplus(-(w0_ref[...] + _dot1(jnp.tanh(xw), w2_ref[...]))) - 0.5
    lw = -jnp.exp(w_log)
    a = _sigmoid(a0_ref[...] + _dot1(xa, a2_ref[...]))
    gate = _dot1(_sigmoid(xg), g2_ref[...])

    gw = pg * LANES
    fp = (lax.broadcasted_iota(jnp.int32, (c, LANES), 1) & N_B) == 0

    def seg_sum(x):
        outs = []
        for p in range(pg):
            xp = x[:, p * LANES:(p + 1) * LANES]
            s_a =jnp.sum(jnp.where(fp, xp, 0.0), axis=-1, keepdims=True)
            s_b = jnp.sum(jnp.where(fp, 0.0, xp), axis=-1, keepdims=True)
            outs.append(jnp.where(fp, s_a, s_b))
        return jnp.concatenate(outs, axis=1) if pg > 1 else outs[0]

    kkx = k * kk_ref[...]
    kk = kkx * lax.rsqrt(seg_sum(kkx * kkx) + EPS)
    kh = k * (1.0 + (a - 1.0) * ka_ref[...])
    bb = kk * a
    cw = _cumsum_rows(lw, c)
    e_inc = jnp.exp(cw)
    e_neg = jnp.exp(-cw)
    row_g = lax.broadcasted_iota(jnp.int32, (c, gw), 0)
    cw_last = jnp.sum(jnp.where(row_g == c - 1, cw, 0.0), axis=0, keepdims=True)
    e_last = jnp.exp(cw_last - cw)
    rt = r * e_inc
    kt = kh * e_neg
    bt = bb * e_neg
    at = kk * jnp.exp(cw - lw)
    k_hat = kh * e_last
    b_hat = bb * e_last
    e_end = jnp.exp(cw_last)

    i = lax.broadcasted_iota(jnp.int32, (c, c), 0)
    j = lax.broadcasted_iota(jnp.int32, (c, c), 1)
    incl = i >= j
    strict = i > j
    bi = lax.broadcasted_iota(jnp.int32, (LANES, LANES), 0)
    bj = lax.broadcasted_iota(jnp.int32, (LANES, LANES), 1)
    same_head = ((bi ^ bj) & N_B) == 0

    ps = range(pg)
    hds = [(p, e) for p in ps for e in range(2)]
    sls = [slice(p * LANES, (p + 1) * LANES) for p in ps]
    s_old = [s_scr[p] for p in ps]
    at_p = [at[:, sl] for sl in sls]
    rt_p = [rt[:, sl] for sl in sls]
    kt_p = [kt[:, sl] for sl in sls]
    bt_p = [bt[:, sl] for sl in sls]
    v_p = [v[:, sl] for sl in sls]
    a_s = [_dot1(at_p[p], s_old[p], NT) for p in ps]
    r_s = [_dot1(rt_p[p], s_old[p], NT) for p in ps]
    not_fp = jnp.logical_not(fp)
    lhs = [jnp.concatenate([jnp.where(fp if e == 0 else not_fp, at_p[p], 0.0),
                            jnp.where(fp if e == 0 else not_fp, rt_p[p], 0.0)], axis=0) for p, e in hds]
    xb_ = [_dot1(lhs[n], bt_p[p], NT) for n, (p, e) in enumerate(hds)]
    xk_ = [_dot1(lhs[n], kt_p[p], NT) for n, (p, e) in enumerate(hds)]
    l_ab = [jnp.where(strict, x[:c], 0.0) for x in xb_]
    l_ak = [jnp.where(strict, x[:c], 0.0) for x in xk_]
    r_b = [jnp.where(incl, x[c:], 0.0) for x in xb_]
    r_k = [jnp.where(incl, x[c:], 0.0) for x in xk_]
    t_inv = _tri_inv_many(l_ab, c)
    rhs = [a_s[p] + _dot1(l_ak[n], v_p[p]) for n, (p, e) in enumerate(hds)]
    pz = [_dot1(t_inv[n], rhs[n]) for n in range(len(hds))]
    p_all = [jnp.where(fp, pz[2 * p], pz[2 * p + 1]) for p in ps]
    yz = [r_s[p] + _dot1(r_k[n], v_p[p]) - _dot1(r_b[n], p_all[p]) for n, (p, e) in enumerate(hds)]
    ys = [jnp.where(fp, yz[2 * p], yz[2 * p + 1]) for p in ps]
    upd = [_dot1(jnp.concatenate([v_p[p], p_all[p]], axis=0),
                 jnp.concatenate([k_hat[:, sls[p]], -b_hat[:, sls[p]]], axis=0), TN) for p in ps]
    for p in ps:
        s_scr[p] = s_old[p] * e_end[:, sls[p]] + jnp.where(same_head, upd[p], 0.0)

    y = jnp.concatenate(ys, axis=1) if pg > 1 else ys[0]
    inv_n = 1.0 / N_B
    yc = y - seg_sum(y) * inv_n
    yn = yc * lax.rsqrt(seg_sum(yc * yc) * inv_n + LNX_EPS)
    yn = yn * lg_ref[...] + lb_ref[...]
    bonus = seg_sum(r * kh * rk_ref[...]) * v
    o_ref[...] = ((yn + bonus) * gate).astype(o_ref.dtype)

    @pl.when(t_idx == pl.num_programs(2) - 1)
    def _():
        for p in range(pg):
            s = s_scr[p]
            sout_ref[0, 2 * p] = s[:N_B, :N_B]
            sout_ref[0, 2 * p + 1] = s[N_B:, N_B:]


def rwkv_branch(p, prev, mu, w0, a0, k_k, k_a, r_k, lnx_g, lnx_b, w2p, a2p, g2, s0, b, t, c, pg):
    nc = t // c
    gw = pg * LANES
    ng = W_B // gw
    tw = XB_PAD - 3 * W_B
    tail_blk = 3 * W_B // tw
    row = lambda bi, g, ti: bi * nc + ti
    seg = lambda k: pl.BlockSpec((c, gw), lambda bi, g, ti: (row(bi, g, ti), P3_OFF // gw + k * ng + g))
    pseg = lambda k: pl.BlockSpec((1, 1, gw), lambda bi, g, ti: (bi, 0, k * ng + g))
    mseg = lambda k: pl.BlockSpec((1, gw), lambda bi, g, ti: (0, k * ng + g))
    chan = pl.BlockSpec((1, gw), lambda bi, g, ti: (0, g))
    state = pl.BlockSpec((1, 2 * pg, N_B, N_B), lambda bi, g, ti: (bi, g, 0, 0))
    return pl.pallas_call(
        functools.partial(_rwkv_kernel, c=c, pg=pg),
        grid=(b, ng, nc),
        in_specs=[seg(0), seg(1), seg(2),
                  pl.BlockSpec((c, tw), lambda bi, g, ti: (row(bi, g, ti), P3_OFF // tw + tail_blk)),
                  pseg(0), pseg(1), pseg(2),
                  pl.BlockSpec((1, 1, tw), lambda bi, g, ti: (bi, 0, tail_blk)),
                  mseg(0), mseg(1), mseg(2),
                  pl.BlockSpec((1, tw), lambda bi, g, ti: (0, tail_blk)),
                  chan, chan, chan, chan, chan, chan, chan,
                  pl.BlockSpec((LANES, gw), lambda bi, g, ti: (0, g)),
                  pl.BlockSpec((LANES, gw), lambda bi, g, ti: (0, g)),
                  pl.BlockSpec((LORA_G, gw), lambda bi, g, ti: (0, g)),
                  state],
        out_specs=[pl.BlockSpec((c, gw), lambda bi, g, ti: (row(bi, g, ti), g)), state],
        out_shape=[jax.ShapeDtypeStruct((b * t, W_B), BF16),
                   jax.ShapeDtypeStruct((b, H_B, N_B, N_B), F32)],
        scratch_shapes=[pltpu.VMEM((1, gw), F32)] * 3 + [pltpu.VMEM((1, tw), F32),
                                                        pltpu.VMEM((pg, LANES, LANES), F32)],
        compiler_params=_params("parallel", "parallel", "arbitrary"),
        name="rwkv_branch",
    )(p, p, p, p, prev, prev, prev, prev, mu, mu, mu, mu,
      w0, a0, k_k, k_a, r_k, lnx_g, lnx_b, w2p, a2p, g2, s0)


def _pad_xb(a, ab=None):
    lead = a.shape[:-1]
    z = lambda n: jnp.zeros(lead + (n,), a.dtype)
    o_w, o_a, o_g = 3 * W_B, 3 * W_B + LORA_W, 3 * W_B + LORA_W + LORA_A
    mid = z(2 * H_A) if ab is None else ab
    return jnp.concatenate([a[..., :o_a], mid, z(LANES - LORA_W - 2 * H_A),
                            a[..., o_a:o_g], z(LANES - LORA_A), a[..., o_g:]], axis=-1)


def _unpad_xb(a):
    o_a = 3 * W_B + LORA_W
    return jnp.concatenate([a[..., :o_a], a[..., 3 * W_B + LANES:3 * W_B + LANES + LORA_A],
                            a[..., 3 * W_B + 2 * LANES:]], axis=-1)


def _pad_rows(w, n):
    return jnp.concatenate([w, jnp.zeros((n - w.shape[0],) + w.shape[1:], w.dtype)], axis=0)


def _lane_vec(v, start):
    return jnp.zeros((1, LANES), F32).at[0, start:start + v.shape[0]].set(v)


def _prepare(lw):
    (g_mix, w_in, conv_w, a_log, dt_bias, gdn_norm_g, mu_shift, w0, w2, a0, a2, g2,
     k_k, k_a, r_k, lnx_g, lnx_b, w_up_a, w_up_b, w_o, g_mlp, w_ff1, w_ff2,
     g_ple, w_ple_gate, w_ple) = lw
    o_alpha = CONV_CH
    o_gate_a = CONV_CH + 2 * H_A
    o_xb = o_gate_a + W_A
    o_gm = o_xb + SHIFT_W
    row = lambda v: v.reshape(1, -1)
    return dict(
        g_mix=row(g_mix),
        w_proj=jnp.concatenate([w_in[:, :CONV_CH], w_in[:, o_gate_a:o_xb], w_in[:, o_gm:],
                                _pad_xb(w_in[:, o_xb:o_gm], w_in[:, o_alpha:o_gate_a])], axis=1).astype(BF16),
        conv_w=conv_w,
        alog_vec=_lane_vec(a_log, ALPHA_LANE), dt_vec=_lane_vec(dt_bias, ALPHA_LANE),
        gdn_norm_g=row(gdn_norm_g),
        mu=row(_pad_xb(mu_shift)),
        w0=row(w0), a0=row(a0), k_k=row(k_k), k_a=row(k_a), r_k=row(r_k.reshape(-1)),
        lnx_g=row(lnx_g), lnx_b=row(lnx_b),
        w2p=_pad_rows(w2, LANES).astype(BF16), a2p=_pad_rows(a2, LANES).astype(BF16), g2=g2.astype(BF16),
        w_up_a=w_up_a.astype(BF16), w_up_b=w_up_b.astype(BF16), w_o=w_o.astype(BF16),
        g_mlp=row(g_mlp), w_ff1=w_ff1.astype(BF16), w_ff2=w_ff2.astype(BF16),
        g_ple=row(g_ple), w_ple_gate=w_ple_gate.astype(BF16), w_ple=w_ple.astype(BF16),
    )


def _run_layer(x, ple, s_gdn, buf_gdn, s_rwkv, shift_rwkv, pw, g_final):
    b, t, _ = x.shape
    m = b * t
    c = min(CHUNK, t)
    assert t % c == 0 and c % SUBLANES == 0 and t >= CONV_W - 1
    tm_stream = min(m, 1024)
    tm_res = min(m, 256)
    x2 = x.reshape(m, D_MODEL)

    p = norm_matmul(x2, pw["g_mix"], pw["w_proj"], tm_stream, 512)

    hist8 = jnp.concatenate([jnp.zeros((b, SUBLANES - (CONV_W - 1), CONV_CH), F32), buf_gdn], axis=1)
    o_a, s_gdn_new = gdn_branch(p, hist8, pw["conv_w"], pw["alog_vec"], pw["dt_vec"],
                                pw["gdn_norm_g"], s_gdn, b, t, c, hb=GDN_HB)
    prev = _pad_xb(shift_rwkv).reshape(b, 1, XB_PAD)
    o_b, s_rwkv_new = rwkv_branch(p, prev, pw["mu"], pw["w0"], pw["a0"], pw["k_k"], pw["k_a"], pw["r_k"],
                                  pw["lnx_g"], pw["lnx_b"], pw["w2p"], pw["a2p"], pw["g2"], s_rwkv,
                                  b, t, c, pg=RWKV_PG)

    x1, h2 = merge_out(x2, o_a, o_b, p, P2_OFF // D_MODEL, pw["w_up_a"], pw["w_up_b"], pw["w_o"],
                       pw["g_mlp"], tm_res, 512)
    delta = mlp(h2, pw["w_ff1"], pw["w_ff2"], tm_stream, 1024, 512)
    y = ple_final(x1, delta, ple.reshape(m, D_PLE), pw["g_ple"], pw["w_ple_gate"], pw["w_ple"], g_final, tm_res)

    p3d = p.reshape(b, t, N_PROJ)
    new_buf = p3d[:, t - (CONV_W - 1):, P1_OFF:P1_OFF + CONV_CH]
    new_shift = _unpad_xb(p3d[:, t - 1, P3_OFF:])
    return y.reshape(b, t, D_MODEL), s_gdn_new, new_buf, s_rwkv_new, new_shift


def kernel(x_prompt, x_sample, p_prompt, p_sample, state_gdn, cache_gdn_conv, state_rwkv, cache_rwkv_shift, g_mix, w_in, conv_w, a_log, dt_bias, gdn_norm_g, mu_shift, w0, w2, a0, a2, g2, k_k, k_a, r_k, lnx_g, lnx_b, w_up_a, w_up_b, w_o, g_mlp, w_ff1, w_ff2, g_ple, w_ple_gate, w_ple, g_final):
    params = (g_mix, w_in, conv_w, a_log, dt_bias, gdn_norm_g, mu_shift, w0, w2, a0, a2, g2,
              k_k, k_a, r_k, lnx_g, lnx_b, w_up_a, w_up_b, w_o, g_mlp, w_ff1, w_ff2,
              g_ple, w_ple_gate, w_ple)
    depth = w_in.shape[0]
    assert depth == 1
    pw = _prepare(tuple(w[0] for w in params))
    gf = g_final.reshape(1, -1)
    bp = x_prompt.shape[0]
    z_sg = jnp.zeros((bp, H_A, DK_A, DV_A), F32)
    z_buf = jnp.zeros((bp, CONV_W - 1, CONV_CH), F32)
    z_sr = jnp.zeros((bp, H_B, N_B, N_B), F32)
    z_sh = jnp.zeros((bp, SHIFT_W), F32)
    y_p, sg_p, buf_p, sr_p, sh_p = _run_layer(x_prompt, p_prompt[0], z_sg, z_buf, z_sr, z_sh, pw, gf)
    y_s, sg_s, buf_s, sr_s, sh_s = _run_layer(x_sample, p_sample[0], state_gdn[0], cache_gdn_conv[0],
                                              state_rwkv[0], cache_rwkv_shift[0], pw, gf)
    st = lambda a: a[None]
    return (y_p, y_s, st(sg_p), st(buf_p), st(sr_p), st(sh_p), st(sg_s), st(buf_s), st(sr_s), st(sh_s))
```

```python
import functools

import jax
import jax.numpy as jnp
from jax import lax
from jax.experimental import pallas as pl
from jax.experimental.pallas import tpu as pltpu

F32 = jnp.float32
BF16 = jnp.bfloat16

D_MODEL = 2048
H_A, DK_A, DV_A = 8, 128, 128
W_A = H_A * DV_A
CONV_W = 4
CONV_CH = 3 * W_A
H_B, N_B = 16, 64
W_B = H_B * N_B
LORA_W, LORA_A, LORA_G = 96, 96, 256
SHIFT_W = 3 * W_B + LORA_W + LORA_A + LORA_G
D_FF = 4 * D_MODEL
D_PLE = 256
EPS = 1e-6
LNX_EPS = 64e-5
CHUNK = 64
GDN_HB = 8
RWKV_PG = 8
SEQS_PER_STEP = 4

LANES = 128
SUBLANES = 8
VMEM_LIMIT = 52 * 2 ** 20

XB_PAD = 3 * W_B + 2 * LANES + LORA_G
AB_BLOCK = 3 * W_B // LANES
ALPHA_LANE = LORA_W
BETA_LANE = LORA_W + H_A

P1_OFF, P2_OFF, P3_OFF = 0, 4 * W_A, 4 * W_A + 2 * D_MODEL
N_PROJ = P3_OFF + XB_PAD

NN = ((1,), (0,))
NT = ((1,), (1,))
TN = ((0,), (0,))


def _dg(a, b, dims):
    return lax.dot_general(a, b, (dims, ((), ())), preferred_element_type=F32)


def _dot1(a, b, dims=NN):
    return _dg(a.astype(BF16), b.astype(BF16), dims)


def _cumsum_rows(x, c):
    rows = x.shape[0]
    i = lax.broadcasted_iota(jnp.int32, (rows, rows), 0)
    j = lax.broadcasted_iota(jnp.int32, (rows, rows), 1)
    same_chunk = (i ^ j) < c
    tri = jnp.where((i >= j) & same_chunk, 1.0, 0.0).astype(BF16)
    h1 = x.astype(BF16)
    r1 = x - h1.astype(F32)
    h2 = r1.astype(BF16)
    h3 = (r1 - h2.astype(F32)).astype(BF16)
    return _dg(tri, h1, NN) + (_dg(tri, h2, NN) + _dg(tri, h3, NN))


def _tri_inv_many(a_list, c):
    i = lax.broadcasted_iota(jnp.int32, (c, c), 0)
    j = lax.broadcasted_iota(jnp.int32, (c, c), 1)
    ns = [-jnp.where((i ^ j) == 1, a, 0.0) for a in a_list]
    s, ls = 2, 1
    while s < c:
        mask = ((i ^ j) >> ls) == 1
        l_s = [jnp.where(mask, a, 0.0) for a in a_list]
        nl = [_dot1(n, l) for n, l in zip(ns, l_s)]
        ys = [l + x for l, x in zip(l_s, nl)]
        yn = [_dot1(y, n) for y, n in zip(ys, ns)]
        ns = [n - (y + z) for n, y, z in zip(ns, ys, yn)]
        s, ls = 2 * s, ls + 1
    eye = jnp.where(i == j, 1.0, 0.0)
    return [eye + n for n in ns]


def _pair_stack(x, c):
    n = x.shape[1]
    bi = lax.broadcasted_iota(jnp.int32, (2 * c, n), 0)
    bj = lax.broadcasted_iota(jnp.int32, (2 * c, n), 1)
    keep = (bi >= c) == (bj >= n // 2)
    return jnp.where(keep, jnp.concatenate([x, x], axis=0), 0.0)


def _tri_inv_packed(a_list, c):
    i = lax.broadcasted_iota(jnp.int32, (c, 2 * c), 0)
    jm = lax.broadcasted_iota(jnp.int32, (c, 2 * c), 1) & (c - 1)
    ns = [-jnp.where((i ^ jm) == 1, a, 0.0) for a in a_list]
    s, ls = 2, 1
    while s < c:
        mask = ((i ^ jm) >> ls) == 1
        l_s = [jnp.where(mask, a, 0.0) for a in a_list]
        nl = [_dot1(n, _pair_stack(l, c)) for n, l in zip(ns, l_s)]
        ys = [l + x for l, x in zip(l_s, nl)]
        yn = [_dot1(y, _pair_stack(n, c)) for y, n in zip(ys, ns)]
        ns = [n - (y + z) for n, y, z in zip(ns, ys, yn)]
        s, ls = 2 * s, ls + 1
    eye = jnp.where(i == jm, 1.0, 0.0)
    return [eye + n for n in ns]


def _softplus(x):
    return jnp.maximum(x, 0.0) + jnp.log1p(jnp.exp(-jnp.abs(x)))


def _sigmoid(x):
    return jax.nn.sigmoid(x)


def _rms(x, g):
    return x * lax.rsqrt(jnp.mean(x * x, axis=-1, keepdims=True) + EPS) * g


def _params(*sem):
    return pltpu.CompilerParams(dimension_semantics=sem, vmem_limit_bytes=VMEM_LIMIT)


def _norm_matmul_kernel(x_ref, g_ref, w_ref, o_ref, h_scr):
    @pl.when(pl.program_id(1) == 0)
    def _():
        h_scr[...] = _rms(x_ref[...], g_ref[...]).astype(BF16)

    o_ref[...] = _dg(h_scr[...], w_ref[...], NN).astype(o_ref.dtype)


def norm_matmul(x, g, w, tm, tn, out_dtype=F32):
    m, k = x.shape
    n = w.shape[1]
    return pl.pallas_call(
        _norm_matmul_kernel,
        grid=(m // tm, n // tn),
        in_specs=[pl.BlockSpec((tm, k), lambda i, j: (i, 0), pipeline_mode=pl.Buffered(1)),
                  pl.BlockSpec((1, k), lambda i, j: (0, 0)),
                  pl.BlockSpec((k, tn), lambda i, j: (0, j))],
        out_specs=pl.BlockSpec((tm, tn), lambda i, j: (i, j)),
        out_shape=jax.ShapeDtypeStruct((m, n), out_dtype),
        scratch_shapes=[pltpu.VMEM((tm, k), BF16)],
        compiler_params=_params("parallel", "arbitrary"),
        name="norm_matmul",
    )(x, g, w)


def _merge_kernel(x_ref, oa_ref, ob_ref, gma_ref, gmb_ref, wa_ref, wb_ref, wo_ref, g_ref, o_ref, h_ref, *, tn):
    oa = oa_ref[...]
    ob = ob_ref[...]
    acc = x_ref[...]
    for n in range(D_MODEL // tn):
        cs = slice(n * tn, (n + 1) * tn)
        merged = (_sigmoid(gma_ref[:, cs]) * _dg(oa, wa_ref[:, cs], NN)
                  + _sigmoid(gmb_ref[:, cs]) * _dg(ob, wb_ref[:, cs], NN))
        acc = acc + _dg(merged.astype(BF16), wo_ref[cs, :], NN)
    o_ref[...] = acc
    h_ref[...] = _rms(acc, g_ref[...]).astype(BF16)


def merge_out(x, o_a, o_b, p, gate_blk, w_up_a, w_up_b, w_o, g_mlp, tm, tn):
    m = x.shape[0]
    resident = lambda shape: pl.BlockSpec(shape, lambda i: (0, 0), pipeline_mode=pl.Buffered(1))
    return pl.pallas_call(
        functools.partial(_merge_kernel, tn=tn),
        grid=(m // tm,),
        in_specs=[pl.BlockSpec((tm, D_MODEL), lambda i: (i, 0)),
                  pl.BlockSpec((tm, W_A), lambda i: (i, 0)),
                  pl.BlockSpec((tm, W_B), lambda i: (i, 0)),
                  pl.BlockSpec((tm, D_MODEL), lambda i: (i, gate_blk)),
                  pl.BlockSpec((tm, D_MODEL), lambda i: (i, gate_blk + 1)),
                  resident((W_A, D_MODEL)), resident((W_B, D_MODEL)), resident((D_MODEL, D_MODEL)),
                  pl.BlockSpec((1, D_MODEL), lambda i: (0, 0))],
        out_specs=[pl.BlockSpec((tm, D_MODEL), lambda i: (i, 0)),
                   pl.BlockSpec((tm, D_MODEL), lambda i: (i, 0))],
        out_shape=[jax.ShapeDtypeStruct((m, D_MODEL), F32), jax.ShapeDtypeStruct((m, D_MODEL), BF16)],
        compiler_params=_params("parallel"),
        name="merge_out",
    )(x, o_a, o_b, p, p, w_up_a, w_up_b, w_o, g_mlp)


def _mlp_kernel(h_ref, w1_ref, w2_ref, o_ref, *, tc):
    f = pl.program_id(1)

    @pl.when(f == 0)
    def _():
        o_ref[...] = jnp.zeros_like(o_ref)

    h = h_ref[...]
    for fc in range(w1_ref.shape[1] // tc):
        fs = slice(fc * tc, (fc + 1) * tc)
        a = jnp.maximum(_dg(h, w1_ref[:, fs], NN), 0.0)
        a = (a * a).astype(BF16)
        for nc in range(D_MODEL // tc):
            ns = slice(nc * tc, (nc + 1) * tc)
            o_ref[:, ns] += _dg(a, w2_ref[fs, ns], NN)


def mlp(h, w1, w2, tm, tf, tc):
    m = h.shape[0]
    return pl.pallas_call(
        functools.partial(_mlp_kernel, tc=tc),
        grid=(m // tm, D_FF // tf),
        in_specs=[pl.BlockSpec((tm, D_MODEL), lambda i, f: (i, 0)),
                  pl.BlockSpec((D_MODEL, tf), lambda i, f: (0, f)),
                  pl.BlockSpec((tf, D_MODEL), lambda i, f: (f, 0))],
        out_specs=pl.BlockSpec((tm, D_MODEL), lambda i, f: (i, 0)),
        out_shape=jax.ShapeDtypeStruct((m, D_MODEL), F32),
        compiler_params=_params("parallel", "arbitrary"),
        name="mlp",
    )(h, w1, w2)


def _ple_final_kernel(x_ref, d_ref, p_ref, gp_ref, wg_ref, wp_ref, gf_ref, o_ref):
    x = x_ref[...] + d_ref[...]
    gate = _sigmoid(_dg(_rms(x, gp_ref[...]).astype(BF16), wg_ref[...], NN))
    x = x + gate * _dg(p_ref[...].astype(BF16), wp_ref[...], NN)
    o_ref[...] = _rms(x, gf_ref[...])


def ple_final(x, delta, ple, g_ple, w_gate, w_ple, g_final, tm):
    m = x.shape[0]
    resident = lambda shape: pl.BlockSpec(shape, lambda i: (0, 0), pipeline_mode=pl.Buffered(1))
    return pl.pallas_call(
        _ple_final_kernel,
        grid=(m // tm,),
        in_specs=[pl.BlockSpec((tm, D_MODEL), lambda i: (i, 0)),
                  pl.BlockSpec((tm, D_MODEL), lambda i: (i, 0)),
                  pl.BlockSpec((tm, D_PLE), lambda i: (i, 0)),
                  pl.BlockSpec((1, D_MODEL), lambda i: (0, 0)),
                  resident((D_MODEL, D_MODEL)), resident((D_PLE, D_MODEL)),
                  pl.BlockSpec((1, D_MODEL), lambda i: (0, 0))],
        out_specs=pl.BlockSpec((tm, D_MODEL), lambda i: (i, 0)),
        out_shape=jax.ShapeDtypeStruct((m, D_MODEL), F32),
        compiler_params=_params("parallel"),
        name="ple_final",
    )(x, delta, ple, g_ple, w_gate, w_ple, g_final)


def _gdn_kernel(q_ref, k_ref, v_ref, gate_ref, ab_ref, hq_ref, hk_ref, hv_ref,
                cq_ref, ck_ref, cv_ref, alog_ref, dt_ref, ng_ref, s0_ref,
                o_ref, sout_ref, qs, ks, vs, s_scr, *, c, hb, nseq):
    g_idx = pl.program_id(1)
    t_idx = pl.program_id(2)
    hist = SUBLANES
    seqs = range(nseq)

    @pl.when(t_idx == 0)
    def _():
        for s in seqs:
            for h in range(hb):
                s_scr[s * hb + h] = s0_ref[s, h]
            qs[s, 0:hist, :] = hq_ref[s]
            ks[s, 0:hist, :] = hk_ref[s]
            vs[s, 0:hist, :] = hv_ref[s]

    def conv_silu(scr, x_ref, w_ref):
        outs = []
        for s in seqs:
            scr[s, hist:hist + c, :] = x_ref[s]
            out = scr[s, hist - 3:hist - 3 + c, :] * w_ref[0:1, :]
            for j in range(1, CONV_W):
                out = out + scr[s, hist - 3 + j:hist - 3 + j + c, :] * w_ref[j:j + 1, :]
            scr[s, 0:hist, :] = scr[s, c:c + hist, :]
            outs.append(out)
        out = jnp.concatenate(outs, axis=0) if nseq > 1 else outs[0]
        return out * _sigmoid(out)

    qc = conv_silu(qs, q_ref, cq_ref)
    kc = conv_silu(ks, k_ref, ck_ref)
    vc = conv_silu(vs, v_ref, cv_ref)

    ab = ab_ref[...].reshape(nseq * c, LANES)
    g_all = -jnp.exp(alog_ref[...]) * _softplus(ab + dt_ref[...])
    beta_all = _sigmoid(ab)
    gcum = _cumsum_rows(g_all, c)
    lane = lax.broadcasted_iota(jnp.int32, (c, LANES), 1)
    sub_t = lax.broadcasted_iota(jnp.int32, (LANES, c), 0)
    i = lax.broadcasted_iota(jnp.int32, (c, c), 0)
    j = lax.broadcasted_iota(jnp.int32, (c, c), 1)
    incl = i >= j
    strict = i > j
    last_row = lax.broadcasted_iota(jnp.int32, (c, 1), 0) == c - 1
    gate = gate_ref[...].reshape(nseq * c, hb * DV_A)
    ng = ng_ref[...]

    hs = range(hb)
    sls = [slice(h * DK_A, (h + 1) * DK_A) for h in hs]
    rs = [slice(s * c, (s + 1) * c) for s in seqs]
    heads = [g_idx * hb + h for h in hs]
    items = [(s, h) for s in seqs for h in hs]
    gcum_s = [gcum[r] for r in rs]
    gcum_t = [g.T for g in gcum_s]
    beta_s = [beta_all[r] for r in rs]
    qn = [qc[:, sl] for sl in sls]
    kn = [kc[:, sl] for sl in sls]
    qn = [x * lax.rsqrt(jnp.sum(x * x, axis=-1, keepdims=True) + EPS) * (DK_A ** -0.5) for x in qn]
    kn = [x * lax.rsqrt(jnp.sum(x * x, axis=-1, keepdims=True) + EPS) for x in kn]
    gc = {(s, h): jnp.sum(jnp.where(lane == ALPHA_LANE + heads[h], gcum_s[s], 0.0), axis=1, keepdims=True)
          for s, h in items}
    gr = {(s, h): jnp.sum(jnp.where(sub_t == ALPHA_LANE + heads[h], gcum_t[s], 0.0), axis=0, keepdims=True)
          for s, h in items}
    bc = {(s, h): jnp.sum(jnp.where(lane == BETA_LANE + heads[h], beta_s[s], 0.0), axis=1, keepdims=True)
          for s, h in items}
    qh = {(s, h): qn[h][rs[s]] for s, h in items}
    kh = {(s, h): kn[h][rs[s]] for s, h in items}
    vh = {(s, h): vc[rs[s], sls[h]] for s, h in items}
    dec = {n: jnp.exp(jnp.where(incl, gc[n] - gr[n], -jnp.inf)) for n in items}
    kb = {n: kh[n] * bc[n] for n in items}
    kk = {n: _dot1(kb[n], kh[n], NT) for n in items}
    t_inv = dict(zip(items, _tri_inv_many([jnp.where(strict, kk[n] * dec[n], 0.0) for n in items], c)))
    eg = {n: jnp.exp(gc[n]) for n in items}
    uw = {n: _dot1(t_inv[n], jnp.concatenate([vh[n] * bc[n], kb[n] * eg[n]], axis=1)) for n in items}
    qk = {n: jnp.where(incl, _dot1(qh[n], kh[n], NT) * dec[n], 0.0) for n in items}
    g_last = {n: jnp.sum(jnp.where(last_row, gc[n], 0.0), axis=0, keepdims=True) for n in items}
    kd = {n: kh[n] * jnp.exp(g_last[n] - gc[n]) for n in items}
    s_old = {(s, h): s_scr[s * hb + h] for s, h in items}
    ws_qs = {n: _dot1(jnp.concatenate([uw[n][:, DV_A:], qh[n] * eg[n]], axis=0), s_old[n]) for n in items}
    v_new = {n: uw[n][:, :DV_A] - ws_qs[n][:c] for n in items}
    o = {n: ws_qs[n][c:] + _dot1(qk[n], v_new[n]) for n in items}
    upd = {n: _dot1(kd[n], v_new[n], TN) for n in items}
    for s, h in items:
        s_scr[s * hb + h] = s_old[s, h] * jnp.exp(g_last[s, h]) + upd[s, h]
    for s, h in items:
        gh = gate[rs[s], sls[h]]
        o_ref[s, :, sls[h]] = (_rms(o[s, h], ng) * (gh * _sigmoid(gh))).astype(o_ref.dtype)

    @pl.when(t_idx == pl.num_programs(2) - 1)
    def _():
        for s in seqs:
            for h in range(hb):
                sout_ref[s, h] = s_scr[s * hb + h]


def gdn_branch(p, hist8, conv_w, alog_vec, dt_vec, norm_g, s0, b, t, c, hb, nseq):
    nc = t // c
    ng = H_A // hb
    gw = hb * DK_A
    seg = lambda k: pl.BlockSpec((nseq, c, gw), lambda bi, g, ti: (bi, ti, P1_OFF // gw + k * ng + g))
    hist = lambda k: pl.BlockSpec((nseq, SUBLANES, gw), lambda bi, g, ti: (bi, 0, k * ng + g))
    cw = lambda k: pl.BlockSpec((CONV_W, gw), lambda bi, g, ti: (0, k * ng + g))
    vec = pl.BlockSpec((1, LANES), lambda bi, g, ti: (0, 0))
    state = pl.BlockSpec((nseq, hb, DK_A, DV_A), lambda bi, g, ti: (bi, g, 0, 0))
    return pl.pallas_call(
        functools.partial(_gdn_kernel, c=c, hb=hb, nseq=nseq),
        grid=(b // nseq, ng, nc),
        in_specs=[seg(0), seg(1), seg(2), seg(3),
                  pl.BlockSpec((nseq, c, LANES), lambda bi, g, ti: (bi, ti, P3_OFF // LANES + AB_BLOCK)),
                  hist(0), hist(1), hist(2), cw(0), cw(1), cw(2), vec, vec, vec, state],
        out_specs=[pl.BlockSpec((nseq, c, gw), lambda bi, g, ti: (bi, ti, g)), state],
        out_shape=[jax.ShapeDtypeStruct((b, t, W_A), BF16),
                   jax.ShapeDtypeStruct((b, H_A, DK_A, DV_A), F32)],
        scratch_shapes=[pltpu.VMEM((nseq, c + SUBLANES, gw), F32)] * 3 + [pltpu.VMEM((nseq * hb, DK_A, DV_A), F32)],
        compiler_params=_params("parallel", "parallel", "arbitrary"),
        name="gdn_branch",
    )(p, p, p, p, p, hist8, hist8, hist8, conv_w, conv_w, conv_w, alog_vec, dt_vec, norm_g, s0)


def _rwkv_kernel(r_ref, k_ref, v_ref, tail_ref, pr_ref, pk_ref, pv_ref, pt_ref,
                 mr_ref, mk_ref, mv_ref, mt_ref, w0_ref, a0_ref, kk_ref, ka_ref, rk_ref,
                 lg_ref, lb_ref, w2_ref, a2_ref, g2_ref, s0_ref,
                 o_ref, sout_ref, prev_r, prev_k, prev_v, prev_t, s_scr, *, c, pg, nseq):
    t_idx = pl.program_id(2)
    seqs = range(nseq)
    rows = nseq * c

    @pl.when(t_idx == 0)
    def _():
        prev_r[...] = pr_ref[...]
        prev_k[...] = pk_ref[...]
        prev_v[...] = pv_ref[...]
        prev_t[...] = pt_ref[...]
        z = jnp.zeros((N_B, N_B), F32)
        for s in seqs:
            for p in range(pg):
                s_a = s0_ref[s, 2 * p]
                s_b = s0_ref[s, 2 * p + 1]
                s_scr[s * pg + p] = jnp.concatenate([jnp.concatenate([s_a, z], axis=1),
                                                     jnp.concatenate([z, s_b], axis=1)], axis=0)

    def mix(x_ref, prev, mu_ref):
        outs = []
        for s in seqs:
            x = x_ref[s]
            row = lax.broadcasted_iota(jnp.int32, x.shape, 0)
            shifted = jnp.where(row == 0, prev[s], pltpu.roll(x, 1, axis=0))
            prev[s] = x_ref[s, c - 1:c, :]
            outs.append(x + (shifted - x) * mu_ref[...])
        return jnp.concatenate(outs, axis=0) if nseq > 1 else outs[0]

    r = mix(r_ref, prev_r, mr_ref)
    k = mix(k_ref, prev_k, mk_ref)
    v = mix(v_ref, prev_v, mv_ref)
    tail = mix(tail_ref, prev_t, mt_ref)
    xw = tail[:, 0:LANES]
    xa = tail[:, LANES:2 * LANES]
    xg = tail[:, 2 * LANES:]

    w_log = -_softplus(-(w0_ref[...] + _dot1(jnp.tanh(xw), w2_ref[...]))) - 0.5
    lw = -jnp.exp(w_log)
    a = _sigmoid(a0_ref[...] + _dot1(xa, a2_ref[...]))
    gate = _dot1(_sigmoid(xg), g2_ref[...])

    gw = pg * LANES
    fp = (lax.broadcasted_iota(jnp.int32, (rows, LANES), 1) & N_B) == 0

    def seg_sum(x):
        outs = []
        for p in range(pg):
            xp = x[:, p * LANES:(p + 1) * LANES]
            s_a = jnp.sum(jnp.where(fp, xp, 0.0), axis=-1, keepdims=True)
            s_b = jnp.sum(jnp.where(fp, 0.0, xp), axis=-1, keepdims=True)
            outs.append(jnp.where(fp, s_a, s_b))
        return jnp.concatenate(outs, axis=1) if pg > 1 else outs[0]

    kkx = k * kk_ref[...]
    kk = kkx * lax.rsqrt(seg_sum(kkx * kkx) + EPS)
    kh = k * (1.0 + (a - 1.0) * ka_ref[...])
    bb = kk * a
    cw = _cumsum_rows(lw, c)
    rt = r * jnp.exp(cw)
    e_neg = jnp.exp(-cw)
    kt = kh * e_neg
    bt = bb * e_neg
    at = kk * jnp.exp(cw - lw)
    row_c = lax.broadcasted_iota(jnp.int32, (c, gw), 0)
    rs = [slice(s * c, (s + 1) * c) for s in seqs]
    cw_s = [cw[rc] for rc in rs]
    cw_last = [jnp.sum(jnp.where(row_c == c - 1, x, 0.0), axis=0, keepdims=True) for x in cw_s]
    e_last = [jnp.exp(cw_last[s] - cw_s[s]) for s in seqs]
    k_hat = [kh[rs[s]] * e_last[s] for s in seqs]
    b_hat = [bb[rs[s]] * e_last[s] for s in seqs]
    e_end = [jnp.exp(x) for x in cw_last]

    i = lax.broadcasted_iota(jnp.int32, (c, 2 * c), 0)
    jm = lax.broadcasted_iota(jnp.int32, (c, 2 * c), 1) & (c - 1)
    incl = i >= jm
    strict = i > jm
    bi = lax.broadcasted_iota(jnp.int32, (LANES, LANES), 0)
    bj = lax.broadcasted_iota(jnp.int32, (LANES, LANES), 1)
    same_head = ((bi ^ bj) & N_B) == 0

    ps = range(pg)
    sls = [slice(p * LANES, (p + 1) * LANES) for p in ps]
    items = [(s, p) for s in seqs for p in ps]
    at_p = {(s, p): at[rs[s], sls[p]] for s, p in items}
    rt_p = {(s, p): rt[rs[s], sls[p]] for s, p in items}
    v_p = {(s, p): v[rs[s], sls[p]] for s, p in items}
    s_old = {(s, p): s_scr[s * pg + p] for s, p in items}
    a_s = {n: _dot1(at_p[n], s_old[n], NT) for n in items}
    r_s = {n: _dot1(rt_p[n], s_old[n], NT) for n in items}
    lhs = {n: jnp.concatenate([at_p[n], rt_p[n]], axis=0) for n in items}
    xb_ = {(s, p): _dot1(lhs[s, p], _pair_stack(bt[rs[s], sls[p]], c), NT) for s, p in items}
    xk_ = {(s, p): _dot1(lhs[s, p], _pair_stack(kt[rs[s], sls[p]], c), NT) for s, p in items}
    l_ak = {n: jnp.where(strict, xk_[n][:c], 0.0) for n in items}
    r_b = {n: jnp.where(incl, xb_[n][c:], 0.0) for n in items}
    r_k = {n: jnp.where(incl, xk_[n][c:], 0.0) for n in items}
    t_inv = dict(zip(items, _tri_inv_packed([jnp.where(strict, xb_[n][:c], 0.0) for n in items], c)))
    v_st = {n: _pair_stack(v_p[n], c) for n in items}
    rhs = {n: a_s[n] + _dot1(l_ak[n], v_st[n]) for n in items}
    p_all = {n: _dot1(t_inv[n], _pair_stack(rhs[n], c)) for n in items}
    ys = {n: r_s[n] + _dot1(r_k[n], v_st[n]) - _dot1(r_b[n], _pair_stack(p_all[n], c)) for n in items}
    upd = {(s, p): _dot1(jnp.concatenate([v_p[s, p], p_all[s, p]], axis=0),
                         jnp.concatenate([k_hat[s][:, sls[p]], -b_hat[s][:, sls[p]]], axis=0), TN)
           for s, p in items}
    for s, p in items:
        s_scr[s * pg + p] = s_old[s, p] * e_end[s][:, sls[p]] + jnp.where(same_head, upd[s, p], 0.0)

    y = jnp.concatenate([jnp.concatenate([ys[s, p] for p in ps], axis=1) if pg > 1 else ys[s, 0]
                         for s in seqs], axis=0)
    inv_n = 1.0 / N_B
    yc = y - seg_sum(y) * inv_n
    yn = yc * lax.rsqrt(seg_sum(yc * yc) * inv_n + LNX_EPS)
    yn = yn * lg_ref[...] + lb_ref[...]
    bonus = seg_sum(r * kh * rk_ref[...]) * v
    o_ref[...] = ((yn + bonus) * gate).astype(o_ref.dtype).reshape(nseq, c, gw)

    @pl.when(t_idx == pl.num_programs(2) - 1)
    def _():
        for s in seqs:
            for p in range(pg):
                st = s_scr[s * pg + p]
                sout_ref[s, 2 * p] = st[:N_B, :N_B]
                sout_ref[s, 2 * p + 1] = st[N_B:, N_B:]


def rwkv_branch(p, prev, mu, w0, a0, k_k, k_a, r_k, lnx_g, lnx_b, w2p, a2p, g2, s0, b, t, c, pg, nseq):
    nc = t // c
    gw = pg * LANES
    ng = W_B // gw
    tw = XB_PAD - 3 * W_B
    tail_blk = 3 * W_B // tw
    seg = lambda k: pl.BlockSpec((nseq, c, gw), lambda bi, g, ti: (bi, ti, P3_OFF // gw + k * ng + g))
    pseg = lambda k: pl.BlockSpec((nseq, 1, gw), lambda bi, g, ti: (bi, 0, k * ng + g))
    mseg = lambda k: pl.BlockSpec((1, gw), lambda bi, g, ti: (0, k * ng + g))
    chan = pl.BlockSpec((1, gw), lambda bi, g, ti: (0, g))
    state = pl.BlockSpec((nseq, 2 * pg, N_B, N_B), lambda bi, g, ti: (bi, g, 0, 0))
    return pl.pallas_call(
        functools.partial(_rwkv_kernel, c=c, pg=pg, nseq=nseq),
        grid=(b // nseq, ng, nc),
        in_specs=[seg(0), seg(1), seg(2),
                  pl.BlockSpec((nseq, c, tw), lambda bi, g, ti: (bi, ti, P3_OFF // tw + tail_blk)),
                  pseg(0), pseg(1), pseg(2),
                  pl.BlockSpec((nseq, 1, tw), lambda bi, g, ti: (bi, 0, tail_blk)),
                  mseg(0), mseg(1), mseg(2),
                  pl.BlockSpec((1, tw), lambda bi, g, ti: (0, tail_blk)),
                  chan, chan, chan, chan, chan, chan, chan,
                  pl.BlockSpec((LANES, gw), lambda bi, g, ti: (0, g)),
                  pl.BlockSpec((LANES, gw), lambda bi, g, ti: (0, g)),
                  pl.BlockSpec((LORA_G, gw), lambda bi, g, ti: (0, g)),
                  state],
        out_specs=[pl.BlockSpec((nseq, c, gw), lambda bi, g, ti: (bi, ti, g)), state],
        out_shape=[jax.ShapeDtypeStruct((b, t, W_B), BF16),
                   jax.ShapeDtypeStruct((b, H_B, N_B, N_B), F32)],
        scratch_shapes=[pltpu.VMEM((nseq, 1, gw), F32)] * 3 + [pltpu.VMEM((nseq, 1, tw), F32),
                                                               pltpu.VMEM((nseq * pg, LANES, LANES), F32)],
        compiler_params=_params("parallel", "parallel", "arbitrary"),
        name="rwkv_branch",
    )(p, p, p, p, prev, prev, prev, prev, mu, mu, mu, mu,
      w0, a0, k_k, k_a, r_k, lnx_g, lnx_b, w2p, a2p, g2, s0)


def _pad_xb(a, ab=None):
    lead = a.shape[:-1]
    z = lambda n: jnp.zeros(lead + (n,), a.dtype)
    o_w, o_a, o_g = 3 * W_B, 3 * W_B + LORA_W, 3 * W_B + LORA_W + LORA_A
    mid = z(2 * H_A) if ab is None else ab
    return jnp.concatenate([a[..., :o_a], mid, z(LANES - LORA_W - 2 * H_A),
                            a[..., o_a:o_g], z(LANES - LORA_A), a[..., o_g:]], axis=-1)


def _unpad_xb(a):
    o_a = 3 * W_B + LORA_W
    return jnp.concatenate([a[..., :o_a], a[..., 3 * W_B + LANES:3 * W_B + LANES + LORA_A],
                            a[..., 3 * W_B + 2 * LANES:]], axis=-1)


def _pad_rows(w, n):
    return jnp.concatenate([w, jnp.zeros((n - w.shape[0],) + w.shape[1:], w.dtype)], axis=0)


def _lane_vec(v, start):
    return jnp.zeros((1, LANES), F32).at[0, start:start + v.shape[0]].set(v)


def _prepare(lw):
    (g_mix, w_in, conv_w, a_log, dt_bias, gdn_norm_g, mu_shift, w0, w2, a0, a2, g2,
     k_k, k_a, r_k, lnx_g, lnx_b, w_up_a, w_up_b, w_o, g_mlp, w_ff1, w_ff2,
     g_ple, w_ple_gate, w_ple) = lw
    o_alpha = CONV_CH
    o_gate_a = CONV_CH + 2 * H_A
    o_xb = o_gate_a + W_A
    o_gm = o_xb + SHIFT_W
    row = lambda v: v.reshape(1, -1)
    return dict(
        g_mix=row(g_mix),
        w_proj=jnp.concatenate([w_in[:, :CONV_CH], w_in[:, o_gate_a:o_xb], w_in[:, o_gm:],
                                _pad_xb(w_in[:, o_xb:o_gm], w_in[:, o_alpha:o_gate_a])], axis=1).astype(BF16),
        conv_w=conv_w,
        alog_vec=_lane_vec(a_log, ALPHA_LANE), dt_vec=_lane_vec(dt_bias, ALPHA_LANE),
        gdn_norm_g=row(gdn_norm_g),
        mu=row(_pad_xb(mu_shift)),
        w0=row(w0), a0=row(a0), k_k=row(k_k), k_a=row(k_a), r_k=row(r_k.reshape(-1)),
        lnx_g=row(lnx_g), lnx_b=row(lnx_b),
        w2p=_pad_rows(w2, LANES).astype(BF16), a2p=_pad_rows(a2, LANES).astype(BF16), g2=g2.astype(BF16),
        w_up_a=w_up_a.astype(BF16), w_up_b=w_up_b.astype(BF16), w_o=w_o.astype(BF16),
        g_mlp=row(g_mlp), w_ff1=w_ff1.astype(BF16), w_ff2=w_ff2.astype(BF16),
        g_ple=row(g_ple), w_ple_gate=w_ple_gate.astype(BF16), w_ple=w_ple.astype(BF16),
    )


def _run_layer(x, ple, s_gdn, buf_gdn, s_rwkv, shift_rwkv, pw, g_final):
    b, t, _ = x.shape
    m = b * t
    c = min(CHUNK, t)
    assert t % c == 0 and c % SUBLANES == 0 and t >= CONV_W - 1 and b % SEQS_PER_STEP == 0
    tm_proj = min(m, 2048)
    tm_stream = min(m, 1024)
    tm_res = min(m, 256)
    x2 = x.reshape(m, D_MODEL)

    p = norm_matmul(x2, pw["g_mix"], pw["w_proj"], tm_proj, 512)
    p3d = p.reshape(b, t, N_PROJ)

    hist8 = jnp.concatenate([jnp.zeros((b, SUBLANES - (CONV_W - 1), CONV_CH), F32), buf_gdn], axis=1)
    o_a, s_gdn_new = gdn_branch(p3d, hist8, pw["conv_w"], pw["alog_vec"], pw["dt_vec"],
                                pw["gdn_norm_g"], s_gdn, b, t, c, hb=GDN_HB, nseq=SEQS_PER_STEP)
    prev = _pad_xb(shift_rwkv).reshape(b, 1, XB_PAD)
    o_b, s_rwkv_new = rwkv_branch(p3d, prev, pw["mu"], pw["w0"], pw["a0"], pw["k_k"], pw["k_a"], pw["r_k"],
                                  pw["lnx_g"], pw["lnx_b"], pw["w2p"], pw["a2p"], pw["g2"], s_rwkv,
                                  b, t, c, pg=RWKV_PG, nseq=SEQS_PER_STEP)

    x1, h2 = merge_out(x2, o_a.reshape(m, W_A), o_b.reshape(m, W_B), p, P2_OFF // D_MODEL,
                       pw["w_up_a"], pw["w_up_b"], pw["w_o"], pw["g_mlp"], tm_res, 512)
    delta = mlp(h2, pw["w_ff1"], pw["w_ff2"], tm_stream, 1024, 512)
    y = ple_final(x1, delta, ple.reshape(m, D_PLE), pw["g_ple"], pw["w_ple_gate"], pw["w_ple"], g_final, tm_res)

    new_buf = p3d[:, t - (CONV_W - 1):, P1_OFF:P1_OFF + CONV_CH]
    new_shift = _unpad_xb(p3d[:, t - 1, P3_OFF:])
    return y.reshape(b, t, D_MODEL), s_gdn_new, new_buf, s_rwkv_new, new_shift


def kernel(x_prompt, x_sample, p_prompt, p_sample, state_gdn, cache_gdn_conv, state_rwkv, cache_rwkv_shift, g_mix, w_in, conv_w, a_log, dt_bias, gdn_norm_g, mu_shift, w0, w2, a0, a2, g2, k_k, k_a, r_k, lnx_g, lnx_b, w_up_a, w_up_b, w_o, g_mlp, w_ff1, w_ff2, g_ple, w_ple_gate, w_ple, g_final):
    params = (g_mix, w_in, conv_w, a_log, dt_bias, gdn_norm_g, mu_shift, w0, w2, a0, a2, g2,
              k_k, k_a, r_k, lnx_g, lnx_b, w_up_a, w_up_b, w_o, g_mlp, w_ff1, w_ff2,
              g_ple, w_ple_gate, w_ple)
    depth = w_in.shape[0]
    assert depth == 1
    pw = _prepare(tuple(w[0] for w in params))
    gf = g_final.reshape(1, -1)
    bp = x_prompt.shape[0]
    z_sg = jnp.zeros((bp, H_A, DK_A, DV_A), F32)
    z_buf = jnp.zeros((bp, CONV_W - 1, CONV_CH), F32)
    z_sr = jnp.zeros((bp, H_B, N_B, N_B), F32)
    z_sh = jnp.zeros((bp, SHIFT_W), F32)
    y_p, sg_p, buf_p, sr_p, sh_p = _run_layer(x_prompt, p_prompt[0], z_sg, z_buf, z_sr, z_sh, pw, gf)
    y_s, sg_s, buf_s, sr_s, sh_s = _run_layer(x_sample, p_sample[0], state_gdn[0], cache_gdn_conv[0],
                                              state_rwkv[0], cache_rwkv_shift[0], pw, gf)
    st = lambda a: a[None]
    return (y_p, y_s, st(sg_p), st(buf_p), st(sr_p), st(sh_p), st(sg_s), st(buf_s), st(sr_s), st(sh_s))
```

```python
import functools

import jax
import jax.numpy as jnp
from jax import lax
from jax.experimental import pallas as pl
from jax.experimental.pallas import tpu as pltpu

F32 = jnp.float32
BF16 = jnp.bfloat16

D_MODEL = 2048
H_A, DK_A, DV_A = 8, 128, 128
W_A = H_A * DV_A
CONV_W = 4
CONV_CH = 3 * W_A
H_B, N_B = 16, 64
W_B = H_B * N_B
LORA_W, LORA_A, LORA_G = 96, 96, 256
SHIFT_W = 3 * W_B + LORA_W + LORA_A + LORA_G
D_FF = 4 * D_MODEL
D_PLE = 256
EPS = 1e-6
LNX_EPS = 64e-5
CHUNK = 64
GDN_HB = 8
RWKV_PG = 8
SEQS_PER_STEP = 4

LANES = 128
SUBLANES = 8
VMEM_LIMIT = 52 * 2 ** 20

XB_PAD = 3 * W_B + 2 * LANES + LORA_G
AB_BLOCK = 3 * W_B // LANES
ALPHA_LANE = LORA_W
BETA_LANE = LORA_W + H_A

P1_OFF, P2_OFF, P3_OFF = 0, 4 * W_A, 4 * W_A + 2 * D_MODEL
N_PROJ = P3_OFF + XB_PAD

NN = ((1,), (0,))
NT = ((1,), (1,))
TN = ((0,), (0,))


def _dg(a, b, dims):
    return lax.dot_general(a, b, (dims, ((), ())), preferred_element_type=F32)


def _dot1(a, b, dims=NN):
    return _dg(a.astype(BF16), b.astype(BF16), dims)


def _cumsum_rows(x, c):
    rows = x.shape[0]
    i = lax.broadcasted_iota(jnp.int32, (rows, rows), 0)
    j = lax.broadcasted_iota(jnp.int32, (rows, rows), 1)
    same_chunk = (i ^ j) < c
    tri = jnp.where((i >= j) & same_chunk, 1.0, 0.0).astype(BF16)
    h1 = x.astype(BF16)
    r1 = x - h1.astype(F32)
    h2 = r1.astype(BF16)
    h3 = (r1 - h2.astype(F32)).astype(BF16)
    return _dg(tri, h1, NN) + (_dg(tri, h2, NN) + _dg(tri, h3, NN))


def _tri_inv_many(a_list, c):
    i = lax.broadcasted_iota(jnp.int32, (c, c), 0)
    j = lax.broadcasted_iota(jnp.int32, (c, c), 1)
    ns = [-jnp.where((i ^ j) == 1, a, 0.0) for a in a_list]
    s, ls = 2, 1
    while s < c:
        mask = ((i ^ j) >> ls) == 1
        l_s = [jnp.where(mask, a, 0.0) for a in a_list]
        nl = [_dot1(n, l) for n, l in zip(ns, l_s)]
        ys = [l + x for l, x in zip(l_s, nl)]
        yn = [_dot1(y, n) for y, n in zip(ys, ns)]
        ns = [n - (y + z) for n, y, z in zip(ns, ys, yn)]
        s, ls = 2 * s, ls + 1
    eye = jnp.where(i == j, 1.0, 0.0)
    return [eye + n for n in ns]


def _pair_stack(x, c):
    n = x.shape[1]
    bi = lax.broadcasted_iota(jnp.int32, (2 * c, n), 0)
    bj = lax.broadcasted_iota(jnp.int32, (2 * c, n), 1)
    keep = (bi >= c) == (bj >= n // 2)
    return jnp.where(keep, jnp.concatenate([x, x], axis=0), 0.0)


def _tri_inv_packed(a_list, c):
    i = lax.broadcasted_iota(jnp.int32, (c, 2 * c), 0)
    jm = lax.broadcasted_iota(jnp.int32, (c, 2 * c), 1) & (c - 1)
    ns = [-jnp.where((i ^ jm) == 1, a, 0.0) for a in a_list]
    s, ls = 2, 1
    while s < c:
        mask = ((i ^ jm) >> ls) == 1
        l_s = [jnp.where(mask, a, 0.0) for a in a_list]
        nl = [_dot1(n, _pair_stack(l, c)) for n, l in zip(ns, l_s)]
        ys = [l + x for l, x in zip(l_s, nl)]
        yn = [_dot1(y, _pair_stack(n, c)) for y, n in zip(ys, ns)]
        ns = [n - (y + z) for n, y, z in zip(ns, ys, yn)]
        s, ls = 2 * s, ls + 1
    eye = jnp.where(i == jm, 1.0, 0.0)
    return [eye + n for n in ns]


def _softplus(x):
    return jnp.maximum(x, 0.0) + jnp.log1p(jnp.exp(-jnp.abs(x)))


def _sigmoid(x):
    return jax.nn.sigmoid(x)


def _rms(x, g):
    return x * lax.rsqrt(jnp.mean(x * x, axis=-1, keepdims=True) + EPS) * g


def _params(*sem):
    return pltpu.CompilerParams(dimension_semantics=sem, vmem_limit_bytes=VMEM_LIMIT)


def _norm_matmul_kernel(x_ref, g_ref, w_ref, o_ref, h_scr):
    @pl.when(pl.program_id(1) == 0)
    def _():
        h_scr[...] = _rms(x_ref[...], g_ref[...]).astype(BF16)

    o_ref[...] = _dg(h_scr[...], w_ref[...], NN).astype(o_ref.dtype)


def norm_matmul(x, g, w, tm, tn, out_dtype=F32):
    m, k = x.shape
    n = w.shape[1]
    return pl.pallas_call(
        _norm_matmul_kernel,
        grid=(m // tm, n // tn),
        in_specs=[pl.BlockSpec((tm, k), lambda i, j: (i, 0), pipeline_mode=pl.Buffered(1)),
                  pl.BlockSpec((1, k), lambda i, j: (0, 0)),
                  pl.BlockSpec((k, tn), lambda i, j: (0, j))],
        out_specs=pl.BlockSpec((tm, tn), lambda i, j: (i, j)),
        out_shape=jax.ShapeDtypeStruct((m, n), out_dtype),
        scratch_shapes=[pltpu.VMEM((tm, k), BF16)],
        compiler_params=_params("parallel", "arbitrary"),
        name="norm_matmul",
    )(x, g, w)


def _relayout_kernel(w_ref, o_ref, *, pieces):
    w = w_ref[...]
    parts = [jnp.zeros((w.shape[0], n), F32) if lo is None else w[:, lo:lo + n] for lo, n in pieces]
    o_ref[...] = jnp.concatenate(parts, axis=1).astype(o_ref.dtype)


def relayout_columns(w, pieces, tk):
    k, n = w.shape
    n_out = sum(width for _, width in pieces)
    return pl.pallas_call(
        functools.partial(_relayout_kernel, pieces=tuple(pieces)),
        grid=(k // tk,),
        in_specs=[pl.BlockSpec((tk, n), lambda i: (i, 0))],
        out_specs=pl.BlockSpec((tk, n_out), lambda i: (i, 0)),
        out_shape=jax.ShapeDtypeStruct((k, n_out), BF16),
        compiler_params=_params("parallel"),
        name="relayout_columns",
    )(w)


def _merge_kernel(x_ref, oa_ref, ob_ref, gma_ref, gmb_ref, wa_ref, wb_ref, wo_ref, g_ref, o_ref, h_ref, *, tn):
    oa = oa_ref[...]
    ob = ob_ref[...]
    acc = x_ref[...]
    for n in range(D_MODEL // tn):
        cs = slice(n * tn, (n + 1) * tn)
        merged = (_sigmoid(gma_ref[:, cs]) * _dg(oa, wa_ref[:, cs], NN)
                  + _sigmoid(gmb_ref[:, cs]) * _dg(ob, wb_ref[:, cs], NN))
        acc = acc + _dg(merged.astype(BF16), wo_ref[cs, :], NN)
    o_ref[...] = acc
    h_ref[...] = _rms(acc, g_ref[...]).astype(BF16)


def merge_out(x, o_a, o_b, p, gate_blk, w_up_a, w_up_b, w_o, g_mlp, tm, tn):
    m = x.shape[0]
    resident = lambda shape: pl.BlockSpec(shape, lambda i: (0, 0), pipeline_mode=pl.Buffered(1))
    return pl.pallas_call(
        functools.partial(_merge_kernel, tn=tn),
        grid=(m // tm,),
        in_specs=[pl.BlockSpec((tm, D_MODEL), lambda i: (i, 0)),
                  pl.BlockSpec((tm, W_A), lambda i: (i, 0)),
                  pl.BlockSpec((tm, W_B), lambda i: (i, 0)),
                  pl.BlockSpec((tm, D_MODEL), lambda i: (i, gate_blk)),
                  pl.BlockSpec((tm, D_MODEL), lambda i: (i, gate_blk + 1)),
                  resident((W_A, D_MODEL)), resident((W_B, D_MODEL)), resident((D_MODEL, D_MODEL)),
                  pl.BlockSpec((1, D_MODEL), lambda i: (0, 0))],
        out_specs=[pl.BlockSpec((tm, D_MODEL), lambda i: (i, 0)),
                   pl.BlockSpec((tm, D_MODEL), lambda i: (i, 0))],
        out_shape=[jax.ShapeDtypeStruct((m, D_MODEL), F32), jax.ShapeDtypeStruct((m, D_MODEL), BF16)],
        compiler_params=_params("parallel"),
        name="merge_out",
    )(x, o_a, o_b, p, p, w_up_a, w_up_b, w_o, g_mlp)


def _mlp_kernel(h_ref, w1_ref, w2_ref, o_ref, *, tc):
    f = pl.program_id(1)

    @pl.when(f == 0)
    def _():
        o_ref[...] = jnp.zeros_like(o_ref)

    h = h_ref[...]
    for fc in range(w1_ref.shape[1] // tc):
        fs = slice(fc * tc, (fc + 1) * tc)
        a = jnp.maximum(_dg(h, w1_ref[:, fs], NN), 0.0)
        a = (a * a).astype(BF16)
        for nc in range(D_MODEL // tc):
            ns = slice(nc * tc, (nc + 1) * tc)
            o_ref[:, ns] += _dg(a, w2_ref[fs, ns], NN)


def mlp(h, w1, w2, tm, tf, tc):
    m = h.shape[0]
    return pl.pallas_call(
        functools.partial(_mlp_kernel, tc=tc),
        grid=(m // tm, D_FF // tf),
        in_specs=[pl.BlockSpec((tm, D_MODEL), lambda i, f: (i, 0)),
                  pl.BlockSpec((D_MODEL, tf), lambda i, f: (0, f)),
                  pl.BlockSpec((tf, D_MODEL), lambda i, f: (f, 0))],
        out_specs=pl.BlockSpec((tm, D_MODEL), lambda i, f: (i, 0)),
        out_shape=jax.ShapeDtypeStruct((m, D_MODEL), F32),
        compiler_params=_params("parallel", "arbitrary"),
        name="mlp",
    )(h, w1, w2)


def _ple_final_kernel(x_ref, d_ref, p_ref, gp_ref, wg_ref, wp_ref, gf_ref, o_ref):
    x = x_ref[...] + d_ref[...]
    gate = _sigmoid(_dg(_rms(x, gp_ref[...]).astype(BF16), wg_ref[...], NN))
    x = x + gate * _dg(p_ref[...].astype(BF16), wp_ref[...], NN)
    o_ref[...] = _rms(x, gf_ref[...])


def ple_final(x, delta, ple, g_ple, w_gate, w_ple, g_final, tm):
    m = x.shape[0]
    resident = lambda shape: pl.BlockSpec(shape, lambda i: (0, 0), pipeline_mode=pl.Buffered(1))
    return pl.pallas_call(
        _ple_final_kernel,
        grid=(m // tm,),
        in_specs=[pl.BlockSpec((tm, D_MODEL), lambda i: (i, 0)),
                  pl.BlockSpec((tm, D_MODEL), lambda i: (i, 0)),
                  pl.BlockSpec((tm, D_PLE), lambda i: (i, 0)),
                  pl.BlockSpec((1, D_MODEL), lambda i: (0, 0)),
                  resident((D_MODEL, D_MODEL)), resident((D_PLE, D_MODEL)),
                  pl.BlockSpec((1, D_MODEL), lambda i: (0, 0))],
        out_specs=pl.BlockSpec((tm, D_MODEL), lambda i: (i, 0)),
        out_shape=jax.ShapeDtypeStruct((m, D_MODEL), F32),
        compiler_params=_params("parallel"),
        name="ple_final",
    )(x, delta, ple, g_ple, w_gate, w_ple, g_final)


def _gdn_kernel(q_ref, k_ref, v_ref, gate_ref, ab_ref, hq_ref, hk_ref, hv_ref,
                cq_ref, ck_ref, cv_ref, alog_ref, dt_ref, ng_ref, s0_ref,
                o_ref, sout_ref, qs, ks, vs, s_scr, *, c, hb, nseq):
    g_idx = pl.program_id(1)
    t_idx = pl.program_id(2)
    hist = SUBLANES
    seqs = range(nseq)

    @pl.when(t_idx == 0)
    def _():
        for s in seqs:
            for h in range(hb):
                s_scr[s * hb + h] = s0_ref[s, h]
            qs[s] = hq_ref[s]
            ks[s] = hk_ref[s]
            vs[s] = hv_ref[s]

    row8 = lax.broadcasted_iota(jnp.int32, (hist, hb * DK_A), 0)

    def conv_silu(scr, x_ref, w_ref):
        outs = []
        for s in seqs:
            u = x_ref[s]
            prev8 = scr[s]
            out = None
            for j in range(CONV_W - 1):
                sh = CONV_W - 1 - j
                rolled = pltpu.roll(u, sh, axis=0)
                top = jnp.where(row8 < sh, pltpu.roll(prev8, sh, axis=0), rolled[0:hist])
                term = jnp.concatenate([top, rolled[hist:]], axis=0) * w_ref[j:j + 1, :]
                out = term if out is None else out + term
            out = out + u * w_ref[CONV_W - 1:CONV_W, :]
            scr[s] = x_ref[s, c - hist:c, :]
            outs.append(out)
        out = jnp.concatenate(outs, axis=0) if nseq > 1 else outs[0]
        return out * _sigmoid(out)

    qc = conv_silu(qs, q_ref, cq_ref)
    kc = conv_silu(ks, k_ref, ck_ref)
    vc = conv_silu(vs, v_ref, cv_ref)

    ab = ab_ref[...].reshape(nseq * c, LANES)
    g_all = -jnp.exp(alog_ref[...]) * _softplus(ab + dt_ref[...])
    beta_all = _sigmoid(ab)
    gcum = _cumsum_rows(g_all, c)
    lane = lax.broadcasted_iota(jnp.int32, (c, LANES), 1)
    sub_t = lax.broadcasted_iota(jnp.int32, (LANES, c), 0)
    i = lax.broadcasted_iota(jnp.int32, (c, c), 0)
    j = lax.broadcasted_iota(jnp.int32, (c, c), 1)
    incl = i >= j
    strict = i > j
    last_row = lax.broadcasted_iota(jnp.int32, (c, 1), 0) == c - 1
    gate = gate_ref[...].reshape(nseq * c, hb * DV_A)
    ng = ng_ref[...]

    hs = range(hb)
    sls = [slice(h * DK_A, (h + 1) * DK_A) for h in hs]
    rs = [slice(s * c, (s + 1) * c) for s in seqs]
    heads = [g_idx * hb + h for h in hs]
    items = [(s, h) for s in seqs for h in hs]
    gcum_s = [gcum[r] for r in rs]
    gcum_t = [g.T for g in gcum_s]
    beta_s = [beta_all[r] for r in rs]
    qn = [qc[:, sl] for sl in sls]
    kn = [kc[:, sl] for sl in sls]
    qn = [x * lax.rsqrt(jnp.sum(x * x, axis=-1, keepdims=True) + EPS) * (DK_A ** -0.5) for x in qn]
    kn = [x * lax.rsqrt(jnp.sum(x * x, axis=-1, keepdims=True) + EPS) for x in kn]
    gc = {(s, h): jnp.sum(jnp.where(lane == ALPHA_LANE + heads[h], gcum_s[s], 0.0), axis=1, keepdims=True)
          for s, h in items}
    gr = {(s, h): jnp.sum(jnp.where(sub_t == ALPHA_LANE + heads[h], gcum_t[s], 0.0), axis=0, keepdims=True)
          for s, h in items}
    bc = {(s, h): jnp.sum(jnp.where(lane == BETA_LANE + heads[h], beta_s[s], 0.0), axis=1, keepdims=True)
          for s, h in items}
    qh = {(s, h): qn[h][rs[s]] for s, h in items}
    kh = {(s, h): kn[h][rs[s]] for s, h in items}
    vh = {(s, h): vc[rs[s], sls[h]] for s, h in items}
    dec = {n: jnp.exp(jnp.where(incl, gc[n] - gr[n], -jnp.inf)) for n in items}
    kb = {n: kh[n] * bc[n] for n in items}
    kk = {n: _dot1(kb[n], kh[n], NT) for n in items}
    t_inv = dict(zip(items, _tri_inv_many([jnp.where(strict, kk[n] * dec[n], 0.0) for n in items], c)))
    eg = {n: jnp.exp(gc[n]) for n in items}
    uw = {n: _dot1(t_inv[n], jnp.concatenate([vh[n] * bc[n], kb[n] * eg[n]], axis=1)) for n in items}
    qk = {n: jnp.where(incl, _dot1(qh[n], kh[n], NT) * dec[n], 0.0) for n in items}
    g_last = {n: jnp.sum(jnp.where(last_row, gc[n], 0.0), axis=0, keepdims=True) for n in items}
    kd = {n: kh[n] * jnp.exp(g_last[n] - gc[n]) for n in items}
    s_old = {(s, h): s_scr[s * hb + h] for s, h in items}
    ws_qs = {n: _dot1(jnp.concatenate([uw[n][:, DV_A:], qh[n] * eg[n]], axis=0), s_old[n]) for n in items}
    v_new = {n: uw[n][:, :DV_A] - ws_qs[n][:c] for n in items}
    o = {n: ws_qs[n][c:] + _dot1(qk[n], v_new[n]) for n in items}
    upd = {n: _dot1(kd[n], v_new[n], TN) for n in items}
    for s, h in items:
        s_scr[s * hb + h] = s_old[s, h] * jnp.exp(g_last[s, h]) + upd[s, h]
    for s, h in items:
        gh = gate[rs[s], sls[h]]
        o_ref[s, :, sls[h]] = (_rms(o[s, h], ng) * (gh * _sigmoid(gh))).astype(o_ref.dtype)

    @pl.when(t_idx == pl.num_programs(2) - 1)
    def _():
        for s in seqs:
            for h in range(hb):
                sout_ref[s, h] = s_scr[s * hb + h]


def gdn_branch(p, hist8, conv_w, alog_vec, dt_vec, norm_g, s0, b, t, c, hb, nseq):
    nc = t // c
    ng = H_A // hb
    gw = hb * DK_A
    seg = lambda k: pl.BlockSpec((nseq, c, gw), lambda bi, g, ti: (bi, ti, P1_OFF // gw + k * ng + g))
    hist = lambda k: pl.BlockSpec((nseq, SUBLANES, gw), lambda bi, g, ti: (bi, 0, k * ng + g))
    cw = lambda k: pl.BlockSpec((CONV_W, gw), lambda bi, g, ti: (0, k * ng + g))
    vec = pl.BlockSpec((1, LANES), lambda bi, g, ti: (0, 0))
    state = pl.BlockSpec((nseq, hb, DK_A, DV_A), lambda bi, g, ti: (bi, g, 0, 0))
    return pl.pallas_call(
        functools.partial(_gdn_kernel, c=c, hb=hb, nseq=nseq),
        grid=(b // nseq, ng, nc),
        in_specs=[seg(0), seg(1), seg(2), seg(3),
                  pl.BlockSpec((nseq, c, LANES), lambda bi, g, ti: (bi, ti, P3_OFF // LANES + AB_BLOCK)),
                  hist(0), hist(1), hist(2), cw(0), cw(1), cw(2), vec, vec, vec, state],
        out_specs=[pl.BlockSpec((nseq, c, gw), lambda bi, g, ti: (bi, ti, g)), state],
        out_shape=[jax.ShapeDtypeStruct((b, t, W_A), BF16),
                   jax.ShapeDtypeStruct((b, H_A, DK_A, DV_A), F32)],
        scratch_shapes=[pltpu.VMEM((nseq, SUBLANES, gw), F32)] * 3 + [pltpu.VMEM((nseq * hb, DK_A, DV_A), F32)],
        compiler_params=_params("parallel", "parallel", "arbitrary"),
        name="gdn_branch",
    )(p, p, p, p, p, hist8, hist8, hist8, conv_w, conv_w, conv_w, alog_vec, dt_vec, norm_g, s0)


def _rwkv_kernel(r_ref, k_ref, v_ref, tail_ref, pr_ref, pk_ref, pv_ref, pt_ref,
                 mr_ref, mk_ref, mv_ref, mt_ref, w0_ref, a0_ref, kk_ref, ka_ref, rk_ref,
                 lg_ref, lb_ref, w2_ref, a2_ref, g2_ref, s0_ref,
                 o_ref, sout_ref, prev_r, prev_k, prev_v, prev_t, s_scr, *, c, pg, nseq):
    t_idx = pl.program_id(2)
    seqs = range(nseq)
    rows = nseq * c

    @pl.when(t_idx == 0)
    def _():
        prev_r[...] = pr_ref[...]
        prev_k[...] = pk_ref[...]
        prev_v[...] = pv_ref[...]
        prev_t[...] = pt_ref[...]
        z = jnp.zeros((N_B, N_B), F32)
        for s in seqs:
            for p in range(pg):
                s_a = s0_ref[s, 2 * p]
                s_b = s0_ref[s, 2 * p + 1]
                s_scr[s * pg + p] = jnp.concatenate([jnp.concatenate([s_a, z], axis=1),
                                                     jnp.concatenate([z, s_b], axis=1)], axis=0)

    def mix(x_ref, prev, mu_ref):
        outs = []
        for s in seqs:
            x = x_ref[s]
            row = lax.broadcasted_iota(jnp.int32, x.shape, 0)
            shifted = jnp.where(row == 0, prev[s], pltpu.roll(x, 1, axis=0))
            prev[s] = x_ref[s, c - 1:c, :]
            outs.append(x + (shifted - x) * mu_ref[...])
        return jnp.concatenate(outs, axis=0) if nseq > 1 else outs[0]

    r = mix(r_ref, prev_r, mr_ref)
    k = mix(k_ref, prev_k, mk_ref)
    v = mix(v_ref, prev_v, mv_ref)
    tail = mix(tail_ref, prev_t, mt_ref)
    xw = tail[:, 0:LANES]
    xa = tail[:, LANES:2 * LANES]
    xg = tail[:, 2 * LANES:]

    w_log = -_softplus(-(w0_ref[...] + _dot1(jnp.tanh(xw), w2_ref[...]))) - 0.5
    lw = -jnp.exp(w_log)
    a = _sigmoid(a0_ref[...] + _dot1(xa, a2_ref[...]))
    gate = _dot1(_sigmoid(xg), g2_ref[...])

    gw = pg * LANES
    fp = (lax.broadcasted_iota(jnp.int32, (rows, LANES), 1) & N_B) == 0

    def seg_sum(x):
        outs = []
        for p in range(pg):
            xp = x[:, p * LANES:(p + 1) * LANES]
            s_a = jnp.sum(jnp.where(fp, xp, 0.0), axis=-1, keepdims=True)
            s_b = jnp.sum(jnp.where(fp, 0.0, xp), axis=-1, keepdims=True)
            outs.append(jnp.where(fp, s_a, s_b))
        return jnp.concatenate(outs, axis=1) if pg > 1 else outs[0]

    kkx = k * kk_ref[...]
    kk = kkx * lax.rsqrt(seg_sum(kkx * kkx) + EPS)
    kh = k * (1.0 + (a - 1.0) * ka_ref[...])
    bb = kk * a
    cw = _cumsum_rows(lw, c)
    rt = r * jnp.exp(cw)
    e_neg = jnp.exp(-cw)
    kt = kh * e_neg
    bt = bb * e_neg
    at = kk * jnp.exp(cw - lw)
    row_c = lax.broadcasted_iota(jnp.int32, (c, gw), 0)
    rs = [slice(s * c, (s + 1) * c) for s in seqs]
    cw_s = [cw[rc] for rc in rs]
    cw_last = [jnp.sum(jnp.where(row_c == c - 1, x, 0.0), axis=0, keepdims=True) for x in cw_s]
    e_last = [jnp.exp(cw_last[s] - cw_s[s]) for s in seqs]
    k_hat = [kh[rs[s]] * e_last[s] for s in seqs]
    b_hat = [bb[rs[s]] * e_last[s] for s in seqs]
    e_end = [jnp.exp(x) for x in cw_last]

    i = lax.broadcasted_iota(jnp.int32, (c, 2 * c), 0)
    jm = lax.broadcasted_iota(jnp.int32, (c, 2 * c), 1) & (c - 1)
    incl = i >= jm
    strict = i > jm
    bi = lax.broadcasted_iota(jnp.int32, (LANES, LANES), 0)
    bj = lax.broadcasted_iota(jnp.int32, (LANES, LANES), 1)
    same_head = ((bi ^ bj) & N_B) == 0

    ps = range(pg)
    sls = [slice(p * LANES, (p + 1) * LANES) for p in ps]
    items = [(s, p) for s in seqs for p in ps]
    at_p = {(s, p): at[rs[s], sls[p]] for s, p in items}
    rt_p = {(s, p): rt[rs[s], sls[p]] for s, p in items}
    v_p = {(s, p): v[rs[s], sls[p]] for s, p in items}
    s_old = {(s, p): s_scr[s * pg + p] for s, p in items}
    a_s = {n: _dot1(at_p[n], s_old[n], NT) for n in items}
    r_s = {n: _dot1(rt_p[n], s_old[n], NT) for n in items}
    lhs = {n: jnp.concatenate([at_p[n], rt_p[n]], axis=0) for n in items}
    xb_ = {(s, p): _dot1(lhs[s, p], _pair_stack(bt[rs[s], sls[p]], c), NT) for s, p in items}
    xk_ = {(s, p): _dot1(lhs[s, p], _pair_stack(kt[rs[s], sls[p]], c), NT) for s, p in items}
    l_ak = {n: jnp.where(strict, xk_[n][:c], 0.0) for n in items}
    r_b = {n: jnp.where(incl, xb_[n][c:], 0.0) for n in items}
    r_k = {n: jnp.where(incl, xk_[n][c:], 0.0) for n in items}
    t_inv = dict(zip(items, _tri_inv_packed([jnp.where(strict, xb_[n][:c], 0.0) for n in items], c)))
    v_st = {n: _pair_stack(v_p[n], c) for n in items}
    rhs = {n: a_s[n] + _dot1(l_ak[n], v_st[n]) for n in items}
    p_all = {n: _dot1(t_inv[n], _pair_stack(rhs[n], c)) for n in items}
    ys = {n: r_s[n] + _dot1(r_k[n], v_st[n]) - _dot1(r_b[n], _pair_stack(p_all[n], c)) for n in items}
    upd = {(s, p): _dot1(jnp.concatenate([v_p[s, p], p_all[s, p]], axis=0),
                         jnp.concatenate([k_hat[s][:, sls[p]], -b_hat[s][:, sls[p]]], axis=0), TN)
           for s, p in items}
    for s, p in items:
        s_scr[s * pg + p] = s_old[s, p] * e_end[s][:, sls[p]] + jnp.where(same_head, upd[s, p], 0.0)

    y = jnp.concatenate([jnp.concatenate([ys[s, p] for p in ps], axis=1) if pg > 1 else ys[s, 0]
                         for s in seqs], axis=0)
    inv_n = 1.0 / N_B
    yc = y - seg_sum(y) * inv_n
    yn = yc * lax.rsqrt(seg_sum(yc * yc) * inv_n + LNX_EPS)
    yn = yn * lg_ref[...] + lb_ref[...]
    bonus = seg_sum(r * kh * rk_ref[...]) * v
    o_ref[...] = ((yn + bonus) * gate).astype(o_ref.dtype).reshape(nseq, c, gw)

    @pl.when(t_idx == pl.num_programs(2) - 1)
    def _():
        for s in seqs:
            for p in range(pg):
                st = s_scr[s * pg + p]
                sout_ref[s, 2 * p] = st[:N_B, :N_B]
                sout_ref[s, 2 * p + 1] = st[N_B:, N_B:]


def rwkv_branch(p, prev, mu, w0, a0, k_k, k_a, r_k, lnx_g, lnx_b, w2p, a2p, g2, s0, b, t, c, pg, nseq):
    nc = t // c
    gw = pg * LANES
    ng = W_B // gw
    tw = XB_PAD - 3 * W_B
    tail_blk = 3 * W_B // tw
    seg = lambda k: pl.BlockSpec((nseq, c, gw), lambda bi, g, ti: (bi, ti, P3_OFF // gw + k * ng + g))
    pseg = lambda k: pl.BlockSpec((nseq, 1, gw), lambda bi, g, ti: (bi, 0, k * ng + g))
    mseg = lambda k: pl.BlockSpec((1, gw), lambda bi, g, ti: (0, k * ng + g))
    chan = pl.BlockSpec((1, gw), lambda bi, g, ti: (0, g))
    state = pl.BlockSpec((nseq, 2 * pg, N_B, N_B), lambda bi, g, ti: (bi, g, 0, 0))
    return pl.pallas_call(
        functools.partial(_rwkv_kernel, c=c, pg=pg, nseq=nseq),
        grid=(b // nseq, ng, nc),
        in_specs=[seg(0), seg(1), seg(2),
                  pl.BlockSpec((nseq, c, tw), lambda bi, g, ti: (bi, ti, P3_OFF // tw + tail_blk)),
                  pseg(0), pseg(1), pseg(2),
                  pl.BlockSpec((nseq, 1, tw), lambda bi, g, ti: (bi, 0, tail_blk)),
                  mseg(0), mseg(1), mseg(2),
                  pl.BlockSpec((1, tw), lambda bi, g, ti: (0, tail_blk)),
                  chan, chan, chan, chan, chan, chan, chan,
                  pl.BlockSpec((LANES, gw), lambda bi, g, ti: (0, g)),
                  pl.BlockSpec((LANES, gw), lambda bi, g, ti: (0, g)),
                  pl.BlockSpec((LORA_G, gw), lambda bi, g, ti: (0, g)),
                  state],
        out_specs=[pl.BlockSpec((nseq, c, gw), lambda bi, g, ti: (bi, ti, g)), state],
        out_shape=[jax.ShapeDtypeStruct((b, t, W_B), BF16),
                   jax.ShapeDtypeStruct((b, H_B, N_B, N_B), F32)],
        scratch_shapes=[pltpu.VMEM((nseq, 1, gw), F32)] * 3 + [pltpu.VMEM((nseq, 1, tw), F32),
                                                               pltpu.VMEM((nseq * pg, LANES, LANES), F32)],
        compiler_params=_params("parallel", "parallel", "arbitrary"),
        name="rwkv_branch",
    )(p, p, p, p, prev, prev, prev, prev, mu, mu, mu, mu,
      w0, a0, k_k, k_a, r_k, lnx_g, lnx_b, w2p, a2p, g2, s0)


def _pad_xb(a, ab=None):
    lead = a.shape[:-1]
    z = lambda n: jnp.zeros(lead + (n,), a.dtype)
    o_w, o_a, o_g = 3 * W_B, 3 * W_B + LORA_W, 3 * W_B + LORA_W + LORA_A
    mid = z(2 * H_A) if ab is None else ab
    return jnp.concatenate([a[..., :o_a], mid, z(LANES - LORA_W - 2 * H_A),
                            a[..., o_a:o_g], z(LANES - LORA_A), a[..., o_g:]], axis=-1)


def _unpad_xb(a):
    o_a = 3 * W_B + LORA_W
    return jnp.concatenate([a[..., :o_a], a[..., 3 * W_B + LANES:3 * W_B + LANES + LORA_A],
                            a[..., 3 * W_B + 2 * LANES:]], axis=-1)


def _pad_rows(w, n):
    return jnp.concatenate([w, jnp.zeros((n - w.shape[0],) + w.shape[1:], w.dtype)], axis=0)


def _lane_vec(v, start):
    return jnp.zeros((1, LANES), F32).at[0, start:start + v.shape[0]].set(v)


def _prepare(lw):
    (g_mix, w_in, conv_w, a_log, dt_bias, gdn_norm_g, mu_shift, w0, w2, a0, a2, g2,
     k_k, k_a, r_k, lnx_g, lnx_b, w_up_a, w_up_b, w_o, g_mlp, w_ff1, w_ff2,
     g_ple, w_ple_gate, w_ple) = lw
    o_alpha = CONV_CH
    o_gate_a = CONV_CH + 2 * H_A
    o_xb = o_gate_a + W_A
    o_gm = o_xb + SHIFT_W
    row = lambda v: v.reshape(1, -1)
    return dict(
        g_mix=row(g_mix),
        w_proj=relayout_columns(w_in, [
            (0, CONV_CH), (o_gate_a, W_A), (o_gm, 2 * D_MODEL),
            (o_xb, 3 * W_B + LORA_W), (o_alpha, 2 * H_A), (None, LANES - LORA_W - 2 * H_A),
            (o_xb + 3 * W_B + LORA_W, LORA_A), (None, LANES - LORA_A),
            (o_xb + 3 * W_B + LORA_W + LORA_A, LORA_G)], 64),
        conv_w=conv_w,
        alog_vec=_lane_vec(a_log, ALPHA_LANE), dt_vec=_lane_vec(dt_bias, ALPHA_LANE),
        gdn_norm_g=row(gdn_norm_g),
        mu=row(_pad_xb(mu_shift)),
        w0=row(w0), a0=row(a0), k_k=row(k_k), k_a=row(k_a), r_k=row(r_k.reshape(-1)),
        lnx_g=row(lnx_g), lnx_b=row(lnx_b),
        w2p=_pad_rows(w2, LANES).astype(BF16), a2p=_pad_rows(a2, LANES).astype(BF16), g2=g2.astype(BF16),
        w_up_a=w_up_a.astype(BF16), w_up_b=w_up_b.astype(BF16), w_o=w_o.astype(BF16),
        g_mlp=row(g_mlp), w_ff1=w_ff1.astype(BF16), w_ff2=w_ff2.astype(BF16),
        g_ple=row(g_ple), w_ple_gate=w_ple_gate.astype(BF16), w_ple=w_ple.astype(BF16),
    )


def _run_layer(x, ple, s_gdn, buf_gdn, s_rwkv, shift_rwkv, pw, g_final):
    b, t, _ = x.shape
    m = b * t
    c = min(CHUNK, t)
    assert t % c == 0 and c % SUBLANES == 0 and t >= CONV_W - 1 and b % SEQS_PER_STEP == 0
    tm_proj = min(m, 2048)
    tm_stream = min(m, 1024)
    tm_res = min(m, 256)
    x2 = x.reshape(m, D_MODEL)

    p = norm_matmul(x2, pw["g_mix"], pw["w_proj"], tm_proj, 512)
    p3d = p.reshape(b, t, N_PROJ)

    hist8 = jnp.concatenate([jnp.zeros((b, SUBLANES - (CONV_W - 1), CONV_CH), F32), buf_gdn], axis=1)
    o_a, s_gdn_new = gdn_branch(p3d, hist8, pw["conv_w"], pw["alog_vec"], pw["dt_vec"],
                                pw["gdn_norm_g"], s_gdn, b, t, c, hb=GDN_HB, nseq=SEQS_PER_STEP)
    prev = _pad_xb(shift_rwkv).reshape(b, 1, XB_PAD)
    o_b, s_rwkv_new = rwkv_branch(p3d, prev, pw["mu"], pw["w0"], pw["a0"], pw["k_k"], pw["k_a"], pw["r_k"],
                                  pw["lnx_g"], pw["lnx_b"], pw["w2p"], pw["a2p"], pw["g2"], s_rwkv,
                                  b, t, c, pg=RWKV_PG, nseq=SEQS_PER_STEP)

    x1, h2 = merge_out(x2, o_a.reshape(m, W_A), o_b.reshape(m, W_B), p, P2_OFF // D_MODEL,
                       pw["w_up_a"], pw["w_up_b"], pw["w_o"], pw["g_mlp"], tm_res, 512)
    delta = mlp(h2, pw["w_ff1"], pw["w_ff2"], tm_stream, 1024, 512)
    y = ple_final(x1, delta, ple.reshape(m, D_PLE), pw["g_ple"], pw["w_ple_gate"], pw["w_ple"], g_final,
                  min(m, 512))

    new_buf = p3d[:, t - (CONV_W - 1):, P1_OFF:P1_OFF + CONV_CH]
    new_shift = _unpad_xb(p3d[:, t - 1, P3_OFF:])
    return y.reshape(b, t, D_MODEL), s_gdn_new, new_buf, s_rwkv_new, new_shift


def kernel(x_prompt, x_sample, p_prompt, p_sample, state_gdn, cache_gdn_conv, state_rwkv, cache_rwkv_shift, g_mix, w_in, conv_w, a_log, dt_bias, gdn_norm_g, mu_shift, w0, w2, a0, a2, g2, k_k, k_a, r_k, lnx_g, lnx_b, w_up_a, w_up_b, w_o, g_mlp, w_ff1, w_ff2, g_ple, w_ple_gate, w_ple, g_final):
    params = (g_mix, w_in, conv_w, a_log, dt_bias, gdn_norm_g, mu_shift, w0, w2, a0, a2, g2,
              k_k, k_a, r_k, lnx_g, lnx_b, w_up_a, w_up_b, w_o, g_mlp, w_ff1, w_ff2,
              g_ple, w_ple_gate, w_ple)
    depth = w_in.shape[0]
    assert depth == 1
    pw = _prepare(tuple(w[0] for w in params))
    gf = g_final.reshape(1, -1)
    bp = x_prompt.shape[0]
    z_sg = jnp.zeros((bp, H_A, DK_A, DV_A), F32)
    z_buf = jnp.zeros((bp, CONV_W - 1, CONV_CH), F32)
    z_sr = jnp.zeros((bp, H_B, N_B, N_B), F32)
    z_sh = jnp.zeros((bp, SHIFT_W), F32)
    y_p, sg_p, buf_p, sr_p, sh_p = _run_layer(x_prompt, p_prompt[0], z_sg, z_buf, z_sr, z_sh, pw, gf)
    y_s, sg_s, buf_s, sr_s, sh_s = _run_layer(x_sample, p_sample[0], state_gdn[0], cache_gdn_conv[0],
                                              state_rwkv[0], cache_rwkv_shift[0], pw, gf)
    st = lambda a: a[None]
    return (y_p, y_s, st(sg_p), st(buf_p), st(sr_p), st(sh_p), st(sg_s), st(buf_s), st(sr_s), st(sh_s))
```

```python
import functools

import jax
import jax.numpy as jnp
from jax import lax
from jax.experimental import pallas as pl
from jax.experimental.pallas import tpu as pltpu

F32 = jnp.float32
BF16 = jnp.bfloat16

D_MODEL = 2048
H_A, DK_A, DV_A = 8, 128, 128
W_A = H_A * DV_A
CONV_W = 4
CONV_CH = 3 * W_A
H_B, N_B = 16, 64
W_B = H_B * N_B
LORA_W, LORA_A, LORA_G = 96, 96, 256
SHIFT_W = 3 * W_B + LORA_W + LORA_A + LORA_G
D_FF = 4 * D_MODEL
D_PLE = 256
EPS = 1e-6
LNX_EPS = 64e-5
CHUNK = 64
GDN_HB = 8
RWKV_PG = 8
SEQS_PER_STEP = 4

LANES = 128
SUBLANES = 8
VMEM_LIMIT = 52 * 2 ** 20

XB_PAD = 3 * W_B + 2 * LANES + LORA_G
AB_BLOCK = 3 * W_B // LANES
ALPHA_LANE = LORA_W
BETA_LANE = LORA_W + H_A

P1_OFF, P2_OFF, P3_OFF = 0, 4 * W_A, 4 * W_A + 2 * D_MODEL
N_PROJ = P3_OFF + XB_PAD

NN = ((1,), (0,))
NT = ((1,), (1,))
TN = ((0,), (0,))


def _dg(a, b, dims):
    return lax.dot_general(a, b, (dims, ((), ())), preferred_element_type=F32)


def _dot1(a, b, dims=NN):
    return _dg(a.astype(BF16), b.astype(BF16), dims)


def _cumsum_rows(x, c):
    rows = x.shape[0]
    i = lax.broadcasted_iota(jnp.int32, (rows, rows), 0)
    j = lax.broadcasted_iota(jnp.int32, (rows, rows), 1)
    same_chunk = (i ^ j) < c
    tri = jnp.where((i >= j) & same_chunk, 1.0, 0.0).astype(BF16)
    h1 = x.astype(BF16)
    r1 = x - h1.astype(F32)
    h2 = r1.astype(BF16)
    h3 = (r1 - h2.astype(F32)).astype(BF16)
    return _dg(tri, h1, NN) + (_dg(tri, h2, NN) + _dg(tri, h3, NN))


def _tri_inv_many(a_list, c):
    i = lax.broadcasted_iota(jnp.int32, (c, c), 0)
    j = lax.broadcasted_iota(jnp.int32, (c, c), 1)
    ns = [-jnp.where((i ^ j) == 1, a, 0.0) for a in a_list]
    s, ls = 2, 1
    while s < c:
        mask = ((i ^ j) >> ls) == 1
        l_s = [jnp.where(mask, a, 0.0) for a in a_list]
        nl = [_dot1(n, l) for n, l in zip(ns, l_s)]
        ys = [l + x for l, x in zip(l_s, nl)]
        yn = [_dot1(y, n) for y, n in zip(ys, ns)]
        ns = [n - (y + z) for n, y, z in zip(ns, ys, yn)]
        s, ls = 2 * s, ls + 1
    eye = jnp.where(i == j, 1.0, 0.0)
    return [eye + n for n in ns]


def _pair_stack(x, c):
    n = x.shape[1]
    bi = lax.broadcasted_iota(jnp.int32, (2 * c, n), 0)
    bj = lax.broadcasted_iota(jnp.int32, (2 * c, n), 1)
    keep = (bi >= c) == (bj >= n // 2)
    return jnp.where(keep, jnp.concatenate([x, x], axis=0), 0.0)


def _tri_inv_packed(a_list, c):
    i = lax.broadcasted_iota(jnp.int32, (c, 2 * c), 0)
    jm = lax.broadcasted_iota(jnp.int32, (c, 2 * c), 1) & (c - 1)
    ns = [-jnp.where((i ^ jm) == 1, a, 0.0) for a in a_list]
    s, ls = 2, 1
    while s < c:
        mask = ((i ^ jm) >> ls) == 1
        l_s = [jnp.where(mask, a, 0.0) for a in a_list]
        nl = [_dot1(n, _pair_stack(l, c)) for n, l in zip(ns, l_s)]
        ys = [l + x for l, x in zip(l_s, nl)]
        yn = [_dot1(y, _pair_stack(n, c)) for y, n in zip(ys, ns)]
        ns = [n - (y + z) for n, y, z in zip(ns, ys, yn)]
        s, ls = 2 * s, ls + 1
    eye = jnp.where(i == jm, 1.0, 0.0)
    return [eye + n for n in ns]


def _softplus(x):
    return jnp.maximum(x, 0.0) + jnp.log1p(jnp.exp(-jnp.abs(x)))


def _sigmoid(x):
    return jax.nn.sigmoid(x)


def _rms(x, g):
    return x * lax.rsqrt(jnp.mean(x * x, axis=-1, keepdims=True) + EPS) * g


def _params(*sem):
    return pltpu.CompilerParams(dimension_semantics=sem, vmem_limit_bytes=VMEM_LIMIT)


def _norm_matmul_kernel(x_ref, g_ref, w_ref, o_ref, h_scr):
    @pl.when(pl.program_id(1) == 0)
    def _():
        h_scr[...] = _rms(x_ref[...], g_ref[...]).astype(BF16)

    o_ref[...] = _dg(h_scr[...], w_ref[...], NT).astype(o_ref.dtype)


def norm_matmul(x, g, wt, tm, tn, out_dtype=F32):
    m, k = x.shape
    n = wt.shape[0]
    return pl.pallas_call(
        _norm_matmul_kernel,
        grid=(m // tm, n // tn),
        in_specs=[pl.BlockSpec((tm, k), lambda i, j: (i, 0), pipeline_mode=pl.Buffered(1)),
                  pl.BlockSpec((1, k), lambda i, j: (0, 0)),
                  pl.BlockSpec((tn, k), lambda i, j: (j, 0))],
        out_specs=pl.BlockSpec((tm, tn), lambda i, j: (i, j)),
        out_shape=jax.ShapeDtypeStruct((m, n), out_dtype),
        scratch_shapes=[pltpu.VMEM((tm, k), BF16)],
        compiler_params=_params("parallel", "arbitrary"),
        name="norm_matmul",
    )(x, g, wt)


def _merge_kernel(x_ref, oa_ref, ob_ref, gma_ref, gmb_ref, wa_ref, wb_ref, wo_ref, g_ref, o_ref, h_ref, *, tn):
    oa = oa_ref[...]
    ob = ob_ref[...]
    acc = x_ref[...]
    for n in range(D_MODEL // tn):
        cs = slice(n * tn, (n + 1) * tn)
        merged = (_sigmoid(gma_ref[:, cs]) * _dg(oa, wa_ref[:, cs], NN)
                  + _sigmoid(gmb_ref[:, cs]) * _dg(ob, wb_ref[:, cs], NN))
        acc = acc + _dg(merged.astype(BF16), wo_ref[cs, :], NN)
    o_ref[...] = acc
    h_ref[...] = _rms(acc, g_ref[...]).astype(BF16)


def merge_out(x, o_a, o_b, p, gate_blk, w_up_a, w_up_b, w_o, g_mlp, tm, tn):
    m = x.shape[0]
    resident = lambda shape: pl.BlockSpec(shape, lambda i: (0, 0), pipeline_mode=pl.Buffered(1))
    return pl.pallas_call(
        functools.partial(_merge_kernel, tn=tn),
        grid=(m // tm,),
        in_specs=[pl.BlockSpec((tm, D_MODEL), lambda i: (i, 0)),
                  pl.BlockSpec((tm, W_A), lambda i: (i, 0)),
                  pl.BlockSpec((tm, W_B), lambda i: (i, 0)),
                  pl.BlockSpec((tm, D_MODEL), lambda i: (i, gate_blk)),
                  pl.BlockSpec((tm, D_MODEL), lambda i: (i, gate_blk + 1)),
                  resident((W_A, D_MODEL)), resident((W_B, D_MODEL)), resident((D_MODEL, D_MODEL)),
                  pl.BlockSpec((1, D_MODEL), lambda i: (0, 0))],
        out_specs=[pl.BlockSpec((tm, D_MODEL), lambda i: (i, 0)),
                   pl.BlockSpec((tm, D_MODEL), lambda i: (i, 0))],
        out_shape=[jax.ShapeDtypeStruct((m, D_MODEL), F32), jax.ShapeDtypeStruct((m, D_MODEL), BF16)],
        compiler_params=_params("parallel"),
        name="merge_out",
    )(x, o_a, o_b, p, p, w_up_a, w_up_b, w_o, g_mlp)


def _mlp_kernel(h_ref, w1_ref, w2_ref, o_ref, *, tc):
    f = pl.program_id(1)

    @pl.when(f == 0)
    def _():
        o_ref[...] = jnp.zeros_like(o_ref)

    h = h_ref[...]
    for fc in range(w1_ref.shape[1] // tc):
        fs = slice(fc * tc, (fc + 1) * tc)
        a = jnp.maximum(_dg(h, w1_ref[:, fs], NN), 0.0)
        a = (a * a).astype(BF16)
        for nc in range(D_MODEL // tc):
            ns = slice(nc * tc, (nc + 1) * tc)
            o_ref[:, ns] += _dg(a, w2_ref[fs, ns], NN)


def mlp(h, w1, w2, tm, tf, tc):
    m = h.shape[0]
    return pl.pallas_call(
        functools.partial(_mlp_kernel, tc=tc),
        grid=(m // tm, D_FF // tf),
        in_specs=[pl.BlockSpec((tm, D_MODEL), lambda i, f: (i, 0)),
                  pl.BlockSpec((D_MODEL, tf), lambda i, f: (0, f)),
                  pl.BlockSpec((tf, D_MODEL), lambda i, f: (f, 0))],
        out_specs=pl.BlockSpec((tm, D_MODEL), lambda i, f: (i, 0)),
        out_shape=jax.ShapeDtypeStruct((m, D_MODEL), F32),
        compiler_params=_params("parallel", "arbitrary"),
        name="mlp",
    )(h, w1, w2)


def _ple_final_kernel(x_ref, d_ref, p_ref, gp_ref, wg_ref, wp_ref, gf_ref, o_ref):
    x = x_ref[...] + d_ref[...]
    gate = _sigmoid(_dg(_rms(x, gp_ref[...]).astype(BF16), wg_ref[...], NN))
    x = x + gate * _dg(p_ref[...].astype(BF16), wp_ref[...], NN)
    o_ref[...] = _rms(x, gf_ref[...])


def ple_final(x, delta, ple, g_ple, w_gate, w_ple, g_final, tm):
    m = x.shape[0]
    resident = lambda shape: pl.BlockSpec(shape, lambda i: (0, 0), pipeline_mode=pl.Buffered(1))
    return pl.pallas_call(
        _ple_final_kernel,
        grid=(m // tm,),
        in_specs=[pl.BlockSpec((tm, D_MODEL), lambda i: (i, 0)),
                  pl.BlockSpec((tm, D_MODEL), lambda i: (i, 0)),
                  pl.BlockSpec((tm, D_PLE), lambda i: (i, 0)),
                  pl.BlockSpec((1, D_MODEL), lambda i: (0, 0)),
                  resident((D_MODEL, D_MODEL)), resident((D_PLE, D_MODEL)),
                  pl.BlockSpec((1, D_MODEL), lambda i: (0, 0))],
        out_specs=pl.BlockSpec((tm, D_MODEL), lambda i: (i, 0)),
        out_shape=jax.ShapeDtypeStruct((m, D_MODEL), F32),
        compiler_params=_params("parallel"),
        name="ple_final",
    )(x, delta, ple, g_ple, w_gate, w_ple, g_final)


def _gdn_kernel(q_ref, k_ref, v_ref, gate_ref, ab_ref, hq_ref, hk_ref, hv_ref,
                cq_ref, ck_ref, cv_ref, alog_ref, dt_ref, ng_ref, s0_ref,
                o_ref, sout_ref, qs, ks, vs, s_scr, *, c, hb, nseq):
    g_idx = pl.program_id(1)
    t_idx = pl.program_id(2)
    hist = SUBLANES
    seqs = range(nseq)

    @pl.when(t_idx == 0)
    def _():
        for s in seqs:
            for h in range(hb):
                s_scr[s * hb + h] = s0_ref[s, h]
            qs[s] = hq_ref[s]
            ks[s] = hk_ref[s]
            vs[s] = hv_ref[s]

    row8 = lax.broadcasted_iota(jnp.int32, (hist, hb * DK_A), 0)

    def conv_silu(scr, x_ref, w_ref):
        outs = []
        for s in seqs:
            u = x_ref[s]
            prev8 = scr[s]
            out = None
            for j in range(CONV_W - 1):
                sh = CONV_W - 1 - j
                rolled = pltpu.roll(u, sh, axis=0)
                top = jnp.where(row8 < sh, pltpu.roll(prev8, sh, axis=0), rolled[0:hist])
                term = jnp.concatenate([top, rolled[hist:]], axis=0) * w_ref[j:j + 1, :]
                out = term if out is None else out + term
            out = out + u * w_ref[CONV_W - 1:CONV_W, :]
            scr[s] = x_ref[s, c - hist:c, :]
            outs.append(out)
        out = jnp.concatenate(outs, axis=0) if nseq > 1 else outs[0]
        return out * _sigmoid(out)

    qc = conv_silu(qs, q_ref, cq_ref)
    kc = conv_silu(ks, k_ref, ck_ref)
    vc = conv_silu(vs, v_ref, cv_ref)

    ab = ab_ref[...].reshape(nseq * c, LANES)
    g_all = -jnp.exp(alog_ref[...]) * _softplus(ab + dt_ref[...])
    beta_all = _sigmoid(ab)
    gcum = _cumsum_rows(g_all, c)
    lane = lax.broadcasted_iota(jnp.int32, (c, LANES), 1)
    sub_t = lax.broadcasted_iota(jnp.int32, (LANES, c), 0)
    i = lax.broadcasted_iota(jnp.int32, (c, c), 0)
    j = lax.broadcasted_iota(jnp.int32, (c, c), 1)
    incl = i >= j
    strict = i > j
    last_row = lax.broadcasted_iota(jnp.int32, (c, 1), 0) == c - 1
    gate = gate_ref[...].reshape(nseq * c, hb * DV_A)
    ng = ng_ref[...]

    hs = range(hb)
    sls = [slice(h * DK_A, (h + 1) * DK_A) for h in hs]
    rs = [slice(s * c, (s + 1) * c) for s in seqs]
    heads = [g_idx * hb + h for h in hs]
    items = [(s, h) for s in seqs for h in hs]
    gcum_s = [gcum[r] for r in rs]
    gcum_t = [g.T for g in gcum_s]
    beta_s = [beta_all[r] for r in rs]
    qn = [qc[:, sl] for sl in sls]
    kn = [kc[:, sl] for sl in sls]
    qn = [x * lax.rsqrt(jnp.sum(x * x, axis=-1, keepdims=True) + EPS) * (DK_A ** -0.5) for x in qn]
    kn = [x * lax.rsqrt(jnp.sum(x * x, axis=-1, keepdims=True) + EPS) for x in kn]
    gc = {(s, h): jnp.sum(jnp.where(lane == ALPHA_LANE + heads[h], gcum_s[s], 0.0), axis=1, keepdims=True)
          for s, h in items}
    gr = {(s, h): jnp.sum(jnp.where(sub_t == ALPHA_LANE + heads[h], gcum_t[s], 0.0), axis=0, keepdims=True)
          for s, h in items}
    bc = {(s, h): jnp.sum(jnp.where(lane == BETA_LANE + heads[h], beta_s[s], 0.0), axis=1, keepdims=True)
          for s, h in items}
    qh = {(s, h): qn[h][rs[s]] for s, h in items}
    kh = {(s, h): kn[h][rs[s]] for s, h in items}
    vh = {(s, h): vc[rs[s], sls[h]] for s, h in items}
    dec = {n: jnp.exp(jnp.where(incl, gc[n] - gr[n], -jnp.inf)) for n in items}
    kb = {n: kh[n] * bc[n] for n in items}
    kk = {n: _dot1(kb[n], kh[n], NT) for n in items}
    t_inv = dict(zip(items, _tri_inv_many([jnp.where(strict, kk[n] * dec[n], 0.0) for n in items], c)))
    eg = {n: jnp.exp(gc[n]) for n in items}
    uw = {n: _dot1(t_inv[n], jnp.concatenate([vh[n] * bc[n], kb[n] * eg[n]], axis=1)) for n in items}
    qk = {n: jnp.where(incl, _dot1(qh[n], kh[n], NT) * dec[n], 0.0) for n in items}
    g_last = {n: jnp.sum(jnp.where(last_row, gc[n], 0.0), axis=0, keepdims=True) for n in items}
    kd = {n: kh[n] * jnp.exp(g_last[n] - gc[n]) for n in items}
    s_old = {(s, h): s_scr[s * hb + h] for s, h in items}
    ws_qs = {n: _dot1(jnp.concatenate([uw[n][:, DV_A:], qh[n] * eg[n]], axis=0), s_old[n]) for n in items}
    v_new = {n: uw[n][:, :DV_A] - ws_qs[n][:c] for n in items}
    o = {n: ws_qs[n][c:] + _dot1(qk[n], v_new[n]) for n in items}
    upd = {n: _dot1(kd[n], v_new[n], TN) for n in items}
    for s, h in items:
        s_scr[s * hb + h] = s_old[s, h] * jnp.exp(g_last[s, h]) + upd[s, h]
    for s, h in items:
        gh = gate[rs[s], sls[h]]
        o_ref[s, :, sls[h]] = (_rms(o[s, h], ng) * (gh * _sigmoid(gh))).astype(o_ref.dtype)

    @pl.when(t_idx == pl.num_programs(2) - 1)
    def _():
        for s in seqs:
            for h in range(hb):
                sout_ref[s, h] = s_scr[s * hb + h]


def gdn_branch(p, hist8, conv_w, alog_vec, dt_vec, norm_g, s0, b, t, c, hb, nseq):
    nc = t // c
    ng = H_A // hb
    gw = hb * DK_A
    seg = lambda k: pl.BlockSpec((nseq, c, gw), lambda bi, g, ti: (bi, ti, P1_OFF // gw + k * ng + g))
    hist = lambda k: pl.BlockSpec((nseq, SUBLANES, gw), lambda bi, g, ti: (bi, 0, k * ng + g))
    cw = lambda k: pl.BlockSpec((CONV_W, gw), lambda bi, g, ti: (0, k * ng + g))
    vec = pl.BlockSpec((1, LANES), lambda bi, g, ti: (0, 0))
    state = pl.BlockSpec((nseq, hb, DK_A, DV_A), lambda bi, g, ti: (bi, g, 0, 0))
    return pl.pallas_call(
        functools.partial(_gdn_kernel, c=c, hb=hb, nseq=nseq),
        grid=(b // nseq, ng, nc),
        in_specs=[seg(0), seg(1), seg(2), seg(3),
                  pl.BlockSpec((nseq, c, LANES), lambda bi, g, ti: (bi, ti, P3_OFF // LANES + AB_BLOCK)),
                  hist(0), hist(1), hist(2), cw(0), cw(1), cw(2), vec, vec, vec, state],
        out_specs=[pl.BlockSpec((nseq, c, gw), lambda bi, g, ti: (bi, ti, g)), state],
        out_shape=[jax.ShapeDtypeStruct((b, t, W_A), BF16),
                   jax.ShapeDtypeStruct((b, H_A, DK_A, DV_A), F32)],
        scratch_shapes=[pltpu.VMEM((nseq, SUBLANES, gw), F32)] * 3 + [pltpu.VMEM((nseq * hb, DK_A, DV_A), F32)],
        compiler_params=_params("parallel", "parallel", "arbitrary"),
        name="gdn_branch",
    )(p, p, p, p, p, hist8, hist8, hist8, conv_w, conv_w, conv_w, alog_vec, dt_vec, norm_g, s0)


def _rwkv_kernel(r_ref, k_ref, v_ref, tail_ref, pr_ref, pk_ref, pv_ref, pt_ref,
                 mr_ref, mk_ref, mv_ref, mt_ref, w0_ref, a0_ref, kk_ref, ka_ref, rk_ref,
                 lg_ref, lb_ref, w2_ref, a2_ref, g2_ref, s0_ref,
                 o_ref, sout_ref, prev_r, prev_k, prev_v, prev_t, s_scr, *, c, pg, nseq):
    t_idx = pl.program_id(2)
    seqs = range(nseq)
    rows = nseq * c

    @pl.when(t_idx == 0)
    def _():
        prev_r[...] = pr_ref[...]
        prev_k[...] = pk_ref[...]
        prev_v[...] = pv_ref[...]
        prev_t[...] = pt_ref[...]
        z = jnp.zeros((N_B, N_B), F32)
        for s in seqs:
            for p in range(pg):
                s_a = s0_ref[s, 2 * p]
                s_b = s0_ref[s, 2 * p + 1]
                s_scr[s * pg + p] = jnp.concatenate([jnp.concatenate([s_a, z], axis=1),
                                                     jnp.concatenate([z, s_b], axis=1)], axis=0)

    def mix(x_ref, prev, mu_ref):
        outs = []
        for s in seqs:
            x = x_ref[s]
            row = lax.broadcasted_iota(jnp.int32, x.shape, 0)
            shifted = jnp.where(row == 0, prev[s], pltpu.roll(x, 1, axis=0))
            prev[s] = x_ref[s, c - 1:c, :]
            outs.append(x + (shifted - x) * mu_ref[...])
        return jnp.concatenate(outs, axis=0) if nseq > 1 else outs[0]

    r = mix(r_ref, prev_r, mr_ref)
    k = mix(k_ref, prev_k, mk_ref)
    v = mix(v_ref, prev_v, mv_ref)
    tail = mix(tail_ref, prev_t, mt_ref)
    xw = tail[:, 0:LANES]
    xa = tail[:, LANES:2 * LANES]
    xg = tail[:, 2 * LANES:]

    w_log = -_softplus(-(w0_ref[...] + _dot1(jnp.tanh(xw), w2_ref[...]))) - 0.5
    lw = -jnp.exp(w_log)
    a = _sigmoid(a0_ref[...] + _dot1(xa, a2_ref[...]))
    gate = _dot1(_sigmoid(xg), g2_ref[...])

    gw = pg * LANES
    fp = (lax.broadcasted_iota(jnp.int32, (rows, LANES), 1) & N_B) == 0

    def seg_sum(x):
        outs = []
        for p in range(pg):
            xp = x[:, p * LANES:(p + 1) * LANES]
            s_a = jnp.sum(jnp.where(fp, xp, 0.0), axis=-1, keepdims=True)
            s_b = jnp.sum(jnp.where(fp, 0.0, xp), axis=-1, keepdims=True)
            outs.append(jnp.where(fp, s_a, s_b))
        return jnp.concatenate(outs, axis=1) if pg > 1 else outs[0]

    kkx = k * kk_ref[...]
    kk = kkx * lax.rsqrt(seg_sum(kkx * kkx) + EPS)
    kh = k * (1.0 + (a - 1.0) * ka_ref[...])
    bb = kk * a
    cw = _cumsum_rows(lw, c)
    rt = r * jnp.exp(cw)
    e_neg = jnp.exp(-cw)
    kt = kh * e_neg
    bt = bb * e_neg
    at = kk * jnp.exp(cw - lw)
    row_c = lax.broadcasted_iota(jnp.int32, (c, gw), 0)
    rs = [slice(s * c, (s + 1) * c) for s in seqs]
    cw_s = [cw[rc] for rc in rs]
    cw_last = [jnp.sum(jnp.where(row_c == c - 1, x, 0.0), axis=0, keepdims=True) for x in cw_s]
    e_last = [jnp.exp(cw_last[s] - cw_s[s]) for s in seqs]
    k_hat = [kh[rs[s]] * e_last[s] for s in seqs]
    b_hat = [bb[rs[s]] * e_last[s] for s in seqs]
    e_end = [jnp.exp(x) for x in cw_last]

    i = lax.broadcasted_iota(jnp.int32, (c, 2 * c), 0)
    jm = lax.broadcasted_iota(jnp.int32, (c, 2 * c), 1) & (c - 1)
    incl = i >= jm
    strict = i > jm
    bi = lax.broadcasted_iota(jnp.int32, (LANES, LANES), 0)
    bj = lax.broadcasted_iota(jnp.int32, (LANES, LANES), 1)
    same_head = ((bi ^ bj) & N_B) == 0

    ps = range(pg)
    sls = [slice(p * LANES, (p + 1) * LANES) for p in ps]
    items = [(s, p) for s in seqs for p in ps]
    at_p = {(s, p): at[rs[s], sls[p]] for s, p in items}
    rt_p = {(s, p): rt[rs[s], sls[p]] for s, p in items}
    v_p = {(s, p): v[rs[s], sls[p]] for s, p in items}
    s_old = {(s, p): s_scr[s * pg + p] for s, p in items}
    a_s = {n: _dot1(at_p[n], s_old[n], NT) for n in items}
    r_s = {n: _dot1(rt_p[n], s_old[n], NT) for n in items}
    lhs = {n: jnp.concatenate([at_p[n], rt_p[n]], axis=0) for n in items}
    xb_ = {(s, p): _dot1(lhs[s, p], _pair_stack(bt[rs[s], sls[p]], c), NT) for s, p in items}
    xk_ = {(s, p): _dot1(lhs[s, p], _pair_stack(kt[rs[s], sls[p]], c), NT) for s, p in items}
    l_ak = {n: jnp.where(strict, xk_[n][:c], 0.0) for n in items}
    r_b = {n: jnp.where(incl, xb_[n][c:], 0.0) for n in items}
    r_k = {n: jnp.where(incl, xk_[n][c:], 0.0) for n in items}
    t_inv = dict(zip(items, _tri_inv_packed([jnp.where(strict, xb_[n][:c], 0.0) for n in items], c)))
    v_st = {n: _pair_stack(v_p[n], c) for n in items}
    rhs = {n: a_s[n] + _dot1(l_ak[n], v_st[n]) for n in items}
    p_all = {n: _dot1(t_inv[n], _pair_stack(rhs[n], c)) for n in items}
    ys = {n: r_s[n] + _dot1(r_k[n], v_st[n]) - _dot1(r_b[n], _pair_stack(p_all[n], c)) for n in items}
    upd = {(s, p): _dot1(jnp.concatenate([v_p[s, p], p_all[s, p]], axis=0),
                         jnp.concatenate([k_hat[s][:, sls[p]], -b_hat[s][:, sls[p]]], axis=0), TN)
           for s, p in items}
    for s, p in items:
        s_scr[s * pg + p] = s_old[s, p] * e_end[s][:, sls[p]] + jnp.where(same_head, upd[s, p], 0.0)

    y = jnp.concatenate([jnp.concatenate([ys[s, p] for p in ps], axis=1) if pg > 1 else ys[s, 0]
                         for s in seqs], axis=0)
    inv_n = 1.0 / N_B
    yc = y - seg_sum(y) * inv_n
    yn = yc * lax.rsqrt(seg_sum(yc * yc) * inv_n + LNX_EPS)
    yn = yn * lg_ref[...] + lb_ref[...]
    bonus = seg_sum(r * kh * rk_ref[...]) * v
    o_ref[...] = ((yn + bonus) * gate).astype(o_ref.dtype).reshape(nseq, c, gw)

    @pl.when(t_idx == pl.num_programs(2) - 1)
    def _():
        for s in seqs:
            for p in range(pg):
                st = s_scr[s * pg + p]
                sout_ref[s, 2 * p] = st[:N_B, :N_B]
                sout_ref[s, 2 * p + 1] = st[N_B:, N_B:]


def rwkv_branch(p, prev, mu, w0, a0, k_k, k_a, r_k, lnx_g, lnx_b, w2p, a2p, g2, s0, b, t, c, pg, nseq):
    nc = t // c
    gw = pg * LANES
    ng = W_B // gw
    tw = XB_PAD - 3 * W_B
    tail_blk = 3 * W_B // tw
    seg = lambda k: pl.BlockSpec((nseq, c, gw), lambda bi, g, ti: (bi, ti, P3_OFF // gw + k * ng + g))
    pseg = lambda k: pl.BlockSpec((nseq, 1, gw), lambda bi, g, ti: (bi, 0, k * ng + g))
    mseg = lambda k: pl.BlockSpec((1, gw), lambda bi, g, ti: (0, k * ng + g))
    chan = pl.BlockSpec((1, gw), lambda bi, g, ti: (0, g))
    state = pl.BlockSpec((nseq, 2 * pg, N_B, N_B), lambda bi, g, ti: (bi, g, 0, 0))
    return pl.pallas_call(
        functools.partial(_rwkv_kernel, c=c, pg=pg, nseq=nseq),
        grid=(b // nseq, ng, nc),
        in_specs=[seg(0), seg(1), seg(2),
                  pl.BlockSpec((nseq, c, tw), lambda bi, g, ti: (bi, ti, P3_OFF // tw + tail_blk)),
                  pseg(0), pseg(1), pseg(2),
                  pl.BlockSpec((nseq, 1, tw), lambda bi, g, ti: (bi, 0, tail_blk)),
                  mseg(0), mseg(1), mseg(2),
                  pl.BlockSpec((1, tw), lambda bi, g, ti: (0, tail_blk)),
                  chan, chan, chan, chan, chan, chan, chan,
                  pl.BlockSpec((LANES, gw), lambda bi, g, ti: (0, g)),
                  pl.BlockSpec((LANES, gw), lambda bi, g, ti: (0, g)),
                  pl.BlockSpec((LORA_G, gw), lambda bi, g, ti: (0, g)),
                  state],
        out_specs=[pl.BlockSpec((nseq, c, gw), lambda bi, g, ti: (bi, ti, g)), state],
        out_shape=[jax.ShapeDtypeStruct((b, t, W_B), BF16),
                   jax.ShapeDtypeStruct((b, H_B, N_B, N_B), F32)],
        scratch_shapes=[pltpu.VMEM((nseq, 1, gw), F32)] * 3 + [pltpu.VMEM((nseq, 1, tw), F32),
                                                               pltpu.VMEM((nseq * pg, LANES, LANES), F32)],
        compiler_params=_params("parallel", "parallel", "arbitrary"),
        name="rwkv_branch",
    )(p, p, p, p, prev, prev, prev, prev, mu, mu, mu, mu,
      w0, a0, k_k, k_a, r_k, lnx_g, lnx_b, w2p, a2p, g2, s0)


def _pad_xb(a, ab=None):
    lead = a.shape[:-1]
    z = lambda n: jnp.zeros(lead + (n,), a.dtype)
    o_w, o_a, o_g = 3 * W_B, 3 * W_B + LORA_W, 3 * W_B + LORA_W + LORA_A
    mid = z(2 * H_A) if ab is None else ab
    return jnp.concatenate([a[..., :o_a], mid, z(LANES - LORA_W - 2 * H_A),
                            a[..., o_a:o_g], z(LANES - LORA_A), a[..., o_g:]], axis=-1)


def _unpad_xb(a):
    o_a = 3 * W_B + LORA_W
    return jnp.concatenate([a[..., :o_a], a[..., 3 * W_B + LANES:3 * W_B + LANES + LORA_A],
                            a[..., 3 * W_B + 2 * LANES:]], axis=-1)


def _pad_rows(w, n):
    return jnp.concatenate([w, jnp.zeros((n - w.shape[0],) + w.shape[1:], w.dtype)], axis=0)


def _lane_vec(v, start):
    return jnp.zeros((1, LANES), F32).at[0, start:start + v.shape[0]].set(v)


def _prepare(lw):
    (g_mix, w_in, conv_w, a_log, dt_bias, gdn_norm_g, mu_shift, w0, w2, a0, a2, g2,
     k_k, k_a, r_k, lnx_g, lnx_b, w_up_a, w_up_b, w_o, g_mlp, w_ff1, w_ff2,
     g_ple, w_ple_gate, w_ple) = lw
    o_alpha = CONV_CH
    o_gate_a = CONV_CH + 2 * H_A
    o_xb = o_gate_a + W_A
    o_gm = o_xb + SHIFT_W
    o_xa = o_xb + 3 * W_B + LORA_W
    o_xg = o_xa + LORA_A
    row = lambda v: v.reshape(1, -1)
    w_in_t = w_in.T
    zrows = lambda n: jnp.zeros((n, D_MODEL), F32)
    return dict(
        g_mix=row(g_mix),
        w_proj_t=jnp.concatenate([
            w_in_t[:CONV_CH], w_in_t[o_gate_a:o_xb], w_in_t[o_gm:],
            w_in_t[o_xb:o_xa], w_in_t[o_alpha:o_gate_a], zrows(LANES - LORA_W - 2 * H_A),
            w_in_t[o_xa:o_xg], zrows(LANES - LORA_A), w_in_t[o_xg:o_gm]], axis=0).astype(BF16),
        conv_w=conv_w,
        alog_vec=_lane_vec(a_log, ALPHA_LANE), dt_vec=_lane_vec(dt_bias, ALPHA_LANE),
        gdn_norm_g=row(gdn_norm_g),
        mu=row(_pad_xb(mu_shift)),
        w0=row(w0), a0=row(a0), k_k=row(k_k), k_a=row(k_a), r_k=row(r_k.reshape(-1)),
        lnx_g=row(lnx_g), lnx_b=row(lnx_b),
        w2p=_pad_rows(w2, LANES).astype(BF16), a2p=_pad_rows(a2, LANES).astype(BF16), g2=g2.astype(BF16),
        w_up_a=w_up_a.astype(BF16), w_up_b=w_up_b.astype(BF16), w_o=w_o.astype(BF16),
        g_mlp=row(g_mlp), w_ff1=w_ff1.astype(BF16), w_ff2=w_ff2.astype(BF16),
        g_ple=row(g_ple), w_ple_gate=w_ple_gate.astype(BF16), w_ple=w_ple.astype(BF16),
    )


def _run_layer(x, ple, s_gdn, buf_gdn, s_rwkv, shift_rwkv, pw, g_final):
    b, t, _ = x.shape
    m = b * t
    c = min(CHUNK, t)
    assert t % c == 0 and c % SUBLANES == 0 and t >= CONV_W - 1 and b % SEQS_PER_STEP == 0
    tm_proj = min(m, 2048)
    tm_stream = min(m, 1024)
    tm_res = min(m, 256)
    x2 = x.reshape(m, D_MODEL)

    p = norm_matmul(x2, pw["g_mix"], pw["w_proj_t"], tm_proj, 512)
    p3d = p.reshape(b, t, N_PROJ)

    hist8 = jnp.concatenate([jnp.zeros((b, SUBLANES - (CONV_W - 1), CONV_CH), F32), buf_gdn], axis=1)
    o_a, s_gdn_new = gdn_branch(p3d, hist8, pw["conv_w"], pw["alog_vec"], pw["dt_vec"],
                                pw["gdn_norm_g"], s_gdn, b, t, c, hb=GDN_HB, nseq=SEQS_PER_STEP)
    prev = _pad_xb(shift_rwkv).reshape(b, 1, XB_PAD)
    o_b, s_rwkv_new = rwkv_branch(p3d, prev, pw["mu"], pw["w0"], pw["a0"], pw["k_k"], pw["k_a"], pw["r_k"],
                                  pw["lnx_g"], pw["lnx_b"], pw["w2p"], pw["a2p"], pw["g2"], s_rwkv,
                                  b, t, c, pg=RWKV_PG, nseq=SEQS_PER_STEP)

    x1, h2 = merge_out(x2, o_a.reshape(m, W_A), o_b.reshape(m, W_B), p, P2_OFF // D_MODEL,
                       pw["w_up_a"], pw["w_up_b"], pw["w_o"], pw["g_mlp"], tm_res, 512)
    delta = mlp(h2, pw["w_ff1"], pw["w_ff2"], tm_stream, 1024, 512)
    y = ple_final(x1, delta, ple.reshape(m, D_PLE), pw["g_ple"], pw["w_ple_gate"], pw["w_ple"], g_final,
                  min(m, 512))

    new_buf = p3d[:, t - (CONV_W - 1):, P1_OFF:P1_OFF + CONV_CH]
    new_shift = _unpad_xb(p3d[:, t - 1, P3_OFF:])
    return y.reshape(b, t, D_MODEL), s_gdn_new, new_buf, s_rwkv_new, new_shift


def kernel(x_prompt, x_sample, p_prompt, p_sample, state_gdn, cache_gdn_conv, state_rwkv, cache_rwkv_shift, g_mix, w_in, conv_w, a_log, dt_bias, gdn_norm_g, mu_shift, w0, w2, a0, a2, g2, k_k, k_a, r_k, lnx_g, lnx_b, w_up_a, w_up_b, w_o, g_mlp, w_ff1, w_ff2, g_ple, w_ple_gate, w_ple, g_final):
    params = (g_mix, w_in, conv_w, a_log, dt_bias, gdn_norm_g, mu_shift, w0, w2, a0, a2, g2,
              k_k, k_a, r_k, lnx_g, lnx_b, w_up_a, w_up_b, w_o, g_mlp, w_ff1, w_ff2,
              g_ple, w_ple_gate, w_ple)
    depth = w_in.shape[0]
    assert depth == 1
    pw = _prepare(tuple(w[0] for w in params))
    gf = g_final.reshape(1, -1)
    bp = x_prompt.shape[0]
    z_sg = jnp.zeros((bp, H_A, DK_A, DV_A), F32)
    z_buf = jnp.zeros((bp, CONV_W - 1, CONV_CH), F32)
    z_sr = jnp.zeros((bp, H_B, N_B, N_B), F32)
    z_sh = jnp.zeros((bp, SHIFT_W), F32)
    y_p, sg_p, buf_p, sr_p, sh_p = _run_layer(x_prompt, p_prompt[0], z_sg, z_buf, z_sr, z_sh, pw, gf)
    y_s, sg_s, buf_s, sr_s, sh_s = _run_layer(x_sample, p_sample[0], state_gdn[0], cache_gdn_conv[0],
                                              state_rwkv[0], cache_rwkv_shift[0], pw, gf)
    st = lambda a: a[None]
    return (y_p, y_s, st(sg_p), st(buf_p), st(sr_p), st(sh_p), st(sg_s), st(buf_s), st(sr_s), st(sh_s))
```

```python
import functools

import jax
import jax.numpy as jnp
from jax import lax
from jax.experimental import pallas as pl
from jax.experimental.pallas import tpu as pltpu

F32 = jnp.float32
BF16 = jnp.bfloat16

D_MODEL = 2048
H_A, DK_A, DV_A = 8, 128, 128
W_A = H_A * DV_A
CONV_W = 4
CONV_CH = 3 * W_A
H_B, N_B = 16, 64
W_B = H_B * N_B
LORA_W, LORA_A, LORA_G = 96, 96, 256
SHIFT_W = 3 * W_B + LORA_W + LORA_A + LORA_G
D_FF = 4 * D_MODEL
D_PLE = 256
EPS = 1e-6
LNX_EPS = 64e-5
CHUNK = 64
GDN_HB = 8
RWKV_PG = 8
SEQS_PER_STEP = 4

LANES = 128
SUBLANES = 8
VMEM_LIMIT = 52 * 2 ** 20

XB_PAD = 3 * W_B + 2 * LANES + LORA_G
AB_BLOCK = 3 * W_B // LANES
ALPHA_LANE = LORA_W
BETA_LANE = LORA_W + H_A

P1_OFF, P2_OFF, P3_OFF = 0, 4 * W_A, 4 * W_A + 2 * D_MODEL
N_PROJ = P3_OFF + XB_PAD

NN = ((1,), (0,))
NT = ((1,), (1,))
TN = ((0,), (0,))


def _dg(a, b, dims):
    return lax.dot_general(a, b, (dims, ((), ())), preferred_element_type=F32)


def _dot1(a, b, dims=NN):
    return _dg(a.astype(BF16), b.astype(BF16), dims)


def _cumsum_rows(x, c):
    rows = x.shape[0]
    i = lax.broadcasted_iota(jnp.int32, (rows, rows), 0)
    j = lax.broadcasted_iota(jnp.int32, (rows, rows), 1)
    same_chunk = (i ^ j) < c
    tri = jnp.where((i >= j) & same_chunk, 1.0, 0.0).astype(BF16)
    h1 = x.astype(BF16)
    r1 = x - h1.astype(F32)
    h2 = r1.astype(BF16)
    h3 = (r1 - h2.astype(F32)).astype(BF16)
    return _dg(tri, h1, NN) + (_dg(tri, h2, NN) + _dg(tri, h3, NN))


def _tri_inv_many(a_list, c):
    i = lax.broadcasted_iota(jnp.int32, (c, c), 0)
    j = lax.broadcasted_iota(jnp.int32, (c, c), 1)
    ns = [-jnp.where((i ^ j) == 1, a, 0.0) for a in a_list]
    s, ls = 2, 1
    while s < c:
        mask = ((i ^ j) >> ls) == 1
        l_s = [jnp.where(mask, a, 0.0) for a in a_list]
        nl = [_dot1(n, l) for n, l in zip(ns, l_s)]
        ys = [l + x for l, x in zip(l_s, nl)]
        yn = [_dot1(y, n) for y, n in zip(ys, ns)]
        ns = [n - (y + z) for n, y, z in zip(ns, ys, yn)]
        s, ls = 2 * s, ls + 1
    eye = jnp.where(i == j, 1.0, 0.0)
    return [eye + n for n in ns]


def _pair_stack(x, c):
    n = x.shape[1]
    bi = lax.broadcasted_iota(jnp.int32, (2 * c, n), 0)
    bj = lax.broadcasted_iota(jnp.int32, (2 * c, n), 1)
    keep = (bi >= c) == (bj >= n // 2)
    return jnp.where(keep, jnp.concatenate([x, x], axis=0), 0.0)


def _tri_inv_packed(a_list, c):
    i = lax.broadcasted_iota(jnp.int32, (c, 2 * c), 0)
    jm = lax.broadcasted_iota(jnp.int32, (c, 2 * c), 1) & (c - 1)
    ns = [-jnp.where((i ^ jm) == 1, a, 0.0) for a in a_list]
    s, ls = 2, 1
    while s < c:
        mask = ((i ^ jm) >> ls) == 1
        l_s = [jnp.where(mask, a, 0.0) for a in a_list]
        nl = [_dot1(n, _pair_stack(l, c)) for n, l in zip(ns, l_s)]
        ys = [l + x for l, x in zip(l_s, nl)]
        yn = [_dot1(y, _pair_stack(n, c)) for y, n in zip(ys, ns)]
        ns = [n - (y + z) for n, y, z in zip(ns, ys, yn)]
        s, ls = 2 * s, ls + 1
    eye = jnp.where(i == jm, 1.0, 0.0)
    return [eye + n for n in ns]


def _softplus(x):
    return jnp.maximum(x, 0.0) + jnp.log1p(jnp.exp(-jnp.abs(x)))


def _sigmoid(x):
    return jax.nn.sigmoid(x)


def _rms(x, g):
    return x * lax.rsqrt(jnp.mean(x * x, axis=-1, keepdims=True) + EPS) * g


def _params(*sem):
    return pltpu.CompilerParams(dimension_semantics=sem, vmem_limit_bytes=VMEM_LIMIT)


def _norm_matmul_kernel(x_ref, g_ref, w_ref, o_ref, h_scr):
    @pl.when(pl.program_id(1) == 0)
    def _():
        h_scr[...] = _rms(x_ref[...], g_ref[...]).astype(BF16)

    o_ref[...] = _dg(h_scr[...], w_ref[...], NT).astype(o_ref.dtype)


def norm_matmul(x, g, wt, tm, tn, out_dtype=F32):
    m, k = x.shape
    n = wt.shape[0]
    return pl.pallas_call(
        _norm_matmul_kernel,
        grid=(m // tm, n // tn),
        in_specs=[pl.BlockSpec((tm, k), lambda i, j: (i, 0), pipeline_mode=pl.Buffered(1)),
                  pl.BlockSpec((1, k), lambda i, j: (0, 0)),
                  pl.BlockSpec((tn, k), lambda i, j: (j, 0))],
        out_specs=pl.BlockSpec((tm, tn), lambda i, j: (i, j)),
        out_shape=jax.ShapeDtypeStruct((m, n), out_dtype),
        scratch_shapes=[pltpu.VMEM((tm, k), BF16)],
        compiler_params=_params("parallel", "arbitrary"),
        name="norm_matmul",
    )(x, g, wt)


def _merge_kernel(x_ref, oa_ref, ob_ref, gma_ref, gmb_ref, wa_ref, wb_ref, wo_ref, g_ref, o_ref, h_ref, *, tn):
    oa = oa_ref[...]
    ob = ob_ref[...]
    acc = x_ref[...]
    for n in range(D_MODEL // tn):
        cs = slice(n * tn, (n + 1) * tn)
        merged = (_sigmoid(gma_ref[:, cs]) * _dg(oa, wa_ref[:, cs], NN)
                  + _sigmoid(gmb_ref[:, cs]) * _dg(ob, wb_ref[:, cs], NN))
        acc = acc + _dg(merged.astype(BF16), wo_ref[cs, :], NN)
    o_ref[...] = acc
    h_ref[...] = _rms(acc, g_ref[...]).astype(BF16)


def merge_out(x, o_a, o_b, p, gate_blk, w_up_a, w_up_b, w_o, g_mlp, tm, tn):
    m = x.shape[0]
    resident = lambda shape: pl.BlockSpec(shape, lambda i: (0, 0), pipeline_mode=pl.Buffered(1))
    return pl.pallas_call(
        functools.partial(_merge_kernel, tn=tn),
        grid=(m // tm,),
        in_specs=[pl.BlockSpec((tm, D_MODEL), lambda i: (i, 0)),
                  pl.BlockSpec((tm, W_A), lambda i: (i, 0)),
                  pl.BlockSpec((tm, W_B), lambda i: (i, 0)),
                  pl.BlockSpec((tm, D_MODEL), lambda i: (i, gate_blk)),
                  pl.BlockSpec((tm, D_MODEL), lambda i: (i, gate_blk + 1)),
                  resident((W_A, D_MODEL)), resident((W_B, D_MODEL)), resident((D_MODEL, D_MODEL)),
                  pl.BlockSpec((1, D_MODEL), lambda i: (0, 0))],
        out_specs=[pl.BlockSpec((tm, D_MODEL), lambda i: (i, 0)),
                   pl.BlockSpec((tm, D_MODEL), lambda i: (i, 0))],
        out_shape=[jax.ShapeDtypeStruct((m, D_MODEL), F32), jax.ShapeDtypeStruct((m, D_MODEL), BF16)],
        compiler_params=_params("parallel"),
        name="merge_out",
    )(x, o_a, o_b, p, p, w_up_a, w_up_b, w_o, g_mlp)


def _mlp_kernel(h_ref, w1_ref, w2_ref, o_ref, *, tc):
    f = pl.program_id(1)

    @pl.when(f == 0)
    def _():
        o_ref[...] = jnp.zeros_like(o_ref)

    h = h_ref[...]
    for fc in range(w1_ref.shape[1] // tc):
        fs = slice(fc * tc, (fc + 1) * tc)
        a = jnp.maximum(_dg(h, w1_ref[:, fs], NN), 0.0)
        a = (a * a).astype(BF16)
        for nc in range(D_MODEL // tc):
            ns = slice(nc * tc, (nc + 1) * tc)
            o_ref[:, ns] += _dg(a, w2_ref[fs, ns], NN)


def mlp(h, w1, w2, tm, tf, tc):
    m = h.shape[0]
    return pl.pallas_call(
        functools.partial(_mlp_kernel, tc=tc),
        grid=(m // tm, D_FF // tf),
        in_specs=[pl.BlockSpec((tm, D_MODEL), lambda i, f: (i, 0)),
                  pl.BlockSpec((D_MODEL, tf), lambda i, f: (0, f)),
                  pl.BlockSpec((tf, D_MODEL), lambda i, f: (f, 0))],
        out_specs=pl.BlockSpec((tm, D_MODEL), lambda i, f: (i, 0)),
        out_shape=jax.ShapeDtypeStruct((m, D_MODEL), F32),
        compiler_params=_params("parallel", "arbitrary"),
        name="mlp",
    )(h, w1, w2)


def _ple_final_kernel(x_ref, d_ref, p_ref, gp_ref, wg_ref, wp_ref, gf_ref, o_ref):
    x = x_ref[...] + d_ref[...]
    gate = _sigmoid(_dg(_rms(x, gp_ref[...]).astype(BF16), wg_ref[...], NN))
    x = x + gate * _dg(p_ref[...].astype(BF16), wp_ref[...], NN)
    o_ref[...] = _rms(x, gf_ref[...])


def ple_final(x, delta, ple, g_ple, w_gate, w_ple, g_final, tm):
    m = x.shape[0]
    resident = lambda shape: pl.BlockSpec(shape, lambda i: (0, 0), pipeline_mode=pl.Buffered(1))
    return pl.pallas_call(
        _ple_final_kernel,
        grid=(m // tm,),
        in_specs=[pl.BlockSpec((tm, D_MODEL), lambda i: (i, 0)),
                  pl.BlockSpec((tm, D_MODEL), lambda i: (i, 0)),
                  pl.BlockSpec((tm, D_PLE), lambda i: (i, 0)),
                  pl.BlockSpec((1, D_MODEL), lambda i: (0, 0)),
                  resident((D_MODEL, D_MODEL)), resident((D_PLE, D_MODEL)),
                  pl.BlockSpec((1, D_MODEL), lambda i: (0, 0))],
        out_specs=pl.BlockSpec((tm, D_MODEL), lambda i: (i, 0)),
        out_shape=jax.ShapeDtypeStruct((m, D_MODEL), F32),
        compiler_params=_params("parallel"),
        name="ple_final",
    )(x, delta, ple, g_ple, w_gate, w_ple, g_final)


def _gdn_kernel(q_ref, k_ref, v_ref, gate_ref, ab_ref, hq_ref, hk_ref, hv_ref,
                cq_ref, ck_ref, cv_ref, alog_ref, dt_ref, ng_ref, s0_ref,
                o_ref, sout_ref, qs, ks, vs, s_scr, *, c, hb, nseq):
    g_idx = pl.program_id(1)
    t_idx = pl.program_id(2)
    hist = SUBLANES
    seqs = range(nseq)

    @pl.when(t_idx == 0)
    def _():
        for s in seqs:
            for h in range(hb):
                s_scr[s * hb + h] = s0_ref[s, h]
            qs[s] = hq_ref[s]
            ks[s] = hk_ref[s]
            vs[s] = hv_ref[s]

    row8 = lax.broadcasted_iota(jnp.int32, (hist, hb * DK_A), 0)

    def conv_silu(scr, x_ref, w_ref):
        outs = []
        for s in seqs:
            u = x_ref[s]
            prev8 = scr[s]
            out = None
            for j in range(CONV_W - 1):
                sh = CONV_W - 1 - j
                rolled = pltpu.roll(u, sh, axis=0)
                top = jnp.where(row8 < sh, pltpu.roll(prev8, sh, axis=0), rolled[0:hist])
                term = jnp.concatenate([top, rolled[hist:]], axis=0) * w_ref[j:j + 1, :]
                out = term if out is None else out + term
            out = out + u * w_ref[CONV_W - 1:CONV_W, :]
            scr[s] = x_ref[s, c - hist:c, :]
            outs.append(out)
        out = jnp.concatenate(outs, axis=0) if nseq > 1 else outs[0]
        return out * _sigmoid(out)

    qc = conv_silu(qs, q_ref, cq_ref)
    kc = conv_silu(ks, k_ref, ck_ref)
    vc = conv_silu(vs, v_ref, cv_ref)

    ab = ab_ref[...].reshape(nseq * c, LANES)
    g_all = -jnp.exp(alog_ref[...]) * _softplus(ab + dt_ref[...])
    beta_all = _sigmoid(ab)
    gcum = _cumsum_rows(g_all, c)
    lane = lax.broadcasted_iota(jnp.int32, (c, LANES), 1)
    sub_t = lax.broadcasted_iota(jnp.int32, (LANES, c), 0)
    i = lax.broadcasted_iota(jnp.int32, (c, c), 0)
    j = lax.broadcasted_iota(jnp.int32, (c, c), 1)
    incl = i >= j
    strict = i > j
    last_row = lax.broadcasted_iota(jnp.int32, (c, 1), 0) == c - 1
    gate = gate_ref[...].reshape(nseq * c, hb * DV_A)
    ng = ng_ref[...]

    hs = range(hb)
    sls = [slice(h * DK_A, (h + 1) * DK_A) for h in hs]
    rs = [slice(s * c, (s + 1) * c) for s in seqs]
    heads = [g_idx * hb + h for h in hs]
    items = [(s, h) for s in seqs for h in hs]
    gcum_s = [gcum[r] for r in rs]
    gcum_t = [g.T for g in gcum_s]
    beta_s = [beta_all[r] for r in rs]
    qn = [qc[:, sl] for sl in sls]
    kn = [kc[:, sl] for sl in sls]
    qn = [x * lax.rsqrt(jnp.sum(x * x, axis=-1, keepdims=True) + EPS) * (DK_A ** -0.5) for x in qn]
    kn = [x * lax.rsqrt(jnp.sum(x * x, axis=-1, keepdims=True) + EPS) for x in kn]
    gc = {(s, h): jnp.sum(jnp.where(lane == ALPHA_LANE + heads[h], gcum_s[s], 0.0), axis=1, keepdims=True)
          for s, h in items}
    gr = {(s, h): jnp.sum(jnp.where(sub_t == ALPHA_LANE + heads[h], gcum_t[s], 0.0), axis=0, keepdims=True)
          for s, h in items}
    bc = {(s, h): jnp.sum(jnp.where(lane == BETA_LANE + heads[h], beta_s[s], 0.0), axis=1, keepdims=True)
          for s, h in items}
    qh = {(s, h): qn[h][rs[s]] for s, h in items}
    kh = {(s, h): kn[h][rs[s]] for s, h in items}
    vh = {(s, h): vc[rs[s], sls[h]] for s, h in items}
    dec = {n: jnp.exp(jnp.where(incl, gc[n] - gr[n], -jnp.inf)) for n in items}
    kb = {n: kh[n] * bc[n] for n in items}
    kk = {n: _dot1(kb[n], kh[n], NT) for n in items}
    t_inv = dict(zip(items, _tri_inv_many([jnp.where(strict, kk[n] * dec[n], 0.0) for n in items], c)))
    eg = {n: jnp.exp(gc[n]) for n in items}
    uw = {n: _dot1(t_inv[n], jnp.concatenate([vh[n] * bc[n], kb[n] * eg[n]], axis=1)) for n in items}
    qk = {n: jnp.where(incl, _dot1(qh[n], kh[n], NT) * dec[n], 0.0) for n in items}
    g_last = {n: jnp.sum(jnp.where(last_row, gc[n], 0.0), axis=0, keepdims=True) for n in items}
    kd = {n: kh[n] * jnp.exp(g_last[n] - gc[n]) for n in items}
    s_old = {(s, h): s_scr[s * hb + h] for s, h in items}
    ws_qs = {n: _dot1(jnp.concatenate([uw[n][:, DV_A:], qh[n] * eg[n]], axis=0), s_old[n]) for n in items}
    v_new = {n: uw[n][:, :DV_A] - ws_qs[n][:c] for n in items}
    o = {n: ws_qs[n][c:] + _dot1(qk[n], v_new[n]) for n in items}
    upd = {n: _dot1(kd[n], v_new[n], TN) for n in items}
    for s, h in items:
        s_scr[s * hb + h] = s_old[s, h] * jnp.exp(g_last[s, h]) + upd[s, h]
    for s, h in items:
        gh = gate[rs[s], sls[h]]
        o_ref[s, :, sls[h]] = (_rms(o[s, h], ng) * (gh * _sigmoid(gh))).astype(o_ref.dtype)

    @pl.when(t_idx == pl.num_programs(2) - 1)
    def _():
        for s in seqs:
            for h in range(hb):
                sout_ref[s, h] = s_scr[s * hb + h]


def gdn_branch(p, hist8, conv_w, alog_vec, dt_vec, norm_g, s0, b, t, c, hb, nseq):
    nc = t // c
    ng = H_A // hb
    gw = hb * DK_A
    seg = lambda k: pl.BlockSpec((nseq, c, gw), lambda bi, g, ti: (bi, ti, P1_OFF // gw + k * ng + g))
    hist = lambda k: pl.BlockSpec((nseq, SUBLANES, gw), lambda bi, g, ti: (bi, 0, k * ng + g))
    cw = lambda k: pl.BlockSpec((CONV_W, gw), lambda bi, g, ti: (0, k * ng + g))
    vec = pl.BlockSpec((1, LANES), lambda bi, g, ti: (0, 0))
    state = pl.BlockSpec((nseq, hb, DK_A, DV_A), lambda bi, g, ti: (bi, g, 0, 0))
    return pl.pallas_call(
        functools.partial(_gdn_kernel, c=c, hb=hb, nseq=nseq),
        grid=(b // nseq, ng, nc),
        in_specs=[seg(0), seg(1), seg(2), seg(3),
                  pl.BlockSpec((nseq, c, LANES), lambda bi, g, ti: (bi, ti, P3_OFF // LANES + AB_BLOCK)),
                  hist(0), hist(1), hist(2), cw(0), cw(1), cw(2), vec, vec, vec, state],
        out_specs=[pl.BlockSpec((nseq, c, gw), lambda bi, g, ti: (bi, ti, g)), state],
        out_shape=[jax.ShapeDtypeStruct((b, t, W_A), BF16),
                   jax.ShapeDtypeStruct((b, H_A, DK_A, DV_A), F32)],
        scratch_shapes=[pltpu.VMEM((nseq, SUBLANES, gw), F32)] * 3 + [pltpu.VMEM((nseq * hb, DK_A, DV_A), F32)],
        compiler_params=_params("parallel", "parallel", "arbitrary"),
        name="gdn_branch",
    )(p, p, p, p, p, hist8, hist8, hist8, conv_w, conv_w, conv_w, alog_vec, dt_vec, norm_g, s0)


def _rwkv_kernel(r_ref, k_ref, v_ref, tail_ref, pr_ref, pk_ref, pv_ref, pt_ref,
                 mr_ref, mk_ref, mv_ref, mt_ref, w0_ref, a0_ref, kk_ref, ka_ref, rk_ref,
                 lg_ref, lb_ref, w2_ref, a2_ref, g2_ref, s0_ref,
                 o_ref, sout_ref, prev_r, prev_k, prev_v, prev_t, s_scr, *, c, pg, nseq):
    t_idx = pl.program_id(2)
    seqs = range(nseq)
    rows = nseq * c

    @pl.when(t_idx == 0)
    def _():
        prev_r[...] = pr_ref[...]
        prev_k[...] = pk_ref[...]
        prev_v[...] = pv_ref[...]
        prev_t[...] = pt_ref[...]
        z = jnp.zeros((N_B, N_B), F32)
        for s in seqs:
            for p in range(pg):
                s_a = s0_ref[s, 2 * p]
                s_b = s0_ref[s, 2 * p + 1]
                s_scr[s * pg + p] = jnp.concatenate([jnp.concatenate([s_a, z], axis=1),
                                                     jnp.concatenate([z, s_b], axis=1)], axis=0)

    def mix(x_ref, prev, mu_ref):
        outs = []
        for s in seqs:
            x = x_ref[s]
            row = lax.broadcasted_iota(jnp.int32, x.shape, 0)
            shifted = jnp.where(row == 0, prev[s], pltpu.roll(x, 1, axis=0))
            prev[s] = x_ref[s, c - 1:c, :]
            outs.append(x + (shifted - x) * mu_ref[...])
        return jnp.concatenate(outs, axis=0) if nseq > 1 else outs[0]

    r = mix(r_ref, prev_r, mr_ref)
    k = mix(k_ref, prev_k, mk_ref)
    v = mix(v_ref, prev_v, mv_ref)
    tail = mix(tail_ref, prev_t, mt_ref)
    xw = tail[:, 0:LANES]
    xa = tail[:, LANES:2 * LANES]
    xg = tail[:, 2 * LANES:]

    w_log = -_softplus(-(w0_ref[...] + _dot1(jnp.tanh(xw), w2_ref[...]))) - 0.5
    lw = -jnp.exp(w_log)
    a = _sigmoid(a0_ref[...] + _dot1(xa, a2_ref[...]))
    gate = _dot1(_sigmoid(xg), g2_ref[...])

    gw = pg * LANES
    fp = (lax.broadcasted_iota(jnp.int32, (rows, LANES), 1) & N_B) == 0

    def seg_sum(x):
        outs = []
        for p in range(pg):
            xp = x[:, p * LANES:(p + 1) * LANES]
            s_a = jnp.sum(jnp.where(fp, xp, 0.0), axis=-1, keepdims=True)
            s_b = jnp.sum(jnp.where(fp, 0.0, xp), axis=-1, keepdims=True)
            outs.append(jnp.where(fp, s_a, s_b))
        return jnp.concatenate(outs, axis=1) if pg > 1 else outs[0]

    kkx = k * kk_ref[...]
    kk = kkx * lax.rsqrt(seg_sum(kkx * kkx) + EPS)
    kh = k * (1.0 + (a - 1.0) * ka_ref[...])
    bb = kk * a
    cw = _cumsum_rows(lw, c)
    rt = r * jnp.exp(cw)
    e_neg = jnp.exp(-cw)
    kt = kh * e_neg
    bt = bb * e_neg
    at = kk * jnp.exp(cw - lw)
    row_c = lax.broadcasted_iota(jnp.int32, (c, gw), 0)
    rs = [slice(s * c, (s + 1) * c) for s in seqs]
    cw_s = [cw[rc] for rc in rs]
    cw_last = [jnp.sum(jnp.where(row_c == c - 1, x, 0.0), axis=0, keepdims=True) for x in cw_s]
    e_last = [jnp.exp(cw_last[s] - cw_s[s]) for s in seqs]
    k_hat = [kh[rs[s]] * e_last[s] for s in seqs]
    b_hat = [bb[rs[s]] * e_last[s] for s in seqs]
    e_end = [jnp.exp(x) for x in cw_last]

    i = lax.broadcasted_iota(jnp.int32, (c, 2 * c), 0)
    jm = lax.broadcasted_iota(jnp.int32, (c, 2 * c), 1) & (c - 1)
    incl = i >= jm
    strict = i > jm
    bi = lax.broadcasted_iota(jnp.int32, (LANES, LANES), 0)
    bj = lax.broadcasted_iota(jnp.int32, (LANES, LANES), 1)
    same_head = ((bi ^ bj) & N_B) == 0

    ps = range(pg)
    sls = [slice(p * LANES, (p + 1) * LANES) for p in ps]
    items = [(s, p) for s in seqs for p in ps]
    at_p = {(s, p): at[rs[s], sls[p]] for s, p in items}
    rt_p = {(s, p): rt[rs[s], sls[p]] for s, p in items}
    v_p = {(s, p): v[rs[s], sls[p]] for s, p in items}
    s_old = {(s, p): s_scr[s * pg + p] for s, p in items}
    a_s = {n: _dot1(at_p[n], s_old[n], NT) for n in items}
    r_s = {n: _dot1(rt_p[n], s_old[n], NT) for n in items}
    lhs = {n: jnp.concatenate([at_p[n], rt_p[n]], axis=0) for n in items}
    xb_ = {(s, p): _dot1(lhs[s, p], _pair_stack(bt[rs[s], sls[p]], c), NT) for s, p in items}
    xk_ = {(s, p): _dot1(lhs[s, p], _pair_stack(kt[rs[s], sls[p]], c), NT) for s, p in items}
    l_ak = {n: jnp.where(strict, xk_[n][:c], 0.0) for n in items}
    r_b = {n: jnp.where(incl, xb_[n][c:], 0.0) for n in items}
    r_k = {n: jnp.where(incl, xk_[n][c:], 0.0) for n in items}
    t_inv = dict(zip(items, _tri_inv_packed([jnp.where(strict, xb_[n][:c], 0.0) for n in items], c)))
    v_st = {n: _pair_stack(v_p[n], c) for n in items}
    rhs = {n: a_s[n] + _dot1(l_ak[n], v_st[n]) for n in items}
    p_all = {n: _dot1(t_inv[n], _pair_stack(rhs[n], c)) for n in items}
    ys = {n: r_s[n] + _dot1(r_k[n], v_st[n]) - _dot1(r_b[n], _pair_stack(p_all[n], c)) for n in items}
    upd = {(s, p): _dot1(jnp.concatenate([v_p[s, p], p_all[s, p]], axis=0),
                         jnp.concatenate([k_hat[s][:, sls[p]], -b_hat[s][:, sls[p]]], axis=0), TN)
           for s, p in items}
    for s, p in items:
        s_scr[s * pg + p] = s_old[s, p] * e_end[s][:, sls[p]] + jnp.where(same_head, upd[s, p], 0.0)

    y = jnp.concatenate([jnp.concatenate([ys[s, p] for p in ps], axis=1) if pg > 1 else ys[s, 0]
                         for s in seqs], axis=0)
    inv_n = 1.0 / N_B
    yc = y - seg_sum(y) * inv_n
    yn = yc * lax.rsqrt(seg_sum(yc * yc) * inv_n + LNX_EPS)
    yn = yn * lg_ref[...] + lb_ref[...]
    bonus = seg_sum(r * kh * rk_ref[...]) * v
    o_ref[...] = ((yn + bonus) * gate).astype(o_ref.dtype).reshape(nseq, c, gw)

    @pl.when(t_idx == pl.num_programs(2) - 1)
    def _():
        for s in seqs:
            for p in range(pg):
                st = s_scr[s * pg + p]
                sout_ref[s, 2 * p] = st[:N_B, :N_B]
                sout_ref[s, 2 * p + 1] = st[N_B:, N_B:]


def rwkv_branch(p, prev, mu, w0, a0, k_k, k_a, r_k, lnx_g, lnx_b, w2p, a2p, g2, s0, b, t, c, pg, nseq):
    nc = t // c
    gw = pg * LANES
    ng = W_B // gw
    tw = XB_PAD - 3 * W_B
    tail_blk = 3 * W_B // tw
    seg = lambda k: pl.BlockSpec((nseq, c, gw), lambda bi, g, ti: (bi, ti, P3_OFF // gw + k * ng + g))
    pseg = lambda k: pl.BlockSpec((nseq, 1, gw), lambda bi, g, ti: (bi, 0, k * ng + g))
    mseg = lambda k: pl.BlockSpec((1, gw), lambda bi, g, ti: (0, k * ng + g))
    chan = pl.BlockSpec((1, gw), lambda bi, g, ti: (0, g))
    state = pl.BlockSpec((nseq, 2 * pg, N_B, N_B), lambda bi, g, ti: (bi, g, 0, 0))
    return pl.pallas_call(
        functools.partial(_rwkv_kernel, c=c, pg=pg, nseq=nseq),
        grid=(b // nseq, ng, nc),
        in_specs=[seg(0), seg(1), seg(2),
                  pl.BlockSpec((nseq, c, tw), lambda bi, g, ti: (bi, ti, P3_OFF // tw + tail_blk)),
                  pseg(0), pseg(1), pseg(2),
                  pl.BlockSpec((nseq, 1, tw), lambda bi, g, ti: (bi, 0, tail_blk)),
                  mseg(0), mseg(1), mseg(2),
                  pl.BlockSpec((1, tw), lambda bi, g, ti: (0, tail_blk)),
                  chan, chan, chan, chan, chan, chan, chan,
                  pl.BlockSpec((LANES, gw), lambda bi, g, ti: (0, g)),
                  pl.BlockSpec((LANES, gw), lambda bi, g, ti: (0, g)),
                  pl.BlockSpec((LORA_G, gw), lambda bi, g, ti: (0, g)),
                  state],
        out_specs=[pl.BlockSpec((nseq, c, gw), lambda bi, g, ti: (bi, ti, g)), state],
        out_shape=[jax.ShapeDtypeStruct((b, t, W_B), BF16),
                   jax.ShapeDtypeStruct((b, H_B, N_B, N_B), F32)],
        scratch_shapes=[pltpu.VMEM((nseq, 1, gw), F32)] * 3 + [pltpu.VMEM((nseq, 1, tw), F32),
                                                               pltpu.VMEM((nseq * pg, LANES, LANES), F32)],
        compiler_params=_params("parallel", "parallel", "arbitrary"),
        name="rwkv_branch",
    )(p, p, p, p, prev, prev, prev, prev, mu, mu, mu, mu,
      w0, a0, k_k, k_a, r_k, lnx_g, lnx_b, w2p, a2p, g2, s0)


def _pad_xb(a, ab=None):
    lead = a.shape[:-1]
    z = lambda n: jnp.zeros(lead + (n,), a.dtype)
    o_a, o_g = 3 * W_B + LORA_W, 3 * W_B + LORA_W + LORA_A
    mid = z(2 * H_A) if ab is None else ab
    return jnp.concatenate([a[..., :o_a], mid, z(LANES - LORA_W - 2 * H_A),
                            a[..., o_a:o_g], z(LANES - LORA_A), a[..., o_g:]], axis=-1)


def _unpad_xb(a):
    o_a = 3 * W_B + LORA_W
    return jnp.concatenate([a[..., :o_a], a[..., 3 * W_B + LANES:3 * W_B + LANES + LORA_A],
                            a[..., 3 * W_B + 2 * LANES:]], axis=-1)


def _pad_rows(w, n):
    return jnp.concatenate([w, jnp.zeros((n - w.shape[0],) + w.shape[1:], w.dtype)], axis=0)


def _lane_vec(v, start):
    return jnp.zeros((1, LANES), F32).at[0, start:start + v.shape[0]].set(v)


def _prepare(lw):
    (g_mix, w_in, conv_w, a_log, dt_bias, gdn_norm_g, mu_shift, w0, w2, a0, a2, g2,
     k_k, k_a, r_k, lnx_g, lnx_b, w_up_a, w_up_b, w_o, g_mlp, w_ff1, w_ff2,
     g_ple, w_ple_gate, w_ple) = lw
    o_alpha = CONV_CH
    o_gate_a = CONV_CH + 2 * H_A
    o_xb = o_gate_a + W_A
    o_gm = o_xb + SHIFT_W
    o_xa = o_xb + 3 * W_B + LORA_W
    o_xg = o_xa + LORA_A
    row = lambda v: v.reshape(1, -1)
    w_in_t = w_in.T
    zrows = lambda n: jnp.zeros((n, D_MODEL), F32)
    return dict(
        g_mix=row(g_mix),
        w_proj_t=jnp.concatenate([
            w_in_t[:CONV_CH], w_in_t[o_gate_a:o_xb], w_in_t[o_gm:],
            w_in_t[o_xb:o_xa], w_in_t[o_alpha:o_gate_a], zrows(LANES - LORA_W - 2 * H_A),
            w_in_t[o_xa:o_xg], zrows(LANES - LORA_A), w_in_t[o_xg:o_gm]], axis=0).astype(BF16),
        conv_w=conv_w,
        alog_vec=_lane_vec(a_log, ALPHA_LANE), dt_vec=_lane_vec(dt_bias, ALPHA_LANE),
        gdn_norm_g=row(gdn_norm_g),
        mu=row(_pad_xb(mu_shift)),
        w0=row(w0), a0=row(a0), k_k=row(k_k), k_a=row(k_a), r_k=row(r_k.reshape(-1)),
        lnx_g=row(lnx_g), lnx_b=row(lnx_b),
        w2p=_pad_rows(w2, LANES).astype(BF16), a2p=_pad_rows(a2, LANES).astype(BF16), g2=g2.astype(BF16),
        w_up_a=w_up_a.astype(BF16), w_up_b=w_up_b.astype(BF16), w_o=w_o.astype(BF16),
        g_mlp=row(g_mlp), w_ff1=w_ff1.astype(BF16), w_ff2=w_ff2.astype(BF16),
        g_ple=row(g_ple), w_ple_gate=w_ple_gate.astype(BF16), w_ple=w_ple.astype(BF16),
    )


def _run_layer(x, ple, s_gdn, buf_gdn, s_rwkv, shift_rwkv, pw, g_final):
    b, t, _ = x.shape
    m = b * t
    c = min(CHUNK, t)
    assert t % c == 0 and c % SUBLANES == 0 and t >= CONV_W - 1 and b % SEQS_PER_STEP == 0
    tm_proj = min(m, 2048)
    tm_stream = min(m, 1024)
    tm_res = min(m, 256)
    x2 = x.reshape(m, D_MODEL)

    few_rows = m < 2 * tm_stream
    p = norm_matmul(x2, pw["g_mix"], pw["w_proj_t"], tm_proj, N_PROJ // 4 if few_rows else 512)
    p3d = p.reshape(b, t, N_PROJ)

    hist8 = jnp.concatenate([jnp.zeros((b, SUBLANES - (CONV_W - 1), CONV_CH), F32), buf_gdn], axis=1)
    o_a, s_gdn_new = gdn_branch(p3d, hist8, pw["conv_w"], pw["alog_vec"], pw["dt_vec"],
                                pw["gdn_norm_g"], s_gdn, b, t, c, hb=GDN_HB, nseq=SEQS_PER_STEP)
    prev = _pad_xb(shift_rwkv).reshape(b, 1, XB_PAD)
    o_b, s_rwkv_new = rwkv_branch(p3d, prev, pw["mu"], pw["w0"], pw["a0"], pw["k_k"], pw["k_a"], pw["r_k"],
                                  pw["lnx_g"], pw["lnx_b"], pw["w2p"], pw["a2p"], pw["g2"], s_rwkv,
                                  b, t, c, pg=RWKV_PG, nseq=SEQS_PER_STEP)

    x1, h2 = merge_out(x2, o_a.reshape(m, W_A), o_b.reshape(m, W_B), p, P2_OFF // D_MODEL,
                       pw["w_up_a"], pw["w_up_b"], pw["w_o"], pw["g_mlp"], tm_res, 512)
    delta = mlp(h2, pw["w_ff1"], pw["w_ff2"], tm_stream, 2048 if few_rows else 1024, 512)
    y = ple_final(x1, delta, ple.reshape(m, D_PLE), pw["g_ple"], pw["w_ple_gate"], pw["w_ple"], g_final,
                  min(m, 512))

    new_buf = p3d[:, t - (CONV_W - 1):, P1_OFF:P1_OFF + CONV_CH]
    new_shift = _unpad_xb(p3d[:, t - 1, P3_OFF:])
    return y.reshape(b, t, D_MODEL), s_gdn_new, new_buf, s_rwkv_new, new_shift


def kernel(x_prompt, x_sample, p_prompt, p_sample, state_gdn, cache_gdn_conv, state_rwkv, cache_rwkv_shift, g_mix, w_in, conv_w, a_log, dt_bias, gdn_norm_g, mu_shift, w0, w2, a0, a2, g2, k_k, k_a, r_k, lnx_g, lnx_b, w_up_a, w_up_b, w_o, g_mlp, w_ff1, w_ff2, g_ple, w_ple_gate, w_ple, g_final):
    params = (g_mix, w_in, conv_w, a_log, dt_bias, gdn_norm_g, mu_shift, w0, w2, a0, a2, g2,
              k_k, k_a, r_k, lnx_g, lnx_b, w_up_a, w_up_b, w_o, g_mlp, w_ff1, w_ff2,
              g_ple, w_ple_gate, w_ple)
    depth = w_in.shape[0]
    assert depth == 1
    pw = _prepare(tuple(w[0] for w in params))
    gf = g_final.reshape(1, -1)
    bp = x_prompt.shape[0]
    z_sg = jnp.zeros((bp, H_A, DK_A, DV_A), F32)
    z_buf = jnp.zeros((bp, CONV_W - 1, CONV_CH), F32)
    z_sr = jnp.zeros((bp, H_B, N_B, N_B), F32)
    z_sh = jnp.zeros((bp, SHIFT_W), F32)
    y_p, sg_p, buf_p, sr_p, sh_p = _run_layer(x_prompt, p_prompt[0], z_sg, z_buf, z_sr, z_sh, pw, gf)
    y_s, sg_s, buf_s, sr_s, sh_s = _run_layer(x_sample, p_sample[0], state_gdn[0], cache_gdn_conv[0],
                                              state_rwkv[0], cache_rwkv_shift[0], pw, gf)
    st = lambda a: a[None]
    return (y_p, y_s, st(sg_p), st(buf_p), st(sr_p), st(sh_p), st(sg_s), st(buf_s), st(sr_s), st(sh_s))
```

```python
import functools

import jax
import jax.numpy as jnp
from jax import lax
from jax.experimental import pallas as pl
from jax.experimental.pallas import tpu as pltpu

F32 = jnp.float32
BF16 = jnp.bfloat16

D_MODEL = 2048
H_A, DK_A, DV_A = 8, 128, 128
W_A = H_A * DV_A
CONV_W = 4
CONV_CH = 3 * W_A
H_B, N_B = 16, 64
W_B = H_B * N_B
LORA_W, LORA_A, LORA_G = 96, 96, 256
SHIFT_W = 3 * W_B + LORA_W + LORA_A + LORA_G
D_FF = 4 * D_MODEL
D_PLE = 256
EPS = 1e-6
LNX_EPS = 64e-5
CHUNK = 64
GDN_HB = 8
RWKV_PG = 8
SEQS_PER_STEP = 4

LANES = 128
SUBLANES = 8
VMEM_LIMIT = 52 * 2 ** 20

XB_PAD = 3 * W_B + 2 * LANES + LORA_G
AB_BLOCK = 3 * W_B // LANES
ALPHA_LANE = LORA_W
BETA_LANE = LORA_W + H_A

P1_OFF, P2_OFF, P3_OFF = 0, 4 * W_A, 4 * W_A + 2 * D_MODEL
N_PROJ = P3_OFF + XB_PAD

NN = ((1,), (0,))
NT = ((1,), (1,))
TN = ((0,), (0,))


def _dg(a, b, dims):
    return lax.dot_general(a, b, (dims, ((), ())), preferred_element_type=F32)


def _dot1(a, b, dims=NN):
    return _dg(a.astype(BF16), b.astype(BF16), dims)


def _cumsum_rows(x, c):
    rows = x.shape[0]
    i = lax.broadcasted_iota(jnp.int32, (rows, rows), 0)
    j = lax.broadcasted_iota(jnp.int32, (rows, rows), 1)
    same_chunk = (i ^ j) < c
    tri = jnp.where((i >= j) & same_chunk, 1.0, 0.0).astype(BF16)
    h1 = x.astype(BF16)
    r1 = x - h1.astype(F32)
    h2 = r1.astype(BF16)
    h3 = (r1 - h2.astype(F32)).astype(BF16)
    return _dg(tri, h1, NN) + (_dg(tri, h2, NN) + _dg(tri, h3, NN))


def _tri_inv_many(a_list, c):
    i = lax.broadcasted_iota(jnp.int32, (c, c), 0)
    j = lax.broadcasted_iota(jnp.int32, (c, c), 1)
    ns = [-jnp.where((i ^ j) == 1, a, 0.0) for a in a_list]
    s, ls = 2, 1
    while s < c:
        mask = ((i ^ j) >> ls) == 1
        l_s = [jnp.where(mask, a, 0.0) for a in a_list]
        nl = [_dot1(n, l) for n, l in zip(ns, l_s)]
        ys = [l + x for l, x in zip(l_s, nl)]
        yn = [_dot1(y, n) for y, n in zip(ys, ns)]
        ns = [n - (y + z) for n, y, z in zip(ns, ys, yn)]
        s, ls = 2 * s, ls + 1
    eye = jnp.where(i == j, 1.0, 0.0)
    return [eye + n for n in ns]


def _pair_stack(x, c):
    n = x.shape[1]
    bi = lax.broadcasted_iota(jnp.int32, (2 * c, n), 0)
    bj = lax.broadcasted_iota(jnp.int32, (2 * c, n), 1)
    keep = (bi >= c) == (bj >= n // 2)
    return jnp.where(keep, jnp.concatenate([x, x], axis=0), 0.0)


def _tri_inv_packed(a_list, c):
    i = lax.broadcasted_iota(jnp.int32, (c, 2 * c), 0)
    jm = lax.broadcasted_iota(jnp.int32, (c, 2 * c), 1) & (c - 1)
    ns = [-jnp.where((i ^ jm) == 1, a, 0.0) for a in a_list]
    s, ls = 2, 1
    while s < c:
        mask = ((i ^ jm) >> ls) == 1
        l_s = [jnp.where(mask, a, 0.0) for a in a_list]
        nl = [_dot1(n, _pair_stack(l, c)) for n, l in zip(ns, l_s)]
        ys = [l + x for l, x in zip(l_s, nl)]
        yn = [_dot1(y, _pair_stack(n, c)) for y, n in zip(ys, ns)]
        ns = [n - (y + z) for n, y, z in zip(ns, ys, yn)]
        s, ls = 2 * s, ls + 1
    eye = jnp.where(i == jm, 1.0, 0.0)
    return [eye + n for n in ns]


def _softplus(x):
    return jnp.maximum(x, 0.0) + jnp.log1p(jnp.exp(-jnp.abs(x)))


def _sigmoid(x):
    return jax.nn.sigmoid(x)


def _rms(x, g):
    return x * lax.rsqrt(jnp.mean(x * x, axis=-1, keepdims=True) + EPS) * g


def _params(*sem):
    return pltpu.CompilerParams(dimension_semantics=sem, vmem_limit_bytes=VMEM_LIMIT)


def _norm_matmul_kernel(x_ref, g_ref, w_ref, o_ref, h_scr):
    @pl.when(pl.program_id(1) == 0)
    def _():
        h_scr[...] = _rms(x_ref[...], g_ref[...]).astype(BF16)

    o_ref[...] = _dg(h_scr[...], w_ref[...], NT).astype(o_ref.dtype)


def norm_matmul(x, g, wt, tm, tn, out_dtype=F32):
    m, k = x.shape
    n = wt.shape[0]
    return pl.pallas_call(
        _norm_matmul_kernel,
        grid=(m // tm, n // tn),
        in_specs=[pl.BlockSpec((tm, k), lambda i, j: (i, 0), pipeline_mode=pl.Buffered(1)),
                  pl.BlockSpec((1, k), lambda i, j: (0, 0)),
                  pl.BlockSpec((tn, k), lambda i, j: (j, 0))],
        out_specs=pl.BlockSpec((tm, tn), lambda i, j: (i, j)),
        out_shape=jax.ShapeDtypeStruct((m, n), out_dtype),
        scratch_shapes=[pltpu.VMEM((tm, k), BF16)],
        compiler_params=_params("parallel", "arbitrary"),
        name="norm_matmul",
    )(x, g, wt)


def _merge_kernel(x_ref, oa_ref, ob_ref, gma_ref, gmb_ref, wa_ref, wb_ref, wo_ref, g_ref, o_ref, h_ref, *, tn):
    oa = oa_ref[...]
    ob = ob_ref[...]
    acc = x_ref[...]
    for n in range(D_MODEL // tn):
        cs = slice(n * tn, (n + 1) * tn)
        merged = (_sigmoid(gma_ref[:, cs]) * _dg(oa, wa_ref[:, cs], NN)
                  + _sigmoid(gmb_ref[:, cs]) * _dg(ob, wb_ref[:, cs], NN))
        acc = acc + _dg(merged.astype(BF16), wo_ref[cs, :], NN)
    o_ref[...] = acc
    h_ref[...] = _rms(acc, g_ref[...]).astype(BF16)


def merge_out(x, o_a, o_b, p, gate_blk, w_up_a, w_up_b, w_o, g_mlp, tm, tn):
    m = x.shape[0]
    resident = lambda shape: pl.BlockSpec(shape, lambda i: (0, 0), pipeline_mode=pl.Buffered(1))
    return pl.pallas_call(
        functools.partial(_merge_kernel, tn=tn),
        grid=(m // tm,),
        in_specs=[pl.BlockSpec((tm, D_MODEL), lambda i: (i, 0)),
                  pl.BlockSpec((tm, W_A), lambda i: (i, 0)),
                  pl.BlockSpec((tm, W_B), lambda i: (i, 0)),
                  pl.BlockSpec((tm, D_MODEL), lambda i: (i, gate_blk)),
                  pl.BlockSpec((tm, D_MODEL), lambda i: (i, gate_blk + 1)),
                  resident((W_A, D_MODEL)), resident((W_B, D_MODEL)), resident((D_MODEL, D_MODEL)),
                  pl.BlockSpec((1, D_MODEL), lambda i: (0, 0))],
        out_specs=[pl.BlockSpec((tm, D_MODEL), lambda i: (i, 0)),
                   pl.BlockSpec((tm, D_MODEL), lambda i: (i, 0))],
        out_shape=[jax.ShapeDtypeStruct((m, D_MODEL), F32), jax.ShapeDtypeStruct((m, D_MODEL), BF16)],
        compiler_params=_params("parallel"),
        name="merge_out",
    )(x, o_a, o_b, p, p, w_up_a, w_up_b, w_o, g_mlp)


def _mlp_kernel(h_ref, w1_ref, w2_ref, o_ref, *, tc):
    f = pl.program_id(1)

    @pl.when(f == 0)
    def _():
        o_ref[...] = jnp.zeros_like(o_ref)

    h = h_ref[...]
    for fc in range(w1_ref.shape[1] // tc):
        fs = slice(fc * tc, (fc + 1) * tc)
        a = jnp.maximum(_dg(h, w1_ref[:, fs], NN), 0.0)
        a = (a * a).astype(BF16)
        for nc in range(D_MODEL // tc):
            ns = slice(nc * tc, (nc + 1) * tc)
            o_ref[:, ns] += _dg(a, w2_ref[fs, ns], NN)


def mlp(h, w1, w2, tm, tf, tc):
    m = h.shape[0]
    return pl.pallas_call(
        functools.partial(_mlp_kernel, tc=tc),
        grid=(m // tm, D_FF // tf),
        in_specs=[pl.BlockSpec((tm, D_MODEL), lambda i, f: (i, 0)),
                  pl.BlockSpec((D_MODEL, tf), lambda i, f: (0, f)),
                  pl.BlockSpec((tf, D_MODEL), lambda i, f: (f, 0))],
        out_specs=pl.BlockSpec((tm, D_MODEL), lambda i, f: (i, 0)),
        out_shape=jax.ShapeDtypeStruct((m, D_MODEL), F32),
        compiler_params=_params("parallel", "arbitrary"),
        name="mlp",
    )(h, w1, w2)


def _ple_final_kernel(x_ref, d_ref, p_ref, gp_ref, wg_ref, wp_ref, gf_ref, o_ref):
    x = x_ref[...] + d_ref[...]
    gate = _sigmoid(_dg(_rms(x, gp_ref[...]).astype(BF16), wg_ref[...], NN))
    x = x + gate * _dg(p_ref[...].astype(BF16), wp_ref[...], NN)
    o_ref[...] = _rms(x, gf_ref[...])


def ple_final(x, delta, ple, g_ple, w_gate, w_ple, g_final, tm):
    m = x.shape[0]
    resident = lambda shape: pl.BlockSpec(shape, lambda i: (0, 0), pipeline_mode=pl.Buffered(1))
    return pl.pallas_call(
        _ple_final_kernel,
        grid=(m // tm,),
        in_specs=[pl.BlockSpec((tm, D_MODEL), lambda i: (i, 0)),
                  pl.BlockSpec((tm, D_MODEL), lambda i: (i, 0)),
                  pl.BlockSpec((tm, D_PLE), lambda i: (i, 0)),
                  pl.BlockSpec((1, D_MODEL), lambda i: (0, 0)),
                  resident((D_MODEL, D_MODEL)), resident((D_PLE, D_MODEL)),
                  pl.BlockSpec((1, D_MODEL), lambda i: (0, 0))],
        out_specs=pl.BlockSpec((tm, D_MODEL), lambda i: (i, 0)),
        out_shape=jax.ShapeDtypeStruct((m, D_MODEL), F32),
        compiler_params=_params("parallel"),
        name="ple_final",
    )(x, delta, ple, g_ple, w_gate, w_ple, g_final)


def _gdn_kernel(q_ref, k_ref, v_ref, gate_ref, ab_ref, hq_ref, hk_ref, hv_ref,
                cq_ref, ck_ref, cv_ref, alog_ref, dt_ref, ng_ref, s0_ref,
                o_ref, sout_ref, qs, ks, vs, s_scr, *, c, hb, nseq):
    g_idx = pl.program_id(1)
    t_idx = pl.program_id(2)
    hist = SUBLANES
    seqs = range(nseq)

    @pl.when(t_idx == 0)
    def _():
        for s in seqs:
            for h in range(hb):
                s_scr[s * hb + h] = s0_ref[s, h]
            qs[s] = hq_ref[s]
            ks[s] = hk_ref[s]
            vs[s] = hv_ref[s]

    row8 = lax.broadcasted_iota(jnp.int32, (hist, hb * DK_A), 0)

    def conv_silu(scr, x_ref, w_ref):
        outs = []
        for s in seqs:
            u = x_ref[s]
            prev8 = scr[s]
            out = None
            for j in range(CONV_W - 1):
                sh = CONV_W - 1 - j
                rolled = pltpu.roll(u, sh, axis=0)
                top = jnp.where(row8 < sh, pltpu.roll(prev8, sh, axis=0), rolled[0:hist])
                term = jnp.concatenate([top, rolled[hist:]], axis=0) * w_ref[j:j + 1, :]
                out = term if out is None else out + term
            out = out + u * w_ref[CONV_W - 1:CONV_W, :]
            scr[s] = x_ref[s, c - hist:c, :]
            outs.append(out)
        out = jnp.concatenate(outs, axis=0) if nseq > 1 else outs[0]
        return out * _sigmoid(out)

    qc = conv_silu(qs, q_ref, cq_ref)
    kc = conv_silu(ks, k_ref, ck_ref)
    vc = conv_silu(vs, v_ref, cv_ref)

    ab = ab_ref[...].reshape(nseq * c, LANES)
    g_all = -jnp.exp(alog_ref[...]) * _softplus(ab + dt_ref[...])
    beta_all = _sigmoid(ab)
    gcum = _cumsum_rows(g_all, c)
    lane = lax.broadcasted_iota(jnp.int32, (c, LANES), 1)
    sub_t = lax.broadcasted_iota(jnp.int32, (LANES, c), 0)
    i = lax.broadcasted_iota(jnp.int32, (c, c), 0)
    j = lax.broadcasted_iota(jnp.int32, (c, c), 1)
    incl = i >= j
    strict = i > j
    last_row = lax.broadcasted_iota(jnp.int32, (c, 1), 0) == c - 1
    gate = gate_ref[...].reshape(nseq * c, hb * DV_A)
    ng = ng_ref[...]

    hs = range(hb)
    sls = [slice(h * DK_A, (h + 1) * DK_A) for h in hs]
    rs = [slice(s * c, (s + 1) * c) for s in seqs]
    heads = [g_idx * hb + h for h in hs]
    items = [(s, h) for s in seqs for h in hs]
    gcum_s = [gcum[r] for r in rs]
    gcum_t = [g.T for g in gcum_s]
    beta_s = [beta_all[r] for r in rs]
    qn = [qc[:, sl] for sl in sls]
    kn = [kc[:, sl] for sl in sls]
    qn = [x * lax.rsqrt(jnp.sum(x * x, axis=-1, keepdims=True) + EPS) * (DK_A ** -0.5) for x in qn]
    kn = [x * lax.rsqrt(jnp.sum(x * x, axis=-1, keepdims=True) + EPS) for x in kn]
    gc = {(s, h): jnp.sum(jnp.where(lane == ALPHA_LANE + heads[h], gcum_s[s], 0.0), axis=1, keepdims=True)
          for s, h in items}
    gr = {(s, h): jnp.sum(jnp.where(sub_t == ALPHA_LANE + heads[h], gcum_t[s], 0.0), axis=0, keepdims=True)
          for s, h in items}
    bc = {(s, h): jnp.sum(jnp.where(lane == BETA_LANE + heads[h], beta_s[s], 0.0), axis=1, keepdims=True)
          for s, h in items}
    qh = {(s, h): qn[h][rs[s]] for s, h in items}
    kh = {(s, h): kn[h][rs[s]] for s, h in items}
    vh = {(s, h): vc[rs[s], sls[h]] for s, h in items}
    dec = {n: jnp.exp(jnp.where(incl, gc[n] - gr[n], -jnp.inf)) for n in items}
    kb = {n: kh[n] * bc[n] for n in items}
    kk = {n: _dot1(kb[n], kh[n], NT) for n in items}
    t_inv = dict(zip(items, _tri_inv_many([jnp.where(strict, kk[n] * dec[n], 0.0) for n in items], c)))
    eg = {n: jnp.exp(gc[n]) for n in items}
    uw = {n: _dot1(t_inv[n], jnp.concatenate([vh[n] * bc[n], kb[n] * eg[n]], axis=1)) for n in items}
    qk = {n: jnp.where(incl, _dot1(qh[n], kh[n], NT) * dec[n], 0.0) for n in items}
    g_last = {n: jnp.sum(jnp.where(last_row, gc[n], 0.0), axis=0, keepdims=True) for n in items}
    kd = {n: kh[n] * jnp.exp(g_last[n] - gc[n]) for n in items}
    s_old = {(s, h): s_scr[s * hb + h] for s, h in items}
    ws_qs = {n: _dot1(jnp.concatenate([uw[n][:, DV_A:], qh[n] * eg[n]], axis=0), s_old[n]) for n in items}
    v_new = {n: uw[n][:, :DV_A] - ws_qs[n][:c] for n in items}
    o = {n: ws_qs[n][c:] + _dot1(qk[n], v_new[n]) for n in items}
    upd = {n: _dot1(kd[n], v_new[n], TN) for n in items}
    for s, h in items:
        s_scr[s * hb + h] = s_old[s, h] * jnp.exp(g_last[s, h]) + upd[s, h]
    for s, h in items:
        gh = gate[rs[s], sls[h]]
        o_ref[s, :, sls[h]] = (_rms(o[s, h], ng) * (gh * _sigmoid(gh))).astype(o_ref.dtype)

    @pl.when(t_idx == pl.num_programs(2) - 1)
    def _():
        for s in seqs:
            for h in range(hb):
                sout_ref[s, h] = s_scr[s * hb + h]


def gdn_branch(p, hist8, conv_w, alog_vec, dt_vec, norm_g, s0, b, t, c, hb, nseq):
    nc = t // c
    ng = H_A // hb
    gw = hb * DK_A
    seg = lambda k: pl.BlockSpec((nseq, c, gw), lambda bi, g, ti: (bi, ti, P1_OFF // gw + k * ng + g))
    hist = lambda k: pl.BlockSpec((nseq, SUBLANES, gw), lambda bi, g, ti: (bi, 0, k * ng + g))
    cw = lambda k: pl.BlockSpec((CONV_W, gw), lambda bi, g, ti: (0, k * ng + g))
    vec = pl.BlockSpec((1, LANES), lambda bi, g, ti: (0, 0))
    state = pl.BlockSpec((nseq, hb, DK_A, DV_A), lambda bi, g, ti: (bi, g, 0, 0))
    return pl.pallas_call(
        functools.partial(_gdn_kernel, c=c, hb=hb, nseq=nseq),
        grid=(b // nseq, ng, nc),
        in_specs=[seg(0), seg(1), seg(2), seg(3),
                  pl.BlockSpec((nseq, c, LANES), lambda bi, g, ti: (bi, ti, P3_OFF // LANES + AB_BLOCK)),
                  hist(0), hist(1), hist(2), cw(0), cw(1), cw(2), vec, vec, vec, state],
        out_specs=[pl.BlockSpec((nseq, c, gw), lambda bi, g, ti: (bi, ti, g)), state],
        out_shape=[jax.ShapeDtypeStruct((b, t, W_A), BF16),
                   jax.ShapeDtypeStruct((b, H_A, DK_A, DV_A), F32)],
        scratch_shapes=[pltpu.VMEM((nseq, SUBLANES, gw), F32)] * 3 + [pltpu.VMEM((nseq * hb, DK_A, DV_A), F32)],
        compiler_params=_params("parallel", "parallel", "arbitrary"),
        name="gdn_branch",
    )(p, p, p, p, p, hist8, hist8, hist8, conv_w, conv_w, conv_w, alog_vec, dt_vec, norm_g, s0)


def _rwkv_kernel(r_ref, k_ref, v_ref, tail_ref, pr_ref, pk_ref, pv_ref, pt_ref,
                 mr_ref, mk_ref, mv_ref, mt_ref, w0_ref, a0_ref, kk_ref, ka_ref, rk_ref,
                 lg_ref, lb_ref, w2_ref, a2_ref, g2_ref, s0_ref,
                 o_ref, sout_ref, prev_r, prev_k, prev_v, prev_t, s_scr, *, c, pg, nseq):
    t_idx = pl.program_id(2)
    seqs = range(nseq)
    rows = nseq * c

    @pl.when(t_idx == 0)
    def _():
        prev_r[...] = pr_ref[...]
        prev_k[...] = pk_ref[...]
        prev_v[...] = pv_ref[...]
        prev_t[...] = pt_ref[...]
        z = jnp.zeros((N_B, N_B), F32)
        for s in seqs:
            for p in range(pg):
                s_a = s0_ref[s, 2 * p]
                s_b = s0_ref[s, 2 * p + 1]
                s_scr[s * pg + p] = jnp.concatenate([jnp.concatenate([s_a, z], axis=1),
                                                     jnp.concatenate([z, s_b], axis=1)], axis=0)

    def mix(x_ref, prev, mu_ref):
        outs = []
        for s in seqs:
            x = x_ref[s]
            row = lax.broadcasted_iota(jnp.int32, x.shape, 0)
            shifted = jnp.where(row == 0, prev[s], pltpu.roll(x, 1, axis=0))
            prev[s] = x_ref[s, c - 1:c, :]
            outs.append(x + (shifted - x) * mu_ref[...])
        return jnp.concatenate(outs, axis=0) if nseq > 1 else outs[0]

    r = mix(r_ref, prev_r, mr_ref)
    k = mix(k_ref, prev_k, mk_ref)
    v = mix(v_ref, prev_v, mv_ref)
    tail = mix(tail_ref, prev_t, mt_ref)
    xw = tail[:, 0:LANES]
    xa = tail[:, LANES:2 * LANES]
    xg = tail[:, 2 * LANES:]

    w_log = -_softplus(-(w0_ref[...] + _dot1(jnp.tanh(xw), w2_ref[...]))) - 0.5
    lw = -jnp.exp(w_log)
    a = _sigmoid(a0_ref[...] + _dot1(xa, a2_ref[...]))
    gate = _dot1(_sigmoid(xg), g2_ref[...])

    gw = pg * LANES
    fp = (lax.broadcasted_iota(jnp.int32, (rows, LANES), 1) & N_B) == 0

    def seg_sum(x):
        outs = []
        for p in range(pg):
            xp = x[:, p * LANES:(p + 1) * LANES]
            s_a = jnp.sum(jnp.where(fp, xp, 0.0), axis=-1, keepdims=True)
            s_b = jnp.sum(jnp.where(fp, 0.0, xp), axis=-1, keepdims=True)
            outs.append(jnp.where(fp, s_a, s_b))
        return jnp.concatenate(outs, axis=1) if pg > 1 else outs[0]

    kkx = k * kk_ref[...]
    kk = kkx * lax.rsqrt(seg_sum(kkx * kkx) + EPS)
    kh = k * (1.0 + (a - 1.0) * ka_ref[...])
    bb = kk * a
    cw = _cumsum_rows(lw, c)
    rt = r * jnp.exp(cw)
    e_neg = jnp.exp(-cw)
    kt = kh * e_neg
    bt = bb * e_neg
    at = kk * jnp.exp(cw - lw)
    row_c = lax.broadcasted_iota(jnp.int32, (c, gw), 0)
    rs = [slice(s * c, (s + 1) * c) for s in seqs]
    cw_s = [cw[rc] for rc in rs]
    cw_last = [jnp.sum(jnp.where(row_c == c - 1, x, 0.0), axis=0, keepdims=True) for x in cw_s]
    e_last = [jnp.exp(cw_last[s] - cw_s[s]) for s in seqs]
    k_hat = [kh[rs[s]] * e_last[s] for s in seqs]
    b_hat = [bb[rs[s]] * e_last[s] for s in seqs]
    e_end = [jnp.exp(x) for x in cw_last]

    i = lax.broadcasted_iota(jnp.int32, (c, 2 * c), 0)
    jm = lax.broadcasted_iota(jnp.int32, (c, 2 * c), 1) & (c - 1)
    incl = i >= jm
    strict = i > jm
    bi = lax.broadcasted_iota(jnp.int32, (LANES, LANES), 0)
    bj = lax.broadcasted_iota(jnp.int32, (LANES, LANES), 1)
    same_head = ((bi ^ bj) & N_B) == 0

    ps = range(pg)
    sls = [slice(p * LANES, (p + 1) * LANES) for p in ps]
    items = [(s, p) for s in seqs for p in ps]
    at_p = {(s, p): at[rs[s], sls[p]] for s, p in items}
    rt_p = {(s, p): rt[rs[s], sls[p]] for s, p in items}
    v_p = {(s, p): v[rs[s], sls[p]] for s, p in items}
    s_old = {(s, p): s_scr[s * pg + p] for s, p in items}
    a_s = {n: _dot1(at_p[n], s_old[n], NT) for n in items}
    r_s = {n: _dot1(rt_p[n], s_old[n], NT) for n in items}
    lhs = {n: jnp.concatenate([at_p[n], rt_p[n]], axis=0) for n in items}
    xb_ = {(s, p): _dot1(lhs[s, p], _pair_stack(bt[rs[s], sls[p]], c), NT) for s, p in items}
    xk_ = {(s, p): _dot1(lhs[s, p], _pair_stack(kt[rs[s], sls[p]], c), NT) for s, p in items}
    l_ak = {n: jnp.where(strict, xk_[n][:c], 0.0) for n in items}
    r_b = {n: jnp.where(incl, xb_[n][c:], 0.0) for n in items}
    r_k = {n: jnp.where(incl, xk_[n][c:], 0.0) for n in items}
    t_inv = dict(zip(items, _tri_inv_packed([jnp.where(strict, xb_[n][:c], 0.0) for n in items], c)))
    v_st = {n: _pair_stack(v_p[n], c) for n in items}
    rhs = {n: a_s[n] + _dot1(l_ak[n], v_st[n]) for n in items}
    p_all = {n: _dot1(t_inv[n], _pair_stack(rhs[n], c)) for n in items}
    ys = {n: r_s[n] + _dot1(r_k[n], v_st[n]) - _dot1(r_b[n], _pair_stack(p_all[n], c)) for n in items}
    upd = {(s, p): _dot1(jnp.concatenate([v_p[s, p], p_all[s, p]], axis=0),
                         jnp.concatenate([k_hat[s][:, sls[p]], -b_hat[s][:, sls[p]]], axis=0), TN)
           for s, p in items}
    for s, p in items:
        s_scr[s * pg + p] = s_old[s, p] * e_end[s][:, sls[p]] + jnp.where(same_head, upd[s, p], 0.0)

    y = jnp.concatenate([jnp.concatenate([ys[s, p] for p in ps], axis=1) if pg > 1 else ys[s, 0]
                         for s in seqs], axis=0)
    inv_n = 1.0 / N_B
    yc = y - seg_sum(y) * inv_n
    yn = yc * lax.rsqrt(seg_sum(yc * yc) * inv_n + LNX_EPS)
    yn = yn * lg_ref[...] + lb_ref[...]
    bonus = seg_sum(r * kh * rk_ref[...]) * v
    o_ref[...] = ((yn + bonus) * gate).astype(o_ref.dtype).reshape(nseq, c, gw)

    @pl.when(t_idx == pl.num_programs(2) - 1)
    def _():
        for s in seqs:
            for p in range(pg):
                st = s_scr[s * pg + p]
                sout_ref[s, 2 * p] = st[:N_B, :N_B]
                sout_ref[s, 2 * p + 1] = st[N_B:, N_B:]


def rwkv_branch(p, prev, mu, w0, a0, k_k, k_a, r_k, lnx_g, lnx_b, w2p, a2p, g2, s0, b, t, c, pg, nseq):
    nc = t // c
    gw = pg * LANES
    ng = W_B // gw
    tw = XB_PAD - 3 * W_B
    tail_blk = 3 * W_B // tw
    seg = lambda k: pl.BlockSpec((nseq, c, gw), lambda bi, g, ti: (bi, ti, P3_OFF // gw + k * ng + g))
    pseg = lambda k: pl.BlockSpec((nseq, 1, gw), lambda bi, g, ti: (bi, 0, k * ng + g))
    mseg = lambda k: pl.BlockSpec((1, gw), lambda bi, g, ti: (0, k * ng + g))
    chan = pl.BlockSpec((1, gw), lambda bi, g, ti: (0, g))
    state = pl.BlockSpec((nseq, 2 * pg, N_B, N_B), lambda bi, g, ti: (bi, g, 0, 0))
    return pl.pallas_call(
        functools.partial(_rwkv_kernel, c=c, pg=pg, nseq=nseq),
        grid=(b // nseq, ng, nc),
        in_specs=[seg(0), seg(1), seg(2),
                  pl.BlockSpec((nseq, c, tw), lambda bi, g, ti: (bi, ti, P3_OFF // tw + tail_blk)),
                  pseg(0), pseg(1), pseg(2),
                  pl.BlockSpec((nseq, 1, tw), lambda bi, g, ti: (bi, 0, tail_blk)),
                  mseg(0), mseg(1), mseg(2),
                  pl.BlockSpec((1, tw), lambda bi, g, ti: (0, tail_blk)),
                  chan, chan, chan, chan, chan, chan, chan,
                  pl.BlockSpec((LANES, gw), lambda bi, g, ti: (0, g)),
                  pl.BlockSpec((LANES, gw), lambda bi, g, ti: (0, g)),
                  pl.BlockSpec((LORA_G, gw), lambda bi, g, ti: (0, g)),
                  state],
        out_specs=[pl.BlockSpec((nseq, c, gw), lambda bi, g, ti: (bi, ti, g)), state],
        out_shape=[jax.ShapeDtypeStruct((b, t, W_B), BF16),
                   jax.ShapeDtypeStruct((b, H_B, N_B, N_B), F32)],
        scratch_shapes=[pltpu.VMEM((nseq, 1, gw), F32)] * 3 + [pltpu.VMEM((nseq, 1, tw), F32),
                                                               pltpu.VMEM((nseq * pg, LANES, LANES), F32)],
        compiler_params=_params("parallel", "parallel", "arbitrary"),
        name="rwkv_branch",
    )(p, p, p, p, prev, prev, prev, prev, mu, mu, mu, mu,
      w0, a0, k_k, k_a, r_k, lnx_g, lnx_b, w2p, a2p, g2, s0)


def _pad_xb(a, ab=None):
    lead = a.shape[:-1]
    z = lambda n: jnp.zeros(lead + (n,), a.dtype)
    o_a, o_g = 3 * W_B + LORA_W, 3 * W_B + LORA_W + LORA_A
    mid = z(2 * H_A) if ab is None else ab
    return jnp.concatenate([a[..., :o_a], mid, z(LANES - LORA_W - 2 * H_A),
                            a[..., o_a:o_g], z(LANES - LORA_A), a[..., o_g:]], axis=-1)


def _unpad_xb(a):
    o_a = 3 * W_B + LORA_W
    return jnp.concatenate([a[..., :o_a], a[..., 3 * W_B + LANES:3 * W_B + LANES + LORA_A],
                            a[..., 3 * W_B + 2 * LANES:]], axis=-1)


def _pad_rows(w, n):
    return jnp.concatenate([w, jnp.zeros((n - w.shape[0],) + w.shape[1:], w.dtype)], axis=0)


def _lane_vec(v, start):
    return jnp.zeros((1, LANES), F32).at[0, start:start + v.shape[0]].set(v)


def _prepare(lw):
    (g_mix, w_in, conv_w, a_log, dt_bias, gdn_norm_g, mu_shift, w0, w2, a0, a2, g2,
     k_k, k_a, r_k, lnx_g, lnx_b, w_up_a, w_up_b, w_o, g_mlp, w_ff1, w_ff2,
     g_ple, w_ple_gate, w_ple) = lw
    o_alpha = CONV_CH
    o_gate_a = CONV_CH + 2 * H_A
    o_xb = o_gate_a + W_A
    o_gm = o_xb + SHIFT_W
    o_xa = o_xb + 3 * W_B + LORA_W
    o_xg = o_xa + LORA_A
    row = lambda v: v.reshape(1, -1)
    w_in_t = w_in.T
    zrows = lambda n: jnp.zeros((n, D_MODEL), F32)
    return dict(
        g_mix=row(g_mix),
        w_proj_t=jnp.concatenate([piece.astype(BF16) for piece in (
            w_in_t[:CONV_CH], w_in_t[o_gate_a:o_xb], w_in_t[o_gm:],
            w_in_t[o_xb:o_xa], w_in_t[o_alpha:o_gate_a], zrows(LANES - LORA_W - 2 * H_A),
            w_in_t[o_xa:o_xg], zrows(LANES - LORA_A), w_in_t[o_xg:o_gm])], axis=0),
        conv_w=conv_w,
        alog_vec=_lane_vec(a_log, ALPHA_LANE), dt_vec=_lane_vec(dt_bias, ALPHA_LANE),
        gdn_norm_g=row(gdn_norm_g),
        mu=row(_pad_xb(mu_shift)),
        w0=row(w0), a0=row(a0), k_k=row(k_k), k_a=row(k_a), r_k=row(r_k.reshape(-1)),
        lnx_g=row(lnx_g), lnx_b=row(lnx_b),
        w2p=_pad_rows(w2, LANES).astype(BF16), a2p=_pad_rows(a2, LANES).astype(BF16), g2=g2.astype(BF16),
        w_up_a=w_up_a.astype(BF16), w_up_b=w_up_b.astype(BF16), w_o=w_o.astype(BF16),
        g_mlp=row(g_mlp), w_ff1=w_ff1.astype(BF16), w_ff2=w_ff2.astype(BF16),
        g_ple=row(g_ple), w_ple_gate=w_ple_gate.astype(BF16), w_ple=w_ple.astype(BF16),
    )


def _run_layer(x, ple, s_gdn, buf_gdn, s_rwkv, shift_rwkv, pw, g_final):
    b, t, _ = x.shape
    m = b * t
    c = min(CHUNK, t)
    assert t % c == 0 and c % SUBLANES == 0 and t >= CONV_W - 1 and b % SEQS_PER_STEP == 0
    tm_proj = min(m, 2048)
    tm_stream = min(m, 1024)
    tm_res = min(m, 256)
    x2 = x.reshape(m, D_MODEL)

    few_rows = m < 2 * tm_stream
    p = norm_matmul(x2, pw["g_mix"], pw["w_proj_t"], tm_proj, N_PROJ // 4 if few_rows else 512)
    p3d = p.reshape(b, t, N_PROJ)

    hist8 = jnp.concatenate([jnp.zeros((b, SUBLANES - (CONV_W - 1), CONV_CH), F32), buf_gdn], axis=1)
    o_a, s_gdn_new = gdn_branch(p3d, hist8, pw["conv_w"], pw["alog_vec"], pw["dt_vec"],
                                pw["gdn_norm_g"], s_gdn, b, t, c, hb=GDN_HB, nseq=SEQS_PER_STEP)
    prev = _pad_xb(shift_rwkv).reshape(b, 1, XB_PAD)
    o_b, s_rwkv_new = rwkv_branch(p3d, prev, pw["mu"], pw["w0"], pw["a0"], pw["k_k"], pw["k_a"], pw["r_k"],
                                  pw["lnx_g"], pw["lnx_b"], pw["w2p"], pw["a2p"], pw["g2"], s_rwkv,
                                  b, t, c, pg=RWKV_PG, nseq=SEQS_PER_STEP)

    x1, h2 = merge_out(x2, o_a.reshape(m, W_A), o_b.reshape(m, W_B), p, P2_OFF // D_MODEL,
                       pw["w_up_a"], pw["w_up_b"], pw["w_o"], pw["g_mlp"], tm_res, 512)
    delta = mlp(h2, pw["w_ff1"], pw["w_ff2"], tm_stream, 2048 if few_rows else 1024, 512)
    y = ple_final(x1, delta, ple.reshape(m, D_PLE), pw["g_ple"], pw["w_ple_gate"], pw["w_ple"], g_final,
                  min(m, 512))

    new_buf = p3d[:, t - (CONV_W - 1):, P1_OFF:P1_OFF + CONV_CH]
    new_shift = _unpad_xb(p3d[:, t - 1, P3_OFF:])
    return y.reshape(b, t, D_MODEL), s_gdn_new, new_buf, s_rwkv_new, new_shift


def kernel(x_prompt, x_sample, p_prompt, p_sample, state_gdn, cache_gdn_conv, state_rwkv, cache_rwkv_shift, g_mix, w_in, conv_w, a_log, dt_bias, gdn_norm_g, mu_shift, w0, w2, a0, a2, g2, k_k, k_a, r_k, lnx_g, lnx_b, w_up_a, w_up_b, w_o, g_mlp, w_ff1, w_ff2, g_ple, w_ple_gate, w_ple, g_final):
    params = (g_mix, w_in, conv_w, a_log, dt_bias, gdn_norm_g, mu_shift, w0, w2, a0, a2, g2,
              k_k, k_a, r_k, lnx_g, lnx_b, w_up_a, w_up_b, w_o, g_mlp, w_ff1, w_ff2,
              g_ple, w_ple_gate, w_ple)
    depth = w_in.shape[0]
    assert depth == 1
    pw = _prepare(tuple(w[0] for w in params))
    gf = g_final.reshape(1, -1)
    bp = x_prompt.shape[0]
    z_sg = jnp.zeros((bp, H_A, DK_A, DV_A), F32)
    z_buf = jnp.zeros((bp, CONV_W - 1, CONV_CH), F32)
    z_sr = jnp.zeros((bp, H_B, N_B, N_B), F32)
    z_sh = jnp.zeros((bp, SHIFT_W), F32)
    y_p, sg_p, buf_p, sr_p, sh_p = _run_layer(x_prompt, p_prompt[0], z_sg, z_buf, z_sr, z_sh, pw, gf)
    y_s, sg_s, buf_s, sr_s, sh_s = _run_layer(x_sample, p_sample[0], state_gdn[0], cache_gdn_conv[0],
                                              state_rwkv[0], cache_rwkv_shift[0], pw, gf)
    st = lambda a: a[None]
    return (y_p, y_s, st(sg_p), st(buf_p), st(sr_p), st(sh_p), st(sg_s), st(buf_s), st(sr_s), st(sh_s))
```

```python
import functools

import jax
import jax.numpy as jnp
from jax import lax
from jax.experimental import pallas as pl
from jax.experimental.pallas import tpu as pltpu

F32 = jnp.float32
BF16 = jnp.bfloat16

D_MODEL = 2048
H_A, DK_A, DV_A = 8, 128, 128
W_A = H_A * DV_A
CONV_W = 4
CONV_CH = 3 * W_A
H_B, N_B = 16, 64
W_B = H_B * N_B
LORA_W, LORA_A, LORA_G = 96, 96, 256
SHIFT_W = 3 * W_B + LORA_W + LORA_A + LORA_G
D_FF = 4 * D_MODEL
D_PLE = 256
EPS = 1e-6
LNX_EPS = 64e-5
CHUNK = 64
GDN_HB = 8
RWKV_PG = 8
SEQS_PER_STEP = 4

LANES = 128
SUBLANES = 8
VMEM_LIMIT = 52 * 2 ** 20

XB_PAD = 3 * W_B + 2 * LANES + LORA_G
AB_BLOCK = 3 * W_B // LANES
ALPHA_LANE = LORA_W
BETA_LANE = LORA_W + H_A

P1_OFF, P2_OFF, P3_OFF = 0, 4 * W_A, 4 * W_A + 2 * D_MODEL
N_PROJ = P3_OFF + XB_PAD

NN = ((1,), (0,))
NT = ((1,), (1,))
TN = ((0,), (0,))


def _dg(a, b, dims):
    return lax.dot_general(a, b, (dims, ((), ())), preferred_element_type=F32)


def _dot1(a, b, dims=NN):
    return _dg(a.astype(BF16), b.astype(BF16), dims)


def _cumsum_rows(x, c):
    rows = x.shape[0]
    i = lax.broadcasted_iota(jnp.int32, (rows, rows), 0)
    j = lax.broadcasted_iota(jnp.int32, (rows, rows), 1)
    same_chunk = (i ^ j) < c
    tri = jnp.where((i >= j) & same_chunk, 1.0, 0.0).astype(BF16)
    h1 = x.astype(BF16)
    r1 = x - h1.astype(F32)
    h2 = r1.astype(BF16)
    h3 = (r1 - h2.astype(F32)).astype(BF16)
    return _dg(tri, h1, NN) + (_dg(tri, h2, NN) + _dg(tri, h3, NN))


def _pair_stack(x, c):
    n = x.shape[1]
    bi = lax.broadcasted_iota(jnp.int32, (2 * c, n), 0)
    bj = lax.broadcasted_iota(jnp.int32, (2 * c, n), 1)
    keep = (bi >= c) == (bj >= n // 2)
    return jnp.where(keep, jnp.concatenate([x, x], axis=0), 0.0)


def _tri_inv_packed(a_list, c):
    i = lax.broadcasted_iota(jnp.int32, (c, 2 * c), 0)
    jm = lax.broadcasted_iota(jnp.int32, (c, 2 * c), 1) & (c - 1)
    ns = [-jnp.where((i ^ jm) == 1, a, 0.0) for a in a_list]
    s, ls = 2, 1
    while s < c:
        mask = ((i ^ jm) >> ls) == 1
        l_s = [jnp.where(mask, a, 0.0) for a in a_list]
        nl = [_dot1(n, _pair_stack(l, c)) for n, l in zip(ns, l_s)]
        ys = [l + x for l, x in zip(l_s, nl)]
        yn = [_dot1(y, _pair_stack(n, c)) for y, n in zip(ys, ns)]
        ns = [n - (y + z) for n, y, z in zip(ns, ys, yn)]
        s, ls = 2 * s, ls + 1
    eye = jnp.where(i == jm, 1.0, 0.0)
    return [eye + n for n in ns]


def _softplus(x):
    return jnp.maximum(x, 0.0) + jnp.log1p(jnp.exp(-jnp.abs(x)))


def _sigmoid(x):
    return jax.nn.sigmoid(x)


def _rms(x, g):
    return x * lax.rsqrt(jnp.mean(x * x, axis=-1, keepdims=True) + EPS) * g


def _params(*sem):
    return pltpu.CompilerParams(dimension_semantics=sem, vmem_limit_bytes=VMEM_LIMIT)


def _norm_matmul_kernel(x_ref, g_ref, w_ref, o_ref, h_scr):
    @pl.when(pl.program_id(1) == 0)
    def _():
        h_scr[...] = _rms(x_ref[...], g_ref[...]).astype(BF16)

    o_ref[...] = _dg(h_scr[...], w_ref[...], NT).astype(o_ref.dtype)


def norm_matmul(x, g, wt, tm, tn, out_dtype=F32):
    m, k = x.shape
    n = wt.shape[0]
    return pl.pallas_call(
        _norm_matmul_kernel,
        grid=(m // tm, n // tn),
        in_specs=[pl.BlockSpec((tm, k), lambda i, j: (i, 0), pipeline_mode=pl.Buffered(1)),
                  pl.BlockSpec((1, k), lambda i, j: (0, 0)),
                  pl.BlockSpec((tn, k), lambda i, j: (j, 0))],
        out_specs=pl.BlockSpec((tm, tn), lambda i, j: (i, j)),
        out_shape=jax.ShapeDtypeStruct((m, n), out_dtype),
        scratch_shapes=[pltpu.VMEM((tm, k), BF16)],
        compiler_params=_params("parallel", "arbitrary"),
        name="norm_matmul",
    )(x, g, wt)


def _merge_kernel(x_ref, oa_ref, ob_ref, gma_ref, gmb_ref, wa_ref, wb_ref, wo_ref, g_ref, o_ref, h_ref, *, tn):
    oa = oa_ref[...]
    ob = ob_ref[...]
    acc = x_ref[...]
    for n in range(D_MODEL // tn):
        cs = slice(n * tn, (n + 1) * tn)
        merged = (_sigmoid(gma_ref[:, cs]) * _dg(oa, wa_ref[:, cs], NN)
                  + _sigmoid(gmb_ref[:, cs]) * _dg(ob, wb_ref[:, cs], NN))
        acc = acc + _dg(merged.astype(BF16), wo_ref[cs, :], NN)
    o_ref[...] = acc
    h_ref[...] = _rms(acc, g_ref[...]).astype(BF16)


def merge_out(x, o_a, o_b, p, gate_blk, w_up_a, w_up_b, w_o, g_mlp, tm, tn):
    m = x.shape[0]
    resident = lambda shape: pl.BlockSpec(shape, lambda i: (0, 0), pipeline_mode=pl.Buffered(1))
    return pl.pallas_call(
        functools.partial(_merge_kernel, tn=tn),
        grid=(m // tm,),
        in_specs=[pl.BlockSpec((tm, D_MODEL), lambda i: (i, 0)),
                  pl.BlockSpec((tm, W_A), lambda i: (i, 0)),
                  pl.BlockSpec((tm, W_B), lambda i: (i, 0)),
                  pl.BlockSpec((tm, D_MODEL), lambda i: (i, gate_blk)),
                  pl.BlockSpec((tm, D_MODEL), lambda i: (i, gate_blk + 1)),
                  resident((W_A, D_MODEL)), resident((W_B, D_MODEL)), resident((D_MODEL, D_MODEL)),
                  pl.BlockSpec((1, D_MODEL), lambda i: (0, 0))],
        out_specs=[pl.BlockSpec((tm, D_MODEL), lambda i: (i, 0)),
                   pl.BlockSpec((tm, D_MODEL), lambda i: (i, 0))],
        out_shape=[jax.ShapeDtypeStruct((m, D_MODEL), F32), jax.ShapeDtypeStruct((m, D_MODEL), BF16)],
        compiler_params=_params("parallel"),
        name="merge_out",
    )(x, o_a, o_b, p, p, w_up_a, w_up_b, w_o, g_mlp)


def _mlp_kernel(h_ref, w1_ref, w2_ref, o_ref, *, tc):
    f = pl.program_id(1)

    @pl.when(f == 0)
    def _():
        o_ref[...] = jnp.zeros_like(o_ref)

    h = h_ref[...]
    for fc in range(w1_ref.shape[1] // tc):
        fs = slice(fc * tc, (fc + 1) * tc)
        a = jnp.maximum(_dg(h, w1_ref[:, fs], NN), 0.0)
        a = (a * a).astype(BF16)
        for nc in range(D_MODEL // tc):
            ns = slice(nc * tc, (nc + 1) * tc)
            o_ref[:, ns] += _dg(a, w2_ref[fs, ns], NN)


def mlp(h, w1, w2, tm, tf, tc):
    m = h.shape[0]
    return pl.pallas_call(
        functools.partial(_mlp_kernel, tc=tc),
        grid=(m // tm, D_FF // tf),
        in_specs=[pl.BlockSpec((tm, D_MODEL), lambda i, f: (i, 0)),
                  pl.BlockSpec((D_MODEL, tf), lambda i, f: (0, f)),
                  pl.BlockSpec((tf, D_MODEL), lambda i, f: (f, 0))],
        out_specs=pl.BlockSpec((tm, D_MODEL), lambda i, f: (i, 0)),
        out_shape=jax.ShapeDtypeStruct((m, D_MODEL), F32),
        compiler_params=_params("parallel", "arbitrary"),
        name="mlp",
    )(h, w1, w2)


def _ple_final_kernel(x_ref, d_ref, p_ref, gp_ref, wg_ref, wp_ref, gf_ref, o_ref):
    x = x_ref[...] + d_ref[...]
    gate = _sigmoid(_dg(_rms(x, gp_ref[...]).astype(BF16), wg_ref[...], NN))
    x = x + gate * _dg(p_ref[...].astype(BF16), wp_ref[...], NN)
    o_ref[...] = _rms(x, gf_ref[...])


def ple_final(x, delta, ple, g_ple, w_gate, w_ple, g_final, tm):
    m = x.shape[0]
    resident = lambda shape: pl.BlockSpec(shape, lambda i: (0, 0), pipeline_mode=pl.Buffered(1))
    return pl.pallas_call(
        _ple_final_kernel,
        grid=(m // tm,),
        in_specs=[pl.BlockSpec((tm, D_MODEL), lambda i: (i, 0)),
                  pl.BlockSpec((tm, D_MODEL), lambda i: (i, 0)),
                  pl.BlockSpec((tm, D_PLE), lambda i: (i, 0)),
                  pl.BlockSpec((1, D_MODEL), lambda i: (0, 0)),
                  resident((D_MODEL, D_MODEL)), resident((D_PLE, D_MODEL)),
                  pl.BlockSpec((1, D_MODEL), lambda i: (0, 0))],
        out_specs=pl.BlockSpec((tm, D_MODEL), lambda i: (i, 0)),
        out_shape=jax.ShapeDtypeStruct((m, D_MODEL), F32),
        compiler_params=_params("parallel"),
        name="ple_final",
    )(x, delta, ple, g_ple, w_gate, w_ple, g_final)


def _gdn_kernel(q_ref, k_ref, v_ref, gate_ref, ab_ref, hq_ref, hk_ref, hv_ref,
                cq_ref, ck_ref, cv_ref, alog_ref, dt_ref, ng_ref, s0_ref,
                o_ref, sout_ref, qs, ks, vs, s_scr, *, c, hb, nseq):
    g_idx = pl.program_id(1)
    t_idx = pl.program_id(2)
    hist = SUBLANES
    seqs = range(nseq)

    @pl.when(t_idx == 0)
    def _():
        for s in seqs:
            for h in range(hb):
                s_scr[s * hb + h] = s0_ref[s, h]
            qs[s] = hq_ref[s]
            ks[s] = hk_ref[s]
            vs[s] = hv_ref[s]

    row8 = lax.broadcasted_iota(jnp.int32, (hist, hb * DK_A), 0)

    def conv_silu(scr, x_ref, w_ref):
        outs = []
        for s in seqs:
            u = x_ref[s]
            prev8 = scr[s]
            out = None
            for j in range(CONV_W - 1):
                sh = CONV_W - 1 - j
                rolled = pltpu.roll(u, sh, axis=0)
                top = jnp.where(row8 < sh, pltpu.roll(prev8, sh, axis=0), rolled[0:hist])
                term = jnp.concatenate([top, rolled[hist:]], axis=0) * w_ref[j:j + 1, :]
                out = term if out is None else out + term
            out = out + u * w_ref[CONV_W - 1:CONV_W, :]
            scr[s] = x_ref[s, c - hist:c, :]
            outs.append(out)
        out = jnp.concatenate(outs, axis=0) if nseq > 1 else outs[0]
        return out * _sigmoid(out)

    qc = conv_silu(qs, q_ref, cq_ref)
    kc = conv_silu(ks, k_ref, ck_ref)
    vc = conv_silu(vs, v_ref, cv_ref)

    ab = ab_ref[...].reshape(nseq * c, LANES)
    g_all = -jnp.exp(alog_ref[...]) * _softplus(ab + dt_ref[...])
    beta_all = _sigmoid(ab)
    gcum = _cumsum_rows(g_all, c)
    lane = lax.broadcasted_iota(jnp.int32, (c, LANES), 1)
    last_row =lax.broadcasted_iota(jnp.int32, (c, 1), 0) == c - 1
    gate = gate_ref[...].reshape(nseq * c, hb * DV_A)
    ng = ng_ref[...]

    hs = range(hb)
    sls = [slice(h * DK_A, (h + 1) * DK_A) for h in hs]
    rs = [slice(s * c, (s + 1) * c) for s in seqs]
    heads = [g_idx * hb + h for h in hs]
    items = [(s, h) for s in seqs for h in hs]
    gcum_s = [gcum[r] for r in rs]
    beta_s = [beta_all[r] for r in rs]
    qn = [qc[:, sl] for sl in sls]
    kn = [kc[:, sl] for sl in sls]
    qn = [x * lax.rsqrt(jnp.sum(x * x, axis=-1, keepdims=True) + EPS) * (DK_A ** -0.5) for x in qn]
    kn = [x * lax.rsqrt(jnp.sum(x * x, axis=-1, keepdims=True) + EPS) for x in kn]
    gc = {(s, h): jnp.sum(jnp.where(lane == ALPHA_LANE + heads[h], gcum_s[s], 0.0), axis=1, keepdims=True)
          for s, h in items}
    bc = {(s, h): jnp.sum(jnp.where(lane == BETA_LANE + heads[h], beta_s[s], 0.0), axis=1, keepdims=True)
          for s, h in items}
    qh = {(s, h): qn[h][rs[s]] for s, h in items}
    kh = {(s, h): kn[h][rs[s]] for s, h in items}
    vh = {(s, h): vc[rs[s], sls[h]] for s, h in items}
    pairs = [(s, pr) for s in seqs for pr in range(hb // 2)]
    i2 = lax.broadcasted_iota(jnp.int32, (c, 2 * c), 0)
    j2 = lax.broadcasted_iota(jnp.int32, (c, 2 * c), 1)
    first = j2 < c
    jm = j2 & (c - 1)
    incl_p = i2 >= jm
    strict_p = i2 > jm
    sub2 = lax.broadcasted_iota(jnp.int32, (LANES, 2 * c), 0)
    lane2 = lax.broadcasted_iota(jnp.int32, (LANES, 2 * c), 1)
    gst = [jnp.concatenate([g, g], axis=0).T for g in gcum_s]
    gcp = {(s, pr): jnp.where(first, gc[s, 2 * pr], gc[s, 2 * pr + 1]) for s, pr in pairs}
    grp = {(s, pr): jnp.sum(jnp.where(sub2 == ALPHA_LANE + jnp.where(lane2 < c, heads[2 * pr], heads[2 * pr + 1]),
                                      gst[s], 0.0), axis=0, keepdims=True) for s, pr in pairs}
    dec = {n: jnp.exp(jnp.where(incl_p, gcp[n] - grp[n], -jnp.inf)) for n in pairs}
    kb = {n: kh[n] * bc[n] for n in items}
    pair_of = lambda d, s, pr: jnp.concatenate([d[s, 2 * pr], d[s, 2 * pr + 1]], axis=1)
    k_st = {(s, pr): _pair_stack(pair_of(kh, s, pr), c) for s, pr in pairs}
    kk = {(s, pr): _dot1(pair_of(kb, s, pr), k_st[s, pr], NT) for s, pr in pairs}
    t_inv = dict(zip(pairs, _tri_inv_packed([jnp.where(strict_p, kk[n] * dec[n], 0.0) for n in pairs], c)))
    eg = {n: jnp.exp(gc[n]) for n in items}
    rhs = {n: jnp.concatenate([vh[n] * bc[n], kb[n] * eg[n]], axis=1) for n in items}
    uw_p = {(s, pr): _dot1(t_inv[s, pr], _pair_stack(pair_of(rhs, s, pr), c)) for s, pr in pairs}
    uw = {(s, h): uw_p[s, h // 2][:, (h % 2) * 2 * DV_A:(h % 2 + 1) * 2 * DV_A] for s, h in items}
    qk = {(s, pr): jnp.where(incl_p, _dot1(pair_of(qh, s, pr), k_st[s, pr], NT) * dec[s, pr], 0.0)
          for s, pr in pairs}
    g_last = {n: jnp.sum(jnp.where(last_row, gc[n], 0.0), axis=0, keepdims=True) for n in items}
    kd = {n: kh[n] * jnp.exp(g_last[n] - gc[n]) for n in items}
    s_old = {(s, h): s_scr[s * hb + h] for s, h in items}
    ws_qs = {n: _dot1(jnp.concatenate([uw[n][:, DV_A:], qh[n] * eg[n]], axis=0), s_old[n]) for n in items}
    v_new = {n: uw[n][:, :DV_A] - ws_qs[n][:c] for n in items}
    qkv = {(s, pr): _dot1(qk[s, pr], _pair_stack(pair_of(v_new, s, pr), c)) for s, pr in pairs}
    o = {(s, h): ws_qs[s, h][c:] + qkv[s, h // 2][:, (h % 2) * DV_A:(h % 2 + 1) * DV_A] for s, h in items}
    upd = {n: _dot1(kd[n], v_new[n], TN) for n in items}
    for s, h in items:
        s_scr[s * hb + h] = s_old[s, h] * jnp.exp(g_last[s, h]) + upd[s, h]
    for s, h in items:
        gh = gate[rs[s], sls[h]]
        o_ref[s, :, sls[h]] = (_rms(o[s, h], ng) * (gh * _sigmoid(gh))).astype(o_ref.dtype)

    @pl.when(t_idx == pl.num_programs(2) - 1)
    def _():
        for s in seqs:
            for h in range(hb):
                sout_ref[s, h] = s_scr[s * hb + h]


def gdn_branch(p, hist8, conv_w, alog_vec, dt_vec, norm_g, s0, b, t, c, hb, nseq):
    nc = t // c
    ng = H_A // hb
    gw = hb * DK_A
    seg = lambda k: pl.BlockSpec((nseq, c, gw), lambda bi, g, ti: (bi, ti, P1_OFF // gw + k * ng + g))
    hist = lambda k: pl.BlockSpec((nseq, SUBLANES, gw), lambda bi, g, ti: (bi, 0, k * ng + g))
    cw = lambda k: pl.BlockSpec((CONV_W, gw), lambda bi, g, ti: (0, k * ng + g))
    vec = pl.BlockSpec((1, LANES), lambda bi, g, ti: (0, 0))
    state = pl.BlockSpec((nseq, hb, DK_A, DV_A), lambda bi, g, ti: (bi, g, 0, 0))
    return pl.pallas_call(
        functools.partial(_gdn_kernel, c=c, hb=hb, nseq=nseq),
        grid=(b // nseq, ng, nc),
        in_specs=[seg(0), seg(1), seg(2), seg(3),
                  pl.BlockSpec((nseq, c, LANES), lambda bi, g, ti: (bi, ti, P3_OFF // LANES + AB_BLOCK)),
                  hist(0), hist(1), hist(2), cw(0), cw(1), cw(2), vec, vec, vec, state],
        out_specs=[pl.BlockSpec((nseq, c, gw), lambda bi, g, ti: (bi, ti, g)), state],
        out_shape=[jax.ShapeDtypeStruct((b, t, W_A), BF16),
                   jax.ShapeDtypeStruct((b, H_A, DK_A, DV_A), F32)],
        scratch_shapes=[pltpu.VMEM((nseq, SUBLANES, gw), F32)] * 3 + [pltpu.VMEM((nseq * hb, DK_A, DV_A), F32)],
        compiler_params=_params("parallel", "parallel", "arbitrary"),
        name="gdn_branch",
    )(p, p, p, p, p, hist8, hist8, hist8, conv_w, conv_w, conv_w, alog_vec, dt_vec, norm_g, s0)


def _rwkv_kernel(r_ref, k_ref, v_ref, tail_ref, pr_ref, pk_ref, pv_ref, pt_ref,
                 mr_ref, mk_ref, mv_ref, mt_ref, w0_ref, a0_ref, kk_ref, ka_ref, rk_ref,
                 lg_ref, lb_ref, w2_ref, a2_ref, g2_ref, s0_ref,
                 o_ref, sout_ref, prev_r, prev_k, prev_v, prev_t, s_scr, *, c, pg, nseq):
    t_idx = pl.program_id(2)
    seqs = range(nseq)
    rows = nseq * c

    @pl.when(t_idx == 0)
    def _():
        prev_r[...] = pr_ref[...]
        prev_k[...] = pk_ref[...]
        prev_v[...] = pv_ref[...]
        prev_t[...] = pt_ref[...]
        z = jnp.zeros((N_B, N_B), F32)
        for s in seqs:
            for p in range(pg):
                s_a = s0_ref[s, 2 * p]
                s_b = s0_ref[s, 2 * p + 1]
                s_scr[s * pg + p] = jnp.concatenate([jnp.concatenate([s_a, z], axis=1),
                                                     jnp.concatenate([z, s_b], axis=1)], axis=0)

    def mix(x_ref, prev, mu_ref):
        outs = []
        for s in seqs:
            x = x_ref[s]
            row = lax.broadcasted_iota(jnp.int32, x.shape, 0)
            shifted = jnp.where(row == 0, prev[s], pltpu.roll(x, 1, axis=0))
            prev[s] = x_ref[s, c - 1:c, :]
            outs.append(x + (shifted - x) * mu_ref[...])
        return jnp.concatenate(outs, axis=0) if nseq > 1 else outs[0]

    r = mix(r_ref, prev_r, mr_ref)
    k = mix(k_ref, prev_k, mk_ref)
    v = mix(v_ref, prev_v, mv_ref)
    tail = mix(tail_ref, prev_t, mt_ref)
    xw = tail[:, 0:LANES]
    xa = tail[:, LANES:2 * LANES]
    xg = tail[:, 2 * LANES:]

    w_log = -_softplus(-(w0_ref[...] + _dot1(jnp.tanh(xw), w2_ref[...]))) - 0.5
    lw = -jnp.exp(w_log)
    a = _sigmoid(a0_ref[...] + _dot1(xa, a2_ref[...]))
    gate = _dot1(_sigmoid(xg), g2_ref[...])

    gw = pg * LANES
    fp = (lax.broadcasted_iota(jnp.int32, (rows, LANES), 1) & N_B) == 0

    def seg_sum(x):
        outs = []
        for p in range(pg):
            xp = x[:, p * LANES:(p + 1) * LANES]
            s_a = jnp.sum(jnp.where(fp, xp, 0.0), axis=-1, keepdims=True)
            s_b = jnp.sum(jnp.where(fp, 0.0, xp), axis=-1, keepdims=True)
            outs.append(jnp.where(fp, s_a, s_b))
        return jnp.concatenate(outs, axis=1) if pg > 1 else outs[0]

    kkx = k * kk_ref[...]
    kk = kkx * lax.rsqrt(seg_sum(kkx * kkx) + EPS)
    kh = k * (1.0 + (a - 1.0) * ka_ref[...])
    bb = kk * a
    cw = _cumsum_rows(lw, c)
    rt = r * jnp.exp(cw)
    e_neg = jnp.exp(-cw)
    kt = kh * e_neg
    bt = bb * e_neg
    at = kk * jnp.exp(cw - lw)
    row_c = lax.broadcasted_iota(jnp.int32, (c, gw), 0)
    rs = [slice(s * c, (s + 1) * c) for s in seqs]
    cw_s = [cw[rc] for rc in rs]
    cw_last = [jnp.sum(jnp.where(row_c == c - 1, x, 0.0), axis=0, keepdims=True) for x in cw_s]
    e_last = [jnp.exp(cw_last[s] - cw_s[s]) for s in seqs]
    k_hat = [kh[rs[s]] * e_last[s] for s in seqs]
    b_hat = [bb[rs[s]] * e_last[s] for s in seqs]
    e_end = [jnp.exp(x) for x in cw_last]

    i = lax.broadcasted_iota(jnp.int32, (c, 2 * c), 0)
    jm = lax.broadcasted_iota(jnp.int32, (c, 2 * c), 1) & (c - 1)
    incl = i >= jm
    strict = i > jm
    bi = lax.broadcasted_iota(jnp.int32, (LANES, LANES), 0)
    bj = lax.broadcasted_iota(jnp.int32, (LANES, LANES), 1)
    same_head = ((bi ^ bj) & N_B) == 0

    ps = range(pg)
    sls = [slice(p * LANES, (p + 1) * LANES) for p in ps]
    items = [(s, p) for s in seqs for p in ps]
    at_p = {(s, p): at[rs[s], sls[p]] for s, p in items}
    rt_p = {(s, p): rt[rs[s], sls[p]] for s, p in items}
    v_p = {(s, p): v[rs[s], sls[p]] for s, p in items}
    s_old = {(s, p): s_scr[s * pg + p] for s, p in items}
    a_s = {n: _dot1(at_p[n], s_old[n], NT) for n in items}
    r_s = {n: _dot1(rt_p[n], s_old[n], NT) for n in items}
    lhs = {n: jnp.concatenate([at_p[n], rt_p[n]], axis=0) for n in items}
    xb_ = {(s, p): _dot1(lhs[s, p], _pair_stack(bt[rs[s], sls[p]], c), NT) for s, p in items}
    xk_ = {(s, p): _dot1(lhs[s, p], _pair_stack(kt[rs[s], sls[p]], c), NT) for s, p in items}
    l_ak = {n: jnp.where(strict, xk_[n][:c], 0.0) for n in items}
    r_b = {n: jnp.where(incl, xb_[n][c:], 0.0) for n in items}
    r_k = {n: jnp.where(incl, xk_[n][c:], 0.0) for n in items}
    t_inv = dict(zip(items, _tri_inv_packed([jnp.where(strict, xb_[n][:c], 0.0) for n in items], c)))
    v_st = {n: _pair_stack(v_p[n], c) for n in items}
    rhs = {n: a_s[n] + _dot1(l_ak[n], v_st[n]) for n in items}
    p_all = {n: _dot1(t_inv[n], _pair_stack(rhs[n], c)) for n in items}
    ys = {n: r_s[n] + _dot1(r_k[n], v_st[n]) - _dot1(r_b[n], _pair_stack(p_all[n], c)) for n in items}
    upd = {(s, p): _dot1(jnp.concatenate([v_p[s, p], p_all[s, p]], axis=0),
                         jnp.concatenate([k_hat[s][:, sls[p]], -b_hat[s][:, sls[p]]], axis=0), TN)
           for s, p in items}
    for s, p in items:
        s_scr[s * pg + p] = s_old[s, p] * e_end[s][:, sls[p]] + jnp.where(same_head, upd[s, p], 0.0)

    y = jnp.concatenate([jnp.concatenate([ys[s, p] for p in ps], axis=1) if pg > 1 else ys[s, 0]
                         for s in seqs], axis=0)
    inv_n = 1.0 / N_B
    yc = y - seg_sum(y) * inv_n
    yn = yc * lax.rsqrt(seg_sum(yc * yc) * inv_n + LNX_EPS)
    yn = yn * lg_ref[...] + lb_ref[...]
    bonus = seg_sum(r * kh * rk_ref[...]) * v
    o_ref[...] = ((yn + bonus) * gate).astype(o_ref.dtype).reshape(nseq, c, gw)

    @pl.when(t_idx == pl.num_programs(2) - 1)
    def _():
        for s in seqs:
            for p in range(pg):
                st = s_scr[s * pg + p]
                sout_ref[s, 2 * p] = st[:N_B, :N_B]
                sout_ref[s, 2 * p + 1] = st[N_B:, N_B:]


def rwkv_branch(p, prev, mu, w0, a0, k_k, k_a, r_k, lnx_g, lnx_b, w2p, a2p, g2, s0, b, t, c, pg, nseq):
    nc = t // c
    gw = pg * LANES
    ng = W_B // gw
    tw = XB_PAD - 3 * W_B
    tail_blk = 3 * W_B // tw
    seg = lambda k: pl.BlockSpec((nseq, c, gw), lambda bi, g, ti: (bi, ti, P3_OFF // gw + k * ng + g))
    pseg = lambda k: pl.BlockSpec((nseq, 1, gw), lambda bi, g, ti: (bi, 0, k * ng + g))
    mseg = lambda k: pl.BlockSpec((1, gw), lambda bi, g, ti: (0, k * ng + g))
    chan = pl.BlockSpec((1, gw), lambda bi, g, ti: (0, g))
    state = pl.BlockSpec((nseq, 2 * pg, N_B, N_B), lambda bi, g, ti: (bi, g, 0, 0))
    return pl.pallas_call(
        functools.partial(_rwkv_kernel, c=c, pg=pg, nseq=nseq),
        grid=(b // nseq, ng, nc),
        in_specs=[seg(0), seg(1), seg(2),
                  pl.BlockSpec((nseq, c, tw), lambda bi, g, ti: (bi, ti, P3_OFF // tw + tail_blk)),
                  pseg(0), pseg(1), pseg(2),
                  pl.BlockSpec((nseq, 1, tw), lambda bi, g, ti: (bi, 0, tail_blk)),
                  mseg(0), mseg(1), mseg(2),
                  pl.BlockSpec((1, tw), lambda bi, g, ti: (0, tail_blk)),
                  chan, chan, chan, chan, chan, chan, chan,
                  pl.BlockSpec((LANES, gw), lambda bi, g, ti: (0, g)),
                  pl.BlockSpec((LANES, gw), lambda bi, g, ti: (0, g)),
                  pl.BlockSpec((LORA_G, gw), lambda bi, g, ti: (0, g)),
                  state],
        out_specs=[pl.BlockSpec((nseq, c, gw), lambda bi, g, ti: (bi, ti, g)), state],
        out_shape=[jax.ShapeDtypeStruct((b, t, W_B), BF16),
                   jax.ShapeDtypeStruct((b, H_B, N_B, N_B), F32)],
        scratch_shapes=[pltpu.VMEM((nseq, 1, gw), F32)] * 3 + [pltpu.VMEM((nseq, 1, tw), F32),
                                                               pltpu.VMEM((nseq * pg, LANES, LANES), F32)],
        compiler_params=_params("parallel", "parallel", "arbitrary"),
        name="rwkv_branch",
    )(p, p, p, p, prev, prev, prev, prev, mu, mu, mu, mu,
      w0, a0, k_k, k_a, r_k, lnx_g, lnx_b, w2p, a2p, g2, s0)


def _pad_xb(a, ab=None):
    lead = a.shape[:-1]
    z = lambda n: jnp.zeros(lead + (n,), a.dtype)
    o_a, o_g = 3 * W_B + LORA_W, 3 * W_B + LORA_W + LORA_A
    mid = z(2 * H_A) if ab is None else ab
    return jnp.concatenate([a[..., :o_a], mid, z(LANES - LORA_W - 2 * H_A),
                            a[..., o_a:o_g], z(LANES - LORA_A), a[..., o_g:]], axis=-1)


def _unpad_xb(a):
    o_a = 3 * W_B + LORA_W
    return jnp.concatenate([a[..., :o_a], a[..., 3 * W_B + LANES:3 * W_B + LANES + LORA_A],
                            a[..., 3 * W_B + 2 * LANES:]], axis=-1)


def _pad_rows(w, n):
    return jnp.concatenate([w, jnp.zeros((n - w.shape[0],) + w.shape[1:], w.dtype)], axis=0)


def _lane_vec(v, start):
    return jnp.zeros((1, LANES), F32).at[0, start:start + v.shape[0]].set(v)


def _prepare(lw):
    (g_mix, w_in, conv_w, a_log, dt_bias, gdn_norm_g, mu_shift, w0, w2, a0, a2, g2,
     k_k, k_a, r_k, lnx_g, lnx_b, w_up_a, w_up_b, w_o, g_mlp, w_ff1, w_ff2,
     g_ple, w_ple_gate, w_ple) = lw
    o_alpha = CONV_CH
    o_gate_a = CONV_CH + 2 * H_A
    o_xb = o_gate_a + W_A
    o_gm = o_xb + SHIFT_W
    o_xa = o_xb + 3 * W_B + LORA_W
    o_xg = o_xa + LORA_A
    row = lambda v: v.reshape(1, -1)
    w_in_t = w_in.T
    zrows = lambda n: jnp.zeros((n, D_MODEL), F32)
    return dict(
        g_mix=row(g_mix),
        w_proj_t=jnp.concatenate([
            w_in_t[:CONV_CH], w_in_t[o_gate_a:o_xb], w_in_t[o_gm:],
            w_in_t[o_xb:o_xa], w_in_t[o_alpha:o_gate_a], zrows(LANES - LORA_W - 2 * H_A),
            w_in_t[o_xa:o_xg], zrows(LANES - LORA_A), w_in_t[o_xg:o_gm]], axis=0).astype(BF16),
        conv_w=conv_w,
        alog_vec=_lane_vec(a_log, ALPHA_LANE), dt_vec=_lane_vec(dt_bias, ALPHA_LANE),
        gdn_norm_g=row(gdn_norm_g),
        mu=row(_pad_xb(mu_shift)),
        w0=row(w0), a0=row(a0), k_k=row(k_k), k_a=row(k_a), r_k=row(r_k.reshape(-1)),
        lnx_g=row(lnx_g), lnx_b=row(lnx_b),
        w2p=_pad_rows(w2, LANES).astype(BF16), a2p=_pad_rows(a2, LANES).astype(BF16), g2=g2.astype(BF16),
        w_up_a=w_up_a.astype(BF16), w_up_b=w_up_b.astype(BF16), w_o=w_o.astype(BF16),
        g_mlp=row(g_mlp), w_ff1=w_ff1.astype(BF16), w_ff2=w_ff2.astype(BF16),
        g_ple=row(g_ple), w_ple_gate=w_ple_gate.astype(BF16), w_ple=w_ple.astype(BF16),
    )


def _run_layer(x, ple, s_gdn, buf_gdn, s_rwkv, shift_rwkv, pw, g_final):
    b, t, _ = x.shape
    m = b * t
    c = min(CHUNK, t)
    assert t % c == 0 and c % SUBLANES == 0 and t >= CONV_W - 1 and b % SEQS_PER_STEP == 0
    tm_proj = min(m, 2048)
    tm_stream = min(m, 1024)
    tm_res = min(m, 256)
    x2 = x.reshape(m, D_MODEL)

    few_rows = m < 2 * tm_stream
    p = norm_matmul(x2, pw["g_mix"], pw["w_proj_t"], tm_proj, N_PROJ // 4 if few_rows else 512)
    p3d = p.reshape(b, t, N_PROJ)

    hist8 = jnp.concatenate([jnp.zeros((b, SUBLANES - (CONV_W - 1), CONV_CH), F32), buf_gdn], axis=1)
    o_a, s_gdn_new = gdn_branch(p3d, hist8, pw["conv_w"], pw["alog_vec"], pw["dt_vec"],
                                pw["gdn_norm_g"], s_gdn, b, t, c, hb=GDN_HB, nseq=SEQS_PER_STEP)
    prev = _pad_xb(shift_rwkv).reshape(b, 1, XB_PAD)
    o_b, s_rwkv_new = rwkv_branch(p3d, prev, pw["mu"], pw["w0"], pw["a0"], pw["k_k"], pw["k_a"], pw["r_k"],
                                  pw["lnx_g"], pw["lnx_b"], pw["w2p"], pw["a2p"], pw["g2"], s_rwkv,
                                  b, t, c, pg=RWKV_PG, nseq=SEQS_PER_STEP)

    x1, h2 = merge_out(x2, o_a.reshape(m, W_A), o_b.reshape(m, W_B), p, P2_OFF // D_MODEL,
                       pw["w_up_a"], pw["w_up_b"], pw["w_o"], pw["g_mlp"], tm_res, 512)
    delta = mlp(h2, pw["w_ff1"], pw["w_ff2"], tm_stream, 2048 if few_rows else 1024, 512)
    y = ple_final(x1, delta, ple.reshape(m, D_PLE), pw["g_ple"], pw["w_ple_gate"], pw["w_ple"], g_final,
                  min(m, 512))

    new_buf = p3d[:, t - (CONV_W - 1):, P1_OFF:P1_OFF + CONV_CH]
    new_shift = _unpad_xb(p3d[:, t - 1, P3_OFF:])
    return y.reshape(b, t, D_MODEL), s_gdn_new, new_buf, s_rwkv_new, new_shift


def kernel(x_prompt, x_sample, p_prompt, p_sample, state_gdn, cache_gdn_conv, state_rwkv, cache_rwkv_shift, g_mix, w_in, conv_w, a_log, dt_bias, gdn_norm_g, mu_shift, w0, w2, a0, a2, g2, k_k, k_a, r_k, lnx_g, lnx_b, w_up_a, w_up_b, w_o, g_mlp, w_ff1, w_ff2, g_ple, w_ple_gate, w_ple, g_final):
    params = (g_mix, w_in, conv_w, a_log, dt_bias, gdn_norm_g, mu_shift, w0, w2, a0, a2, g2,
              k_k, k_a, r_k, lnx_g, lnx_b, w_up_a, w_up_b, w_o, g_mlp, w_ff1, w_ff2,
              g_ple, w_ple_gate, w_ple)
    depth = w_in.shape[0]
    assert depth == 1
    pw = _prepare(tuple(w[0] for w in params))
    gf = g_final.reshape(1, -1)
    bp = x_prompt.shape[0]
    z_sg = jnp.zeros((bp, H_A, DK_A, DV_A), F32)
    z_buf = jnp.zeros((bp, CONV_W - 1, CONV_CH), F32)
    z_sr = jnp.zeros((bp, H_B, N_B, N_B), F32)
    z_sh = jnp.zeros((bp, SHIFT_W), F32)
    y_p, sg_p, buf_p, sr_p, sh_p = _run_layer(x_prompt, p_prompt[0], z_sg, z_buf, z_sr, z_sh, pw, gf)
    y_s, sg_s, buf_s, sr_s, sh_s = _run_layer(x_sample, p_sample[0], state_gdn[0], cache_gdn_conv[0],
                                              state_rwkv[0], cache_rwkv_shift[0], pw, gf)
    st = lambda a: a[None]
    return (y_p, y_s, st(sg_p), st(buf_p), st(sr_p), st(sh_p), st(sg_s), st(buf_s), st(sr_s), st(sh_s))
```

```python
import functools

import jax
import jax.numpy as jnp
from jax import lax
from jax.experimental import pallas as pl
from jax.experimental.pallas import tpu as pltpu

F32 = jnp.float32
BF16 = jnp.bfloat16

D_MODEL = 2048
H_A, DK_A, DV_A = 8, 128, 128
W_A = H_A * DV_A
CONV_W = 4
CONV_CH = 3 * W_A
H_B, N_B = 16, 64
W_B = H_B * N_B
LORA_W, LORA_A, LORA_G = 96, 96, 256
SHIFT_W = 3 * W_B + LORA_W + LORA_A + LORA_G
D_FF = 4 * D_MODEL
D_PLE = 256
EPS = 1e-6
LNX_EPS = 64e-5
CHUNK = 64
GDN_HB = 8
RWKV_PG = 8
SEQS_PER_STEP = 4

LANES = 128
SUBLANES = 8
VMEM_LIMIT = 52 * 2 ** 20
VMEM_LIMIT_PROJ = 58 * 2 ** 20

XB_PAD = 3 * W_B + 2 * LANES + LORA_G
AB_BLOCK = 3 * W_B // LANES
ALPHA_LANE = LORA_W
BETA_LANE = LORA_W + H_A

P1_OFF, P2_OFF, P3_OFF = 0, 4 * W_A, 4 * W_A + 2 * D_MODEL
N_PROJ = P3_OFF + XB_PAD

NN = ((1,), (0,))
NT = ((1,), (1,))
TN = ((0,), (0,))


def _dg(a, b, dims):
    return lax.dot_general(a, b, (dims, ((), ())), preferred_element_type=F32)


def _dot1(a, b, dims=NN):
    return _dg(a.astype(BF16), b.astype(BF16), dims)


def _cumsum_rows(x, c):
    rows = x.shape[0]
    i = lax.broadcasted_iota(jnp.int32, (rows, rows), 0)
    j = lax.broadcasted_iota(jnp.int32, (rows, rows), 1)
    same_chunk = (i ^ j) < c
    tri = jnp.where((i >= j) & same_chunk, 1.0, 0.0).astype(BF16)
    h1 = x.astype(BF16)
    r1 = x - h1.astype(F32)
    h2 = r1.astype(BF16)
    h3 = (r1 - h2.astype(F32)).astype(BF16)
    return _dg(tri, h1, NN) + (_dg(tri, h2, NN) + _dg(tri, h3, NN))


def _pair_stack(x, c):
    n = x.shape[1]
    bi = lax.broadcasted_iota(jnp.int32, (2 * c, n), 0)
    bj = lax.broadcasted_iota(jnp.int32, (2 * c, n), 1)
    keep = (bi >= c) == (bj >= n // 2)
    return jnp.where(keep, jnp.concatenate([x, x], axis=0), 0.0)


def _tri_inv_packed(a_list, c):
    i = lax.broadcasted_iota(jnp.int32, (c, 2 * c), 0)
    jm = lax.broadcasted_iota(jnp.int32, (c, 2 * c), 1) & (c - 1)
    ns = [-jnp.where((i ^ jm) == 1, a, 0.0) for a in a_list]
    s, ls = 2, 1
    while s < c:
        mask = ((i ^ jm) >> ls) == 1
        l_s = [jnp.where(mask, a, 0.0) for a in a_list]
        nl = [_dot1(n, _pair_stack(l, c)) for n, l in zip(ns, l_s)]
        ys = [l + x for l, x in zip(l_s, nl)]
        yn = [_dot1(y, _pair_stack(n, c)) for y, n in zip(ys, ns)]
        ns = [n - (y + z) for n, y, z in zip(ns, ys, yn)]
        s, ls = 2 * s, ls + 1
    eye = jnp.where(i == jm, 1.0, 0.0)
    return [eye + n for n in ns]


def _softplus(x):
    return jnp.maximum(x, 0.0) + jnp.log1p(jnp.exp(-jnp.abs(x)))


def _sigmoid(x):
    return jax.nn.sigmoid(x)


def _rms(x, g):
    return x * lax.rsqrt(jnp.mean(x * x, axis=-1, keepdims=True) + EPS) * g


def _params(*sem, vmem=VMEM_LIMIT):
    return pltpu.CompilerParams(dimension_semantics=sem, vmem_limit_bytes=vmem)


def _norm_matmul_kernel(x_ref, g_ref, w_ref, o_ref, h_scr):
    @pl.when(pl.program_id(1) == 0)
    def _():
        h_scr[...] = _rms(x_ref[...], g_ref[...]).astype(BF16)

    o_ref[...] = _dg(h_scr[...], w_ref[...], NT).astype(o_ref.dtype)


def norm_matmul(x, g, wt, tm, tn, out_dtype=F32):
    m, k = x.shape
    n = wt.shape[0]
    return pl.pallas_call(
        _norm_matmul_kernel,
        grid=(m // tm, n // tn),
        in_specs=[pl.BlockSpec((tm, k), lambda i, j: (i, 0)),
                  pl.BlockSpec((1, k), lambda i, j: (0, 0)),
                  pl.BlockSpec((tn, k), lambda i, j: (j, 0))],
        out_specs=pl.BlockSpec((tm, tn), lambda i, j: (i, j)),
        out_shape=jax.ShapeDtypeStruct((m, n), out_dtype),
        scratch_shapes=[pltpu.VMEM((tm, k), BF16)],
        compiler_params=_params("parallel", "arbitrary", vmem=VMEM_LIMIT_PROJ),
        name="norm_matmul",
    )(x, g, wt)


def _merge_kernel(x_ref, oa_ref, ob_ref, gma_ref, gmb_ref, wa_ref, wb_ref, wo_ref, g_ref, o_ref, h_ref, *, tn):
    oa = oa_ref[...]
    ob = ob_ref[...]
    acc = x_ref[...]
    for n in range(D_MODEL // tn):
        cs = slice(n * tn, (n + 1) * tn)
        merged = (_sigmoid(gma_ref[:, cs]) * _dg(oa, wa_ref[:, cs], NN)
                  + _sigmoid(gmb_ref[:, cs]) * _dg(ob, wb_ref[:, cs], NN))
        acc = acc + _dg(merged.astype(BF16), wo_ref[cs, :], NN)
    o_ref[...] = acc
    h_ref[...] = _rms(acc, g_ref[...]).astype(BF16)


def merge_out(x, o_a, o_b, p, gate_blk, w_up_a, w_up_b, w_o, g_mlp, tm, tn):
    m = x.shape[0]
    resident = lambda shape: pl.BlockSpec(shape, lambda i: (0, 0), pipeline_mode=pl.Buffered(1))
    return pl.pallas_call(
        functools.partial(_merge_kernel, tn=tn),
        grid=(m // tm,),
        in_specs=[pl.BlockSpec((tm, D_MODEL), lambda i: (i, 0)),
                  pl.BlockSpec((tm, W_A), lambda i: (i, 0)),
                  pl.BlockSpec((tm, W_B), lambda i: (i, 0)),
                  pl.BlockSpec((tm, D_MODEL), lambda i: (i, gate_blk)),
                  pl.BlockSpec((tm, D_MODEL), lambda i: (i, gate_blk + 1)),
                  resident((W_A, D_MODEL)), resident((W_B, D_MODEL)), resident((D_MODEL, D_MODEL)),
                  pl.BlockSpec((1, D_MODEL), lambda i: (0, 0))],
        out_specs=[pl.BlockSpec((tm, D_MODEL), lambda i: (i, 0)),
                   pl.BlockSpec((tm, D_MODEL), lambda i: (i, 0))],
        out_shape=[jax.ShapeDtypeStruct((m, D_MODEL), F32), jax.ShapeDtypeStruct((m, D_MODEL), BF16)],
        compiler_params=_params("parallel"),
        name="merge_out",
    )(x, o_a, o_b, p, p, w_up_a, w_up_b, w_o, g_mlp)


def _mlp_kernel(h_ref, w1_ref, w2_ref, o_ref, *, tc):
    f = pl.program_id(1)

    @pl.when(f == 0)
    def _():
        o_ref[...] = jnp.zeros_like(o_ref)

    h = h_ref[...]
    for fc in range(w1_ref.shape[1] // tc):
        fs = slice(fc * tc, (fc + 1) * tc)
        a = jnp.maximum(_dg(h, w1_ref[:, fs], NN), 0.0)
        a = (a * a).astype(BF16)
        for nc in range(D_MODEL // tc):
            ns = slice(nc * tc, (nc + 1) * tc)
            o_ref[:, ns] += _dg(a, w2_ref[fs, ns], NN)


def mlp(h, w1, w2, tm, tf, tc):
    m = h.shape[0]
    return pl.pallas_call(
        functools.partial(_mlp_kernel, tc=tc),
        grid=(m // tm, D_FF // tf),
        in_specs=[pl.BlockSpec((tm, D_MODEL), lambda i, f: (i, 0)),
                  pl.BlockSpec((D_MODEL, tf), lambda i, f: (0, f)),
                  pl.BlockSpec((tf, D_MODEL), lambda i, f: (f, 0))],
        out_specs=pl.BlockSpec((tm, D_MODEL), lambda i, f: (i, 0)),
        out_shape=jax.ShapeDtypeStruct((m, D_MODEL), F32),
        compiler_params=_params("parallel", "arbitrary"),
        name="mlp",
    )(h, w1, w2)


def _ple_final_kernel(x_ref, d_ref, p_ref, gp_ref, wg_ref, wp_ref, gf_ref, o_ref):
    x = x_ref[...] + d_ref[...]
    gate = _sigmoid(_dg(_rms(x, gp_ref[...]).astype(BF16), wg_ref[...], NN))
    x = x + gate * _dg(p_ref[...].astype(BF16), wp_ref[...], NN)
    o_ref[...] = _rms(x, gf_ref[...])


def ple_final(x, delta, ple, g_ple, w_gate, w_ple, g_final, tm):
    m = x.shape[0]
    resident = lambda shape: pl.BlockSpec(shape, lambda i: (0, 0), pipeline_mode=pl.Buffered(1))
    return pl.pallas_call(
        _ple_final_kernel,
        grid=(m // tm,),
        in_specs=[pl.BlockSpec((tm, D_MODEL), lambda i: (i, 0)),
                  pl.BlockSpec((tm, D_MODEL), lambda i: (i, 0)),
                  pl.BlockSpec((tm, D_PLE), lambda i: (i, 0)),
                  pl.BlockSpec((1, D_MODEL), lambda i: (0, 0)),
                  resident((D_MODEL, D_MODEL)), resident((D_PLE, D_MODEL)),
                  pl.BlockSpec((1, D_MODEL), lambda i: (0, 0))],
        out_specs=pl.BlockSpec((tm, D_MODEL), lambda i: (i, 0)),
        out_shape=jax.ShapeDtypeStruct((m, D_MODEL), F32),
        compiler_params=_params("parallel"),
        name="ple_final",
    )(x, delta, ple, g_ple, w_gate, w_ple, g_final)


def _gdn_kernel(q_ref, k_ref, v_ref, gate_ref, ab_ref, hq_ref, hk_ref, hv_ref,
                cq_ref, ck_ref, cv_ref, alog_ref, dt_ref, ng_ref, s0_ref,
                o_ref, sout_ref, qs, ks, vs, s_scr, *, c, hb, nseq):
    g_idx = pl.program_id(1)
    t_idx = pl.program_id(2)
    hist = SUBLANES
    seqs = range(nseq)

    @pl.when(t_idx == 0)
    def _():
        for s in seqs:
            for h in range(hb):
                s_scr[s * hb + h] = s0_ref[s, h]
            qs[s] = hq_ref[s]
            ks[s] = hk_ref[s]
            vs[s] = hv_ref[s]

    row8 = lax.broadcasted_iota(jnp.int32, (hist, hb * DK_A), 0)

    def conv_silu(scr, x_ref, w_ref):
        outs = []
        for s in seqs:
            u = x_ref[s]
            prev8 = scr[s]
            out = None
            for j in range(CONV_W - 1):
                sh = CONV_W - 1 - j
                rolled = pltpu.roll(u, sh, axis=0)
                top = jnp.where(row8 < sh, pltpu.roll(prev8, sh, axis=0), rolled[0:hist])
                term = jnp.concatenate([top, rolled[hist:]], axis=0) * w_ref[j:j + 1, :]
                out = term if out is None else out + term
            out = out + u * w_ref[CONV_W - 1:CONV_W, :]
            scr[s] = x_ref[s, c - hist:c, :]
            outs.append(out)
        out = jnp.concatenate(outs, axis=0) if nseq > 1 else outs[0]
        return out * _sigmoid(out)

    qc = conv_silu(qs, q_ref, cq_ref)
    kc = conv_silu(ks, k_ref, ck_ref)
    vc = conv_silu(vs, v_ref, cv_ref)

    ab = ab_ref[...].reshape(nseq * c, LANES)
    g_all = -jnp.exp(alog_ref[...]) * _softplus(ab + dt_ref[...])
    beta_all = _sigmoid(ab)
    gcum = _cumsum_rows(g_all, c)
    lane = lax.broadcasted_iota(jnp.int32, (c, LANES), 1)
    last_row =lax.broadcasted_iota(jnp.int32, (c, 1), 0) == c - 1
    gate = gate_ref[...].reshape(nseq * c, hb * DV_A)
    ng = ng_ref[...]

    hs = range(hb)
    sls = [slice(h * DK_A, (h + 1) * DK_A) for h in hs]
    rs = [slice(s * c, (s + 1) * c) for s in seqs]
    heads = [g_idx * hb + h for h in hs]
    items = [(s, h) for s in seqs for h in hs]
    gcum_s = [gcum[r] for r in rs]
    beta_s = [beta_all[r] for r in rs]
    qn = [qc[:, sl] for sl in sls]
    kn = [kc[:, sl] for sl in sls]
    qn = [x * lax.rsqrt(jnp.sum(x * x, axis=-1, keepdims=True) + EPS) * (DK_A ** -0.5) for x in qn]
    kn = [x * lax.rsqrt(jnp.sum(x * x, axis=-1, keepdims=True) + EPS) for x in kn]
    gc = {(s, h): jnp.sum(jnp.where(lane == ALPHA_LANE + heads[h], gcum_s[s], 0.0), axis=1, keepdims=True)
          for s, h in items}
    bc = {(s, h): jnp.sum(jnp.where(lane == BETA_LANE + heads[h], beta_s[s], 0.0), axis=1, keepdims=True)
          for s, h in items}
    qh = {(s, h): qn[h][rs[s]] for s, h in items}
    kh = {(s, h): kn[h][rs[s]] for s, h in items}
    vh = {(s, h): vc[rs[s], sls[h]] for s, h in items}
    pairs = [(s, pr) for s in seqs for pr in range(hb // 2)]
    i2 = lax.broadcasted_iota(jnp.int32, (c, 2 * c), 0)
    j2 = lax.broadcasted_iota(jnp.int32, (c, 2 * c), 1)
    first = j2 < c
    jm = j2 & (c - 1)
    incl_p = i2 >= jm
    strict_p = i2 > jm
    sub2 = lax.broadcasted_iota(jnp.int32, (LANES, 2 * c), 0)
    lane2 = lax.broadcasted_iota(jnp.int32, (LANES, 2 * c), 1)
    gst = [jnp.concatenate([g, g], axis=0).T for g in gcum_s]
    gcp = {(s, pr): jnp.where(first, gc[s, 2 * pr], gc[s, 2 * pr + 1]) for s, pr in pairs}
    grp = {(s, pr): jnp.sum(jnp.where(sub2 == ALPHA_LANE + jnp.where(lane2 < c, heads[2 * pr], heads[2 * pr + 1]),
                                      gst[s], 0.0), axis=0, keepdims=True) for s, pr in pairs}
    dec = {n: jnp.exp(jnp.where(incl_p, gcp[n] - grp[n], -jnp.inf)) for n in pairs}
    kb = {n: kh[n] * bc[n] for n in items}
    pair_of = lambda d, s, pr: jnp.concatenate([d[s, 2 * pr], d[s, 2 * pr + 1]], axis=1)
    k_st = {(s, pr): _pair_stack(pair_of(kh, s, pr), c) for s, pr in pairs}
    kk = {(s, pr): _dot1(pair_of(kb, s, pr), k_st[s, pr], NT) for s, pr in pairs}
    t_inv = dict(zip(pairs, _tri_inv_packed([jnp.where(strict_p, kk[n] * dec[n], 0.0) for n in pairs], c)))
    eg = {n: jnp.exp(gc[n]) for n in items}
    rhs = {n: jnp.concatenate([vh[n] * bc[n], kb[n] * eg[n]], axis=1) for n in items}
    uw_p = {(s, pr): _dot1(t_inv[s, pr], _pair_stack(pair_of(rhs, s, pr), c)) for s, pr in pairs}
    uw = {(s, h): uw_p[s, h // 2][:, (h % 2) * 2 * DV_A:(h % 2 + 1) * 2 * DV_A] for s, h in items}
    qk = {(s, pr): jnp.where(incl_p, _dot1(pair_of(qh, s, pr), k_st[s, pr], NT) * dec[s, pr], 0.0)
          for s, pr in pairs}
    g_last = {n: jnp.sum(jnp.where(last_row, gc[n], 0.0), axis=0, keepdims=True) for n in items}
    kd = {n: kh[n] * jnp.exp(g_last[n] - gc[n]) for n in items}
    s_old = {(s, h): s_scr[s * hb + h] for s, h in items}
    ws_qs = {n: _dot1(jnp.concatenate([uw[n][:, DV_A:], qh[n] * eg[n]], axis=0), s_old[n]) for n in items}
    v_new = {n: uw[n][:, :DV_A] - ws_qs[n][:c] for n in items}
    qkv = {(s, pr): _dot1(qk[s, pr], _pair_stack(pair_of(v_new, s, pr), c)) for s, pr in pairs}
    o = {(s, h): ws_qs[s, h][c:] + qkv[s, h // 2][:, (h % 2) * DV_A:(h % 2 + 1) * DV_A] for s, h in items}
    upd = {n: _dot1(kd[n], v_new[n], TN) for n in items}
    for s, h in items:
        s_scr[s * hb + h] = s_old[s, h] * jnp.exp(g_last[s, h]) + upd[s, h]
    for s, h in items:
        gh = gate[rs[s], sls[h]]
        o_ref[s, :, sls[h]] = (_rms(o[s, h], ng) * (gh * _sigmoid(gh))).astype(o_ref.dtype)

    @pl.when(t_idx == pl.num_programs(2) - 1)
    def _():
        for s in seqs:
            for h in range(hb):
                sout_ref[s, h] = s_scr[s * hb + h]


def gdn_branch(p, hist8, conv_w, alog_vec, dt_vec, norm_g, s0, b, t, c, hb, nseq):
    nc = t // c
    ng = H_A // hb
    gw = hb * DK_A
    seg = lambda k: pl.BlockSpec((nseq, c, gw), lambda bi, g, ti: (bi, ti, P1_OFF // gw + k * ng + g))
    hist = lambda k: pl.BlockSpec((nseq, SUBLANES, gw), lambda bi, g, ti: (bi, 0, k * ng + g))
    cw = lambda k: pl.BlockSpec((CONV_W, gw), lambda bi, g, ti: (0, k * ng + g))
    vec = pl.BlockSpec((1, LANES), lambda bi, g, ti: (0, 0))
    state = pl.BlockSpec((nseq, hb, DK_A, DV_A), lambda bi, g, ti: (bi, g, 0, 0))
    return pl.pallas_call(
        functools.partial(_gdn_kernel, c=c, hb=hb, nseq=nseq),
        grid=(b // nseq, ng, nc),
        in_specs=[seg(0), seg(1), seg(2), seg(3),
                  pl.BlockSpec((nseq, c, LANES), lambda bi, g, ti: (bi, ti, P3_OFF // LANES + AB_BLOCK)),
                  hist(0), hist(1), hist(2), cw(0), cw(1), cw(2), vec, vec, vec, state],
        out_specs=[pl.BlockSpec((nseq, c, gw), lambda bi, g, ti: (bi, ti, g)), state],
        out_shape=[jax.ShapeDtypeStruct((b, t, W_A), BF16),
                   jax.ShapeDtypeStruct((b, H_A, DK_A, DV_A), F32)],
        scratch_shapes=[pltpu.VMEM((nseq, SUBLANES, gw), F32)] * 3 + [pltpu.VMEM((nseq * hb, DK_A, DV_A), F32)],
        compiler_params=_params("parallel", "parallel", "arbitrary"),
        name="gdn_branch",
    )(p, p, p, p, p, hist8, hist8, hist8, conv_w, conv_w, conv_w, alog_vec, dt_vec, norm_g, s0)


def _rwkv_kernel(r_ref, k_ref, v_ref, tail_ref, pr_ref, pk_ref, pv_ref, pt_ref,
                 mr_ref, mk_ref, mv_ref, mt_ref, w0_ref, a0_ref, kk_ref, ka_ref, rk_ref,
                 lg_ref, lb_ref, w2_ref, a2_ref, g2_ref, s0_ref,
                 o_ref, sout_ref, prev_r, prev_k, prev_v, prev_t, s_scr, *, c, pg, nseq):
    t_idx = pl.program_id(2)
    seqs = range(nseq)
    rows = nseq * c

    @pl.when(t_idx == 0)
    def _():
        prev_r[...] = pr_ref[...]
        prev_k[...] = pk_ref[...]
        prev_v[...] = pv_ref[...]
        prev_t[...] = pt_ref[...]
        z = jnp.zeros((N_B, N_B), F32)
        for s in seqs:
            for p in range(pg):
                s_a = s0_ref[s, 2 * p]
                s_b = s0_ref[s, 2 * p + 1]
                s_scr[s * pg + p] = jnp.concatenate([jnp.concatenate([s_a, z], axis=1),
                                                     jnp.concatenate([z, s_b], axis=1)], axis=0)

    def mix(x_ref, prev, mu_ref):
        outs = []
        for s in seqs:
            x = x_ref[s]
            row = lax.broadcasted_iota(jnp.int32, x.shape, 0)
            shifted = jnp.where(row == 0, prev[s], pltpu.roll(x, 1, axis=0))
            prev[s] = x_ref[s, c - 1:c, :]
            outs.append(x + (shifted - x) * mu_ref[...])
        return jnp.concatenate(outs, axis=0) if nseq > 1 else outs[0]

    r = mix(r_ref, prev_r, mr_ref)
    k = mix(k_ref, prev_k, mk_ref)
    v = mix(v_ref, prev_v, mv_ref)
    tail = mix(tail_ref, prev_t, mt_ref)
    xw = tail[:, 0:LANES]
    xa = tail[:, LANES:2 * LANES]
    xg = tail[:, 2 * LANES:]

    w_log = -_softplus(-(w0_ref[...] + _dot1(jnp.tanh(xw), w2_ref[...]))) - 0.5
    lw = -jnp.exp(w_log)
    a = _sigmoid(a0_ref[...] + _dot1(xa, a2_ref[...]))
    gate = _dot1(_sigmoid(xg), g2_ref[...])

    gw = pg * LANES
    fp = (lax.broadcasted_iota(jnp.int32, (rows, LANES), 1) & N_B) == 0

    def seg_sum(x):
        outs = []
        for p in range(pg):
            xp = x[:, p * LANES:(p + 1) * LANES]
            s_a = jnp.sum(jnp.where(fp, xp, 0.0), axis=-1, keepdims=True)
            s_b = jnp.sum(jnp.where(fp, 0.0, xp), axis=-1, keepdims=True)
            outs.append(jnp.where(fp, s_a, s_b))
        return jnp.concatenate(outs, axis=1) if pg > 1 else outs[0]

    kkx = k * kk_ref[...]
    kk = kkx * lax.rsqrt(seg_sum(kkx * kkx) + EPS)
    kh = k * (1.0 + (a - 1.0) * ka_ref[...])
    bb = kk * a
    cw = _cumsum_rows(lw, c)
    rt = r * jnp.exp(cw)
    e_neg = jnp.exp(-cw)
    kt = kh * e_neg
    bt = bb * e_neg
    at = kk * jnp.exp(cw - lw)
    row_c = lax.broadcasted_iota(jnp.int32, (c, gw), 0)
    rs = [slice(s * c, (s + 1) * c) for s in seqs]
    cw_s = [cw[rc] for rc in rs]
    cw_last = [jnp.sum(jnp.where(row_c == c - 1, x, 0.0), axis=0, keepdims=True) for x in cw_s]
    e_last = [jnp.exp(cw_last[s] - cw_s[s]) for s in seqs]
    k_hat = [kh[rs[s]] * e_last[s] for s in seqs]
    b_hat = [bb[rs[s]] * e_last[s] for s in seqs]
    e_end = [jnp.exp(x) for x in cw_last]

    i = lax.broadcasted_iota(jnp.int32, (c, 2 * c), 0)
    jm = lax.broadcasted_iota(jnp.int32, (c, 2 * c), 1) & (c - 1)
    incl = i >= jm
    strict = i > jm
    bi = lax.broadcasted_iota(jnp.int32, (LANES, LANES), 0)
    bj = lax.broadcasted_iota(jnp.int32, (LANES, LANES), 1)
    same_head = ((bi ^ bj) & N_B) == 0

    ps = range(pg)
    sls = [slice(p * LANES, (p + 1) * LANES) for p in ps]
    items = [(s, p) for s in seqs for p in ps]
    at_p = {(s, p): at[rs[s], sls[p]] for s, p in items}
    rt_p = {(s, p): rt[rs[s], sls[p]] for s, p in items}
    v_p = {(s, p): v[rs[s], sls[p]] for s, p in items}
    s_old = {(s, p): s_scr[s * pg + p] for s, p in items}
    a_s = {n: _dot1(at_p[n], s_old[n], NT) for n in items}
    r_s = {n: _dot1(rt_p[n], s_old[n], NT) for n in items}
    lhs = {n: jnp.concatenate([at_p[n], rt_p[n]], axis=0) for n in items}
    xb_ = {(s, p): _dot1(lhs[s, p], _pair_stack(bt[rs[s], sls[p]], c), NT) for s, p in items}
    xk_ = {(s, p): _dot1(lhs[s, p], _pair_stack(kt[rs[s], sls[p]], c), NT) for s, p in items}
    l_ak = {n: jnp.where(strict, xk_[n][:c], 0.0) for n in items}
    r_b = {n: jnp.where(incl, xb_[n][c:], 0.0) for n in items}
    r_k = {n: jnp.where(incl, xk_[n][c:], 0.0) for n in items}
    t_inv = dict(zip(items, _tri_inv_packed([jnp.where(strict, xb_[n][:c], 0.0) for n in items], c)))
    v_st = {n: _pair_stack(v_p[n], c) for n in items}
    rhs = {n: a_s[n] + _dot1(l_ak[n], v_st[n]) for n in items}
    p_all = {n: _dot1(t_inv[n], _pair_stack(rhs[n], c)) for n in items}
    ys = {n: r_s[n] + _dot1(r_k[n], v_st[n]) - _dot1(r_b[n], _pair_stack(p_all[n], c)) for n in items}
    upd = {(s, p): _dot1(jnp.concatenate([v_p[s, p], p_all[s, p]], axis=0),
                         jnp.concatenate([k_hat[s][:, sls[p]], -b_hat[s][:, sls[p]]], axis=0), TN)
           for s, p in items}
    for s, p in items:
        s_scr[s * pg + p] = s_old[s, p] * e_end[s][:, sls[p]] + jnp.where(same_head, upd[s, p], 0.0)

    y = jnp.concatenate([jnp.concatenate([ys[s, p] for p in ps], axis=1) if pg > 1 else ys[s, 0]
                         for s in seqs], axis=0)
    inv_n = 1.0 / N_B
    yc = y - seg_sum(y) * inv_n
    yn = yc * lax.rsqrt(seg_sum(yc * yc) * inv_n + LNX_EPS)
    yn = yn * lg_ref[...] + lb_ref[...]
    bonus = seg_sum(r * kh * rk_ref[...]) * v
    o_ref[...] = ((yn + bonus) * gate).astype(o_ref.dtype).reshape(nseq, c, gw)

    @pl.when(t_idx == pl.num_programs(2) - 1)
    def _():
        for s in seqs:
            for p in range(pg):
                st = s_scr[s * pg + p]
                sout_ref[s, 2 * p] = st[:N_B, :N_B]
                sout_ref[s, 2 * p + 1] = st[N_B:, N_B:]


def rwkv_branch(p, prev, mu, w0, a0, k_k, k_a, r_k, lnx_g, lnx_b, w2p, a2p, g2, s0, b, t, c, pg, nseq):
    nc = t // c
    gw = pg * LANES
    ng = W_B // gw
    tw = XB_PAD - 3 * W_B
    tail_blk = 3 * W_B // tw
    seg = lambda k: pl.BlockSpec((nseq, c, gw), lambda bi, g, ti: (bi, ti, P3_OFF // gw + k * ng + g))
    pseg = lambda k: pl.BlockSpec((nseq, 1, gw), lambda bi, g, ti: (bi, 0, k * ng + g))
    mseg = lambda k: pl.BlockSpec((1, gw), lambda bi, g, ti: (0, k * ng + g))
    chan = pl.BlockSpec((1, gw), lambda bi, g, ti: (0, g))
    state = pl.BlockSpec((nseq, 2 * pg, N_B, N_B), lambda bi, g, ti: (bi, g, 0, 0))
    return pl.pallas_call(
        functools.partial(_rwkv_kernel, c=c, pg=pg, nseq=nseq),
        grid=(b // nseq, ng, nc),
        in_specs=[seg(0), seg(1), seg(2),
                  pl.BlockSpec((nseq, c, tw), lambda bi, g, ti: (bi, ti, P3_OFF // tw + tail_blk)),
                  pseg(0), pseg(1), pseg(2),
                  pl.BlockSpec((nseq, 1, tw), lambda bi, g, ti: (bi, 0, tail_blk)),
                  mseg(0), mseg(1), mseg(2),
                  pl.BlockSpec((1, tw), lambda bi, g, ti: (0, tail_blk)),
                  chan, chan, chan, chan, chan, chan, chan,
                  pl.BlockSpec((LANES, gw), lambda bi, g, ti: (0, g)),
                  pl.BlockSpec((LANES, gw), lambda bi, g, ti: (0, g)),
                  pl.BlockSpec((LORA_G, gw), lambda bi, g, ti: (0, g)),
                  state],
        out_specs=[pl.BlockSpec((nseq, c, gw), lambda bi, g, ti: (bi, ti, g)), state],
        out_shape=[jax.ShapeDtypeStruct((b, t, W_B), BF16),
                   jax.ShapeDtypeStruct((b, H_B, N_B, N_B), F32)],
        scratch_shapes=[pltpu.VMEM((nseq, 1, gw), F32)] * 3 + [pltpu.VMEM((nseq, 1, tw), F32),
                                                               pltpu.VMEM((nseq * pg, LANES, LANES), F32)],
        compiler_params=_params("parallel", "parallel", "arbitrary"),
        name="rwkv_branch",
    )(p, p, p, p, prev, prev, prev, prev, mu, mu, mu, mu,
      w0, a0, k_k, k_a, r_k, lnx_g, lnx_b, w2p, a2p, g2, s0)


def _pad_xb(a, ab=None):
    lead = a.shape[:-1]
    z = lambda n: jnp.zeros(lead + (n,), a.dtype)
    o_a, o_g = 3 * W_B + LORA_W, 3 * W_B + LORA_W + LORA_A
    mid = z(2 * H_A) if ab is None else ab
    return jnp.concatenate([a[..., :o_a], mid, z(LANES - LORA_W - 2 * H_A),
                            a[..., o_a:o_g], z(LANES - LORA_A), a[..., o_g:]], axis=-1)


def _unpad_xb(a):
    o_a = 3 * W_B + LORA_W
    return jnp.concatenate([a[..., :o_a], a[..., 3 * W_B + LANES:3 * W_B + LANES + LORA_A],
                            a[..., 3 * W_B + 2 * LANES:]], axis=-1)


def _pad_rows(w, n):
    return jnp.concatenate([w, jnp.zeros((n - w.shape[0],) + w.shape[1:], w.dtype)], axis=0)


def _lane_vec(v, start):
    return jnp.zeros((1, LANES), F32).at[0, start:start + v.shape[0]].set(v)


def _prepare(lw):
    (g_mix, w_in, conv_w, a_log, dt_bias, gdn_norm_g, mu_shift, w0, w2, a0, a2, g2,
     k_k, k_a, r_k, lnx_g, lnx_b, w_up_a, w_up_b, w_o, g_mlp, w_ff1, w_ff2,
     g_ple, w_ple_gate, w_ple) = lw
    o_alpha = CONV_CH
    o_gate_a = CONV_CH + 2 * H_A
    o_xb = o_gate_a + W_A
    o_gm = o_xb + SHIFT_W
    o_xa = o_xb + 3 * W_B + LORA_W
    o_xg = o_xa + LORA_A
    row = lambda v: v.reshape(1, -1)
    w_in_t = w_in.T
    zrows = lambda n: jnp.zeros((n, D_MODEL), F32)
    return dict(
        g_mix=row(g_mix),
        w_proj_t=jnp.concatenate([
            w_in_t[:CONV_CH], w_in_t[o_gate_a:o_xb], w_in_t[o_gm:],
            w_in_t[o_xb:o_xa], w_in_t[o_alpha:o_gate_a], zrows(LANES - LORA_W - 2 * H_A),
            w_in_t[o_xa:o_xg], zrows(LANES - LORA_A), w_in_t[o_xg:o_gm]], axis=0).astype(BF16),
        conv_w=conv_w,
        alog_vec=_lane_vec(a_log, ALPHA_LANE), dt_vec=_lane_vec(dt_bias, ALPHA_LANE),
        gdn_norm_g=row(gdn_norm_g),
        mu=row(_pad_xb(mu_shift)),
        w0=row(w0), a0=row(a0), k_k=row(k_k), k_a=row(k_a), r_k=row(r_k.reshape(-1)),
        lnx_g=row(lnx_g), lnx_b=row(lnx_b),
        w2p=_pad_rows(w2, LANES).astype(BF16), a2p=_pad_rows(a2, LANES).astype(BF16), g2=g2.astype(BF16),
        w_up_a=w_up_a.astype(BF16), w_up_b=w_up_b.astype(BF16), w_o=w_o.astype(BF16),
        g_mlp=row(g_mlp), w_ff1=w_ff1.astype(BF16), w_ff2=w_ff2.astype(BF16),
        g_ple=row(g_ple), w_ple_gate=w_ple_gate.astype(BF16), w_ple=w_ple.astype(BF16),
    )


def _run_layer(x, ple, s_gdn, buf_gdn, s_rwkv, shift_rwkv, pw, g_final):
    b, t, _ = x.shape
    m = b * t
    c = min(CHUNK, t)
    assert t % c == 0 and c % SUBLANES == 0 and t >= CONV_W - 1 and b % SEQS_PER_STEP == 0
    tm_proj = min(m, 2048)
    tm_stream = min(m, 1024)
    tm_res = min(m, 256)
    x2 = x.reshape(m, D_MODEL)

    few_rows = m < 2 * tm_stream
    p = norm_matmul(x2, pw["g_mix"], pw["w_proj_t"], tm_proj, N_PROJ // 4 if few_rows else 512)
    p3d = p.reshape(b, t, N_PROJ)

    hist8 = jnp.concatenate([jnp.zeros((b, SUBLANES - (CONV_W - 1), CONV_CH), F32), buf_gdn], axis=1)
    o_a, s_gdn_new = gdn_branch(p3d, hist8, pw["conv_w"], pw["alog_vec"], pw["dt_vec"],
                                pw["gdn_norm_g"], s_gdn, b, t, c, hb=GDN_HB, nseq=SEQS_PER_STEP)
    prev = _pad_xb(shift_rwkv).reshape(b, 1, XB_PAD)
    o_b, s_rwkv_new = rwkv_branch(p3d, prev, pw["mu"], pw["w0"], pw["a0"], pw["k_k"], pw["k_a"], pw["r_k"],
                                  pw["lnx_g"], pw["lnx_b"], pw["w2p"], pw["a2p"], pw["g2"], s_rwkv,
                                  b, t, c, pg=RWKV_PG, nseq=SEQS_PER_STEP)

    x1, h2 = merge_out(x2, o_a.reshape(m, W_A), o_b.reshape(m, W_B), p, P2_OFF // D_MODEL,
                       pw["w_up_a"], pw["w_up_b"], pw["w_o"], pw["g_mlp"], tm_res, 512)
    delta = mlp(h2, pw["w_ff1"], pw["w_ff2"], tm_stream, 2048 if few_rows else 1024, 512)
    y = ple_final(x1, delta, ple.reshape(m, D_PLE), pw["g_ple"], pw["w_ple_gate"], pw["w_ple"], g_final,
                  min(m, 512))

    new_buf = p3d[:, t - (CONV_W - 1):, P1_OFF:P1_OFF + CONV_CH]
    new_shift = _unpad_xb(p3d[:, t - 1, P3_OFF:])
    return y.reshape(b, t, D_MODEL), s_gdn_new, new_buf, s_rwkv_new, new_shift


def kernel(x_prompt, x_sample, p_prompt, p_sample, state_gdn, cache_gdn_conv, state_rwkv, cache_rwkv_shift, g_mix, w_in, conv_w, a_log, dt_bias, gdn_norm_g, mu_shift, w0, w2, a0, a2, g2, k_k, k_a, r_k, lnx_g, lnx_b, w_up_a, w_up_b, w_o, g_mlp, w_ff1, w_ff2, g_ple, w_ple_gate, w_ple, g_final):
    params = (g_mix, w_in, conv_w, a_log, dt_bias, gdn_norm_g, mu_shift, w0, w2, a0, a2, g2,
              k_k, k_a, r_k, lnx_g, lnx_b, w_up_a, w_up_b, w_o, g_mlp, w_ff1, w_ff2,
              g_ple, w_ple_gate, w_ple)
    depth = w_in.shape[0]
    assert depth == 1
    pw = _prepare(tuple(w[0] for w in params))
    gf = g_final.reshape(1, -1)
    bp = x_prompt.shape[0]
    z_sg = jnp.zeros((bp, H_A, DK_A, DV_A), F32)
    z_buf = jnp.zeros((bp, CONV_W - 1, CONV_CH), F32)
    z_sr = jnp.zeros((bp, H_B, N_B, N_B), F32)
    z_sh = jnp.zeros((bp, SHIFT_W), F32)
    y_p, sg_p, buf_p, sr_p, sh_p = _run_layer(x_prompt, p_prompt[0], z_sg, z_buf, z_sr, z_sh, pw, gf)
    y_s, sg_s, buf_s, sr_s, sh_s = _run_layer(x_sample, p_sample[0], state_gdn[0], cache_gdn_conv[0],
                                              state_rwkv[0], cache_rwkv_shift[0], pw, gf)
    st = lambda a: a[None]
    return (y_p, y_s, st(sg_p), st(buf_p), st(sr_p), st(sh_p), st(sg_s), st(buf_s), st(sr_s), st(sh_s))
```

```python
import functools

import jax
import jax.numpy as jnp
from jax import lax
from jax.experimental import pallas as pl
from jax.experimental.pallas import tpu as pltpu

F32 = jnp.float32
BF16 = jnp.bfloat16

D_MODEL = 2048
H_A, DK_A, DV_A = 8, 128, 128
W_A = H_A * DV_A
CONV_W = 4
CONV_CH = 3 * W_A
H_B, N_B = 16, 64
W_B = H_B * N_B
LORA_W, LORA_A, LORA_G = 96, 96, 256
SHIFT_W = 3 * W_B + LORA_W + LORA_A + LORA_G
D_FF = 4 * D_MODEL
D_PLE = 256
EPS = 1e-6
LNX_EPS = 64e-5
CHUNK = 64
GDN_HB = 8
RWKV_PG = 8
SEQS_PER_STEP = 4

LANES = 128
SUBLANES = 8
VMEM_LIMIT = 52 * 2 ** 20
VMEM_LIMIT_PROJ = 58 * 2 ** 20

XB_PAD = 3 * W_B + 2 * LANES + LORA_G
AB_BLOCK = 3 * W_B // LANES
ALPHA_LANE = LORA_W
BETA_LANE = LORA_W + H_A

P1_OFF, P2_OFF, P3_OFF = 0, 4 * W_A, 4 * W_A + 2 * D_MODEL
N_PROJ = P3_OFF + XB_PAD

NN = ((1,), (0,))
NT = ((1,), (1,))
TN = ((0,), (0,))


def _dg(a, b, dims):
    return lax.dot_general(a, b, (dims, ((), ())), preferred_element_type=F32)


def _dot1(a, b, dims=NN):
    return _dg(a.astype(BF16), b.astype(BF16), dims)


def _cumsum_rows(x, c):
    rows = x.shape[0]
    i = lax.broadcasted_iota(jnp.int32, (rows, rows), 0)
    j = lax.broadcasted_iota(jnp.int32, (rows, rows), 1)
    same_chunk = (i ^ j) < c
    tri = jnp.where((i >= j) & same_chunk, 1.0, 0.0).astype(BF16)
    h1 = x.astype(BF16)
    r1 = x - h1.astype(F32)
    h2 = r1.astype(BF16)
    h3 = (r1 - h2.astype(F32)).astype(BF16)
    return _dg(tri, h1, NN) + (_dg(tri, h2, NN) + _dg(tri, h3, NN))


def _pair_stack(x, c):
    n = x.shape[1]
    bi = lax.broadcasted_iota(jnp.int32, (2 * c, n), 0)
    bj = lax.broadcasted_iota(jnp.int32, (2 * c, n), 1)
    keep = (bi >= c) == (bj >= n // 2)
    return jnp.where(keep, jnp.concatenate([x, x], axis=0), 0.0)


def _tri_inv_packed(a_list, c):
    i = lax.broadcasted_iota(jnp.int32, (c, 2 * c), 0)
    jm = lax.broadcasted_iota(jnp.int32, (c, 2 * c), 1) & (c - 1)
    ns = [-jnp.where((i ^ jm) == 1, a, 0.0) for a in a_list]
    s, ls = 2, 1
    while s < c:
        mask = ((i ^ jm) >> ls) == 1
        l_s = [jnp.where(mask, a, 0.0) for a in a_list]
        nl = [_dot1(n, _pair_stack(l, c)) for n, l in zip(ns, l_s)]
        ys = [l + x for l, x in zip(l_s, nl)]
        yn = [_dot1(y, _pair_stack(n, c)) for y, n in zip(ys, ns)]
        ns = [n - (y + z) for n, y, z in zip(ns, ys, yn)]
        s, ls = 2 * s, ls + 1
    eye = jnp.where(i == jm, 1.0, 0.0)
    return [eye + n for n in ns]


def _softplus(x):
    return jnp.maximum(x, 0.0) + jnp.log1p(jnp.exp(-jnp.abs(x)))


def _sigmoid(x):
    return jax.nn.sigmoid(x)


def _rms(x, g):
    return x * lax.rsqrt(jnp.mean(x * x, axis=-1, keepdims=True) + EPS) * g


def _params(*sem, vmem=VMEM_LIMIT):
    return pltpu.CompilerParams(dimension_semantics=sem, vmem_limit_bytes=vmem)


def _norm_matmul_kernel(x_ref, g_ref, w_ref, o_ref, h_scr):
    @pl.when(pl.program_id(1) == 0)
    def _():
        h_scr[...] = _rms(x_ref[...], g_ref[...]).astype(BF16)

    o_ref[...] = _dg(h_scr[...], w_ref[...], NT).astype(o_ref.dtype)


def norm_matmul(x, g, wt, tm, tn, out_dtype=F32):
    m, k = x.shape
    n = wt.shape[0]
    return pl.pallas_call(
        _norm_matmul_kernel,
        grid=(m // tm, n // tn),
        in_specs=[pl.BlockSpec((tm, k), lambda i, j: (i, 0)),
                  pl.BlockSpec((1, k), lambda i, j: (0, 0)),
                  pl.BlockSpec((tn, k), lambda i, j: (j, 0))],
        out_specs=pl.BlockSpec((tm, tn), lambda i, j: (i, j)),
        out_shape=jax.ShapeDtypeStruct((m, n), out_dtype),
        scratch_shapes=[pltpu.VMEM((tm, k), BF16)],
        compiler_params=_params("parallel", "arbitrary", vmem=VMEM_LIMIT_PROJ),
        name="norm_matmul",
    )(x, g, wt)


def _merge_kernel(x_ref, oa_ref, ob_ref, gma_ref, gmb_ref, wa_ref, wb_ref, wo_ref, g_ref, o_ref, h_ref, *, tn):
    oa = oa_ref[...]
    ob = ob_ref[...]
    acc = x_ref[...]
    for n in range(D_MODEL // tn):
        cs = slice(n * tn, (n + 1) * tn)
        merged = (_sigmoid(gma_ref[:, cs]) * _dg(oa, wa_ref[:, cs], NN)
                  + _sigmoid(gmb_ref[:, cs]) * _dg(ob, wb_ref[:, cs], NN))
        acc = acc + _dg(merged.astype(BF16), wo_ref[cs, :], NN)
    o_ref[...] = acc
    h_ref[...] = _rms(acc, g_ref[...]).astype(BF16)


def merge_out(x, o_a, o_b, p, gate_blk, w_up_a, w_up_b, w_o, g_mlp, tm, tn):
    m = x.shape[0]
    resident = lambda shape: pl.BlockSpec(shape, lambda i: (0, 0), pipeline_mode=pl.Buffered(1))
    return pl.pallas_call(
        functools.partial(_merge_kernel, tn=tn),
        grid=(m // tm,),
        in_specs=[pl.BlockSpec((tm, D_MODEL), lambda i: (i, 0)),
                  pl.BlockSpec((tm, W_A), lambda i: (i, 0)),
                  pl.BlockSpec((tm, W_B), lambda i: (i, 0)),
                  pl.BlockSpec((tm, D_MODEL), lambda i: (i, gate_blk)),
                  pl.BlockSpec((tm, D_MODEL), lambda i: (i, gate_blk + 1)),
                  resident((W_A, D_MODEL)), resident((W_B, D_MODEL)), resident((D_MODEL, D_MODEL)),
                  pl.BlockSpec((1, D_MODEL), lambda i: (0, 0))],
        out_specs=[pl.BlockSpec((tm, D_MODEL), lambda i: (i, 0)),
                   pl.BlockSpec((tm, D_MODEL), lambda i: (i, 0))],
        out_shape=[jax.ShapeDtypeStruct((m, D_MODEL), F32), jax.ShapeDtypeStruct((m, D_MODEL), BF16)],
        compiler_params=_params("parallel"),
        name="merge_out",
    )(x, o_a, o_b, p, p, w_up_a, w_up_b, w_o, g_mlp)


def _mlp_kernel(h_ref, x_hbm, w1_ref, w2_ref, o_ref, x_scr, sem, *, tc):
    f = pl.program_id(1)
    tm = o_ref.shape[0]
    x_copy = pltpu.make_async_copy(x_hbm.at[pl.ds(pl.program_id(0) * tm, tm)], x_scr, sem)

    @pl.when(f == 0)
    def _():
        x_copy.start()
        o_ref[...] = jnp.zeros_like(o_ref)

    h = h_ref[...]
    for fc in range(w1_ref.shape[1] // tc):
        fs = slice(fc * tc, (fc + 1) * tc)
        a = jnp.maximum(_dg(h, w1_ref[:, fs], NN), 0.0)
        a = (a * a).astype(BF16)
        for nc in range(D_MODEL // tc):
            ns = slice(nc * tc, (nc + 1) * tc)
            o_ref[:, ns] += _dg(a, w2_ref[fs, ns], NN)

    @pl.when(f == pl.num_programs(1) - 1)
    def _():
        x_copy.wait()
        o_ref[...] += x_scr[...]


def mlp(h, x, w1, w2, tm, tf, tc):
    m = h.shape[0]
    return pl.pallas_call(
        functools.partial(_mlp_kernel, tc=tc),
        grid=(m // tm, D_FF // tf),
        in_specs=[pl.BlockSpec((tm, D_MODEL), lambda i, f: (i, 0)),
                  pl.BlockSpec(memory_space=pl.ANY),
                  pl.BlockSpec((D_MODEL, tf), lambda i, f: (0, f)),
                  pl.BlockSpec((tf, D_MODEL), lambda i, f: (f, 0))],
        out_specs=pl.BlockSpec((tm, D_MODEL), lambda i, f: (i, 0)),
        out_shape=jax.ShapeDtypeStruct((m, D_MODEL), F32),
        scratch_shapes=[pltpu.VMEM((tm, D_MODEL), F32), pltpu.SemaphoreType.DMA(())],
        compiler_params=_params("parallel", "arbitrary", vmem=VMEM_LIMIT_PROJ),
        name="mlp",
    )(h, x, w1, w2)


def _ple_final_kernel(x_ref, p_ref, gp_ref, wg_ref, wp_ref, gf_ref, o_ref):
    x = x_ref[...]
    gate = _sigmoid(_dg(_rms(x, gp_ref[...]).astype(BF16), wg_ref[...], NN))
    x = x + gate * _dg(p_ref[...].astype(BF16), wp_ref[...], NN)
    o_ref[...] = _rms(x, gf_ref[...])


def ple_final(x, ple, g_ple, w_gate, w_ple, g_final, tm):
    m = x.shape[0]
    resident = lambda shape: pl.BlockSpec(shape, lambda i: (0, 0), pipeline_mode=pl.Buffered(1))
    return pl.pallas_call(
        _ple_final_kernel,
        grid=(m // tm,),
        in_specs=[pl.BlockSpec((tm, D_MODEL), lambda i: (i, 0)),
                  pl.BlockSpec((tm, D_PLE), lambda i: (i, 0)),
                  pl.BlockSpec((1, D_MODEL), lambda i: (0, 0)),
                  resident((D_MODEL, D_MODEL)), resident((D_PLE, D_MODEL)),
                  pl.BlockSpec((1, D_MODEL), lambda i: (0, 0))],
        out_specs=pl.BlockSpec((tm, D_MODEL), lambda i: (i, 0)),
        out_shape=jax.ShapeDtypeStruct((m, D_MODEL), F32),
        compiler_params=_params("parallel"),
        name="ple_final",
    )(x, ple, g_ple, w_gate, w_ple, g_final)


def _gdn_kernel(q_ref, k_ref, v_ref, gate_ref, ab_ref, hq_ref, hk_ref, hv_ref,
                cq_ref, ck_ref, cv_ref, alog_ref, dt_ref, ng_ref, s0_ref,
                o_ref, sout_ref, qs, ks, vs, s_scr, *, c, hb, nseq):
    g_idx = pl.program_id(1)
    t_idx = pl.program_id(2)
    hist = SUBLANES
    seqs = range(nseq)

    @pl.when(t_idx == 0)
    def _():
        for s in seqs:
            for h in range(hb):
                s_scr[s * hb + h] = s0_ref[s, h]
            qs[s] = hq_ref[s]
            ks[s] = hk_ref[s]
            vs[s] = hv_ref[s]

    row8 = lax.broadcasted_iota(jnp.int32, (hist, hb * DK_A), 0)

    def conv_silu(scr, x_ref, w_ref):
        outs = []
        for s in seqs:
            u = x_ref[s]
            prev8 = scr[s]
            out = None
            for j in range(CONV_W - 1):
                sh = CONV_W - 1 - j
                rolled = pltpu.roll(u, sh, axis=0)
                top = jnp.where(row8 < sh, pltpu.roll(prev8, sh, axis=0), rolled[0:hist])
                term = jnp.concatenate([top, rolled[hist:]], axis=0) * w_ref[j:j + 1, :]
                out = term if out is None else out + term
            out = out + u * w_ref[CONV_W - 1:CONV_W, :]
            scr[s] = x_ref[s, c - hist:c, :]
            outs.append(out)
        out = jnp.concatenate(outs, axis=0) if nseq > 1 else outs[0]
        return out * _sigmoid(out)

    qc = conv_silu(qs, q_ref, cq_ref)
    kc = conv_silu(ks, k_ref, ck_ref)
    vc = conv_silu(vs, v_ref, cv_ref)

    ab = ab_ref[...].reshape(nseq * c, LANES)
    g_all = -jnp.exp(alog_ref[...]) * _softplus(ab + dt_ref[...])
    beta_all = _sigmoid(ab)
    gcum = _cumsum_rows(g_all, c)
    lane = lax.broadcasted_iota(jnp.int32, (c, LANES), 1)
    last_row =lax.broadcasted_iota(jnp.int32, (c, 1), 0) == c - 1
    gate = gate_ref[...].reshape(nseq * c, hb * DV_A)
    ng = ng_ref[...]

    hs = range(hb)
    sls = [slice(h * DK_A, (h + 1) * DK_A) for h in hs]
    rs = [slice(s * c, (s + 1) * c) for s in seqs]
    heads = [g_idx * hb + h for h in hs]
    items = [(s, h) for s in seqs for h in hs]
    gcum_s = [gcum[r] for r in rs]
    beta_s = [beta_all[r] for r in rs]
    qn = [qc[:, sl] for sl in sls]
    kn = [kc[:, sl] for sl in sls]
    qn = [x * lax.rsqrt(jnp.sum(x * x, axis=-1, keepdims=True) + EPS) * (DK_A ** -0.5) for x in qn]
    kn = [x * lax.rsqrt(jnp.sum(x * x, axis=-1, keepdims=True) + EPS) for x in kn]
    gc = {(s, h): jnp.sum(jnp.where(lane == ALPHA_LANE + heads[h], gcum_s[s], 0.0), axis=1, keepdims=True)
          for s, h in items}
    bc = {(s, h): jnp.sum(jnp.where(lane == BETA_LANE + heads[h], beta_s[s], 0.0), axis=1, keepdims=True)
          for s, h in items}
    qh = {(s, h): qn[h][rs[s]] for s, h in items}
    kh = {(s, h): kn[h][rs[s]] for s, h in items}
    vh = {(s, h): vc[rs[s], sls[h]] for s, h in items}
    pairs = [(s, pr) for s in seqs for pr in range(hb // 2)]
    i2 = lax.broadcasted_iota(jnp.int32, (c, 2 * c), 0)
    j2 = lax.broadcasted_iota(jnp.int32, (c, 2 * c), 1)
    first = j2 < c
    jm = j2 & (c - 1)
    incl_p = i2 >= jm
    strict_p = i2 > jm
    sub2 = lax.broadcasted_iota(jnp.int32, (LANES, 2 * c), 0)
    lane2 = lax.broadcasted_iota(jnp.int32, (LANES, 2 * c), 1)
    gst = [jnp.concatenate([g, g], axis=0).T for g in gcum_s]
    gcp = {(s, pr): jnp.where(first, gc[s, 2 * pr], gc[s, 2 * pr + 1]) for s, pr in pairs}
    grp = {(s, pr): jnp.sum(jnp.where(sub2 == ALPHA_LANE + jnp.where(lane2 < c, heads[2 * pr], heads[2 * pr + 1]),
                                      gst[s], 0.0), axis=0, keepdims=True) for s, pr in pairs}
    dec = {n: jnp.exp(jnp.where(incl_p, gcp[n] - grp[n], -jnp.inf)) for n in pairs}
    kb = {n: kh[n] * bc[n] for n in items}
    pair_of = lambda d, s, pr: jnp.concatenate([d[s, 2 * pr], d[s, 2 * pr + 1]], axis=1)
    k_st = {(s, pr): _pair_stack(pair_of(kh, s, pr), c) for s, pr in pairs}
    kk = {(s, pr): _dot1(pair_of(kb, s, pr), k_st[s, pr], NT) for s, pr in pairs}
    t_inv = dict(zip(pairs, _tri_inv_packed([jnp.where(strict_p, kk[n] * dec[n], 0.0) for n in pairs], c)))
    eg = {n: jnp.exp(gc[n]) for n in items}
    rhs = {n: jnp.concatenate([vh[n] * bc[n], kb[n] * eg[n]], axis=1) for n in items}
    uw_p = {(s, pr): _dot1(t_inv[s, pr], _pair_stack(pair_of(rhs, s, pr), c)) for s, pr in pairs}
    uw = {(s, h): uw_p[s, h // 2][:, (h % 2) * 2 * DV_A:(h % 2 + 1) * 2 * DV_A] for s, h in items}
    qk = {(s, pr): jnp.where(incl_p, _dot1(pair_of(qh, s, pr), k_st[s, pr], NT) * dec[s, pr], 0.0)
          for s, pr in pairs}
    g_last = {n: jnp.sum(jnp.where(last_row, gc[n], 0.0), axis=0, keepdims=True) for n in items}
    kd = {n: kh[n] * jnp.exp(g_last[n] - gc[n]) for n in items}
    s_old = {(s, h): s_scr[s * hb + h] for s, h in items}
    ws_qs = {n: _dot1(jnp.concatenate([uw[n][:, DV_A:], qh[n] * eg[n]], axis=0), s_old[n]) for n in items}
    v_new = {n: uw[n][:, :DV_A] - ws_qs[n][:c] for n in items}
    qkv = {(s, pr): _dot1(qk[s, pr], _pair_stack(pair_of(v_new, s, pr), c)) for s, pr in pairs}
    o = {(s, h): ws_qs[s, h][c:] + qkv[s, h // 2][:, (h % 2) * DV_A:(h % 2 + 1) * DV_A] for s, h in items}
    upd = {n: _dot1(kd[n], v_new[n], TN) for n in items}
    for s, h in items:
        s_scr[s * hb + h] = s_old[s, h] * jnp.exp(g_last[s, h]) + upd[s, h]
    for s, h in items:
        gh = gate[rs[s], sls[h]]
        o_ref[s, :, sls[h]] = (_rms(o[s, h], ng) * (gh * _sigmoid(gh))).astype(o_ref.dtype)

    @pl.when(t_idx == pl.num_programs(2) - 1)
    def _():
        for s in seqs:
            for h in range(hb):
                sout_ref[s, h] = s_scr[s * hb + h]


def gdn_branch(p, hist8, conv_w, alog_vec, dt_vec, norm_g, s0, b, t, c, hb, nseq):
    nc = t // c
    ng = H_A // hb
    gw = hb * DK_A
    seg = lambda k: pl.BlockSpec((nseq, c, gw), lambda bi, g, ti: (bi, ti, P1_OFF // gw + k * ng + g))
    hist = lambda k: pl.BlockSpec((nseq, SUBLANES, gw), lambda bi, g, ti: (bi, 0, k * ng + g))
    cw = lambda k: pl.BlockSpec((CONV_W, gw), lambda bi, g, ti: (0, k * ng + g))
    vec = pl.BlockSpec((1, LANES), lambda bi, g, ti: (0, 0))
    state = pl.BlockSpec((nseq, hb, DK_A, DV_A), lambda bi, g, ti: (bi, g, 0, 0))
    return pl.pallas_call(
        functools.partial(_gdn_kernel, c=c, hb=hb, nseq=nseq),
        grid=(b // nseq, ng, nc),
        in_specs=[seg(0), seg(1), seg(2), seg(3),
                  pl.BlockSpec((nseq, c, LANES), lambda bi, g, ti: (bi, ti, P3_OFF // LANES + AB_BLOCK)),
                  hist(0), hist(1), hist(2), cw(0), cw(1), cw(2), vec, vec, vec, state],
        out_specs=[pl.BlockSpec((nseq, c, gw), lambda bi, g, ti: (bi, ti, g)), state],
        out_shape=[jax.ShapeDtypeStruct((b, t, W_A), BF16),
                   jax.ShapeDtypeStruct((b, H_A, DK_A, DV_A), F32)],
        scratch_shapes=[pltpu.VMEM((nseq, SUBLANES, gw), F32)] * 3 + [pltpu.VMEM((nseq * hb, DK_A, DV_A), F32)],
        compiler_params=_params("parallel", "parallel", "arbitrary"),
        name="gdn_branch",
    )(p, p, p, p, p, hist8, hist8, hist8, conv_w, conv_w, conv_w, alog_vec, dt_vec, norm_g, s0)


def _rwkv_kernel(r_ref, k_ref, v_ref, tail_ref, pr_ref, pk_ref, pv_ref, pt_ref,
                 mr_ref, mk_ref, mv_ref, mt_ref, w0_ref, a0_ref, kk_ref, ka_ref, rk_ref,
                 lg_ref, lb_ref, w2_ref, a2_ref, g2_ref, s0_ref,
                 o_ref, sout_ref, prev_r, prev_k, prev_v, prev_t, s_scr, *, c, pg, nseq):
    t_idx = pl.program_id(2)
    seqs = range(nseq)
    rows = nseq * c

    @pl.when(t_idx == 0)
    def _():
        prev_r[...] = pr_ref[...]
        prev_k[...] = pk_ref[...]
        prev_v[...] = pv_ref[...]
        prev_t[...] = pt_ref[...]
        z = jnp.zeros((N_B, N_B), F32)
        for s in seqs:
            for p in range(pg):
                s_a = s0_ref[s, 2 * p]
                s_b = s0_ref[s, 2 * p + 1]
                s_scr[s * pg + p] = jnp.concatenate([jnp.concatenate([s_a, z], axis=1),
                                                     jnp.concatenate([z, s_b], axis=1)], axis=0)

    def mix(x_ref, prev, mu_ref):
        outs = []
        for s in seqs:
            x = x_ref[s]
            row = lax.broadcasted_iota(jnp.int32, x.shape, 0)
            shifted = jnp.where(row == 0, prev[s], pltpu.roll(x, 1, axis=0))
            prev[s] = x_ref[s, c - 1:c, :]
            outs.append(x + (shifted - x) * mu_ref[...])
        return jnp.concatenate(outs, axis=0) if nseq > 1 else outs[0]

    r = mix(r_ref, prev_r, mr_ref)
    k = mix(k_ref, prev_k, mk_ref)
    v = mix(v_ref, prev_v, mv_ref)
    tail = mix(tail_ref, prev_t, mt_ref)
    xw = tail[:, 0:LANES]
    xa = tail[:, LANES:2 * LANES]
    xg = tail[:, 2 * LANES:]

    w_log = -_softplus(-(w0_ref[...] + _dot1(jnp.tanh(xw), w2_ref[...]))) - 0.5
    lw = -jnp.exp(w_log)
    a = _sigmoid(a0_ref[...] + _dot1(xa, a2_ref[...]))
    gate = _dot1(_sigmoid(xg), g2_ref[...])

    gw = pg * LANES
    fp = (lax.broadcasted_iota(jnp.int32, (rows, LANES), 1) & N_B) == 0

    def seg_sum(x):
        outs = []
        for p in range(pg):
            xp = x[:, p * LANES:(p + 1) * LANES]
            s_a = jnp.sum(jnp.where(fp, xp, 0.0), axis=-1, keepdims=True)
            s_b = jnp.sum(jnp.where(fp, 0.0, xp), axis=-1, keepdims=True)
            outs.append(jnp.where(fp, s_a, s_b))
        return jnp.concatenate(outs, axis=1) if pg > 1 else outs[0]

    kkx = k * kk_ref[...]
    kk = kkx * lax.rsqrt(seg_sum(kkx * kkx) + EPS)
    kh = k * (1.0 + (a - 1.0) * ka_ref[...])
    bb = kk * a
    cw = _cumsum_rows(lw, c)
    rt = r * jnp.exp(cw)
    e_neg = jnp.exp(-cw)
    kt = kh * e_neg
    bt = bb * e_neg
    at = kk * jnp.exp(cw - lw)
    row_c = lax.broadcasted_iota(jnp.int32, (c, gw), 0)
    rs = [slice(s * c, (s + 1) * c) for s in seqs]
    cw_s = [cw[rc] for rc in rs]
    cw_last = [jnp.sum(jnp.where(row_c == c - 1, x, 0.0), axis=0, keepdims=True) for x in cw_s]
    e_last = [jnp.exp(cw_last[s] - cw_s[s]) for s in seqs]
    k_hat = [kh[rs[s]] * e_last[s] for s in seqs]
    b_hat = [bb[rs[s]] * e_last[s] for s in seqs]
    e_end = [jnp.exp(x) for x in cw_last]

    i = lax.broadcasted_iota(jnp.int32, (c, 2 * c), 0)
    jm = lax.broadcasted_iota(jnp.int32, (c, 2 * c), 1) & (c - 1)
    incl = i >= jm
    strict = i > jm
    bi = lax.broadcasted_iota(jnp.int32, (LANES, LANES), 0)
    bj = lax.broadcasted_iota(jnp.int32, (LANES, LANES), 1)
    same_head = ((bi ^ bj) & N_B) == 0

    ps = range(pg)
    sls = [slice(p * LANES, (p + 1) * LANES) for p in ps]
    items = [(s, p) for s in seqs for p in ps]
    at_p = {(s, p): at[rs[s], sls[p]] for s, p in items}
    rt_p = {(s, p): rt[rs[s], sls[p]] for s, p in items}
    v_p = {(s, p): v[rs[s], sls[p]] for s, p in items}
    s_old = {(s, p): s_scr[s * pg + p] for s, p in items}
    a_s = {n: _dot1(at_p[n], s_old[n], NT) for n in items}
    r_s = {n: _dot1(rt_p[n], s_old[n], NT) for n in items}
    lhs = {n: jnp.concatenate([at_p[n], rt_p[n]], axis=0) for n in items}
    xb_ = {(s, p): _dot1(lhs[s, p], _pair_stack(bt[rs[s], sls[p]], c), NT) for s, p in items}
    xk_ = {(s, p): _dot1(lhs[s, p], _pair_stack(kt[rs[s], sls[p]], c), NT) for s, p in items}
    l_ak = {n: jnp.where(strict, xk_[n][:c], 0.0) for n in items}
    r_b = {n: jnp.where(incl, xb_[n][c:], 0.0) for n in items}
    r_k = {n: jnp.where(incl, xk_[n][c:], 0.0) for n in items}
    t_inv = dict(zip(items, _tri_inv_packed([jnp.where(strict, xb_[n][:c], 0.0) for n in items], c)))
    v_st = {n: _pair_stack(v_p[n], c) for n in items}
    rhs = {n: a_s[n] + _dot1(l_ak[n], v_st[n]) for n in items}
    p_all = {n: _dot1(t_inv[n], _pair_stack(rhs[n], c)) for n in items}
    ys = {n: r_s[n] + _dot1(r_k[n], v_st[n]) - _dot1(r_b[n], _pair_stack(p_all[n], c)) for n in items}
    upd = {(s, p): _dot1(jnp.concatenate([v_p[s, p], p_all[s, p]], axis=0),
                         jnp.concatenate([k_hat[s][:, sls[p]], -b_hat[s][:, sls[p]]], axis=0), TN)
           for s, p in items}
    for s, p in items:
        s_scr[s * pg + p] = s_old[s, p] * e_end[s][:, sls[p]] + jnp.where(same_head, upd[s, p], 0.0)

    y = jnp.concatenate([jnp.concatenate([ys[s, p] for p in ps], axis=1) if pg > 1 else ys[s, 0]
                         for s in seqs], axis=0)
    inv_n = 1.0 / N_B
    yc = y - seg_sum(y) * inv_n
    yn = yc * lax.rsqrt(seg_sum(yc * yc) * inv_n + LNX_EPS)
    yn = yn * lg_ref[...] + lb_ref[...]
    bonus = seg_sum(r * kh * rk_ref[...]) * v
    o_ref[...] = ((yn + bonus) * gate).astype(o_ref.dtype).reshape(nseq, c, gw)

    @pl.when(t_idx == pl.num_programs(2) - 1)
    def _():
        for s in seqs:
            for p in range(pg):
                st = s_scr[s * pg + p]
                sout_ref[s, 2 * p] = st[:N_B, :N_B]
                sout_ref[s, 2 * p + 1] = st[N_B:, N_B:]


def rwkv_branch(p, prev, mu, w0, a0, k_k, k_a, r_k, lnx_g, lnx_b, w2p, a2p, g2, s0, b, t, c, pg, nseq):
    nc = t // c
    gw = pg * LANES
    ng = W_B // gw
    tw = XB_PAD - 3 * W_B
    tail_blk = 3 * W_B // tw
    seg = lambda k: pl.BlockSpec((nseq, c, gw), lambda bi, g, ti: (bi, ti, P3_OFF // gw + k * ng + g))
    pseg = lambda k: pl.BlockSpec((nseq, 1, gw), lambda bi, g, ti: (bi, 0, k * ng + g))
    mseg = lambda k: pl.BlockSpec((1, gw), lambda bi, g, ti: (0, k * ng + g))
    chan = pl.BlockSpec((1, gw), lambda bi, g, ti: (0, g))
    state = pl.BlockSpec((nseq, 2 * pg, N_B, N_B), lambda bi, g, ti: (bi, g, 0, 0))
    return pl.pallas_call(
        functools.partial(_rwkv_kernel, c=c, pg=pg, nseq=nseq),
        grid=(b // nseq, ng, nc),
        in_specs=[seg(0), seg(1), seg(2),
                  pl.BlockSpec((nseq, c, tw), lambda bi, g, ti: (bi, ti, P3_OFF // tw + tail_blk)),
                  pseg(0), pseg(1), pseg(2),
                  pl.BlockSpec((nseq, 1, tw), lambda bi, g, ti: (bi, 0, tail_blk)),
                  mseg(0), mseg(1), mseg(2),
                  pl.BlockSpec((1, tw), lambda bi, g, ti: (0, tail_blk)),
                  chan, chan, chan, chan, chan, chan, chan,
                  pl.BlockSpec((LANES, gw), lambda bi, g, ti: (0, g)),
                  pl.BlockSpec((LANES, gw), lambda bi, g, ti: (0, g)),
                  pl.BlockSpec((LORA_G, gw), lambda bi, g, ti: (0, g)),
                  state],
        out_specs=[pl.BlockSpec((nseq, c, gw), lambda bi, g, ti: (bi, ti, g)), state],
        out_shape=[jax.ShapeDtypeStruct((b, t, W_B), BF16),
                   jax.ShapeDtypeStruct((b, H_B, N_B, N_B), F32)],
        scratch_shapes=[pltpu.VMEM((nseq, 1, gw), F32)] * 3 + [pltpu.VMEM((nseq, 1, tw), F32),
                                                               pltpu.VMEM((nseq * pg, LANES, LANES), F32)],
        compiler_params=_params("parallel", "parallel", "arbitrary"),
        name="rwkv_branch",
    )(p, p, p, p, prev, prev, prev, prev, mu, mu, mu, mu,
      w0, a0, k_k, k_a, r_k, lnx_g, lnx_b, w2p, a2p, g2, s0)


def _pad_xb(a, ab=None):
    lead = a.shape[:-1]
    z = lambda n: jnp.zeros(lead + (n,), a.dtype)
    o_a, o_g = 3 * W_B + LORA_W, 3 * W_B + LORA_W + LORA_A
    mid = z(2 * H_A) if ab is None else ab
    return jnp.concatenate([a[..., :o_a], mid, z(LANES - LORA_W - 2 * H_A),
                            a[..., o_a:o_g], z(LANES - LORA_A), a[..., o_g:]], axis=-1)


def _unpad_xb(a):
    o_a = 3 * W_B + LORA_W
    return jnp.concatenate([a[..., :o_a], a[..., 3 * W_B + LANES:3 * W_B + LANES + LORA_A],
                            a[..., 3 * W_B + 2 * LANES:]], axis=-1)


def _pad_rows(w, n):
    return jnp.concatenate([w, jnp.zeros((n - w.shape[0],) + w.shape[1:], w.dtype)], axis=0)


def _lane_vec(v, start):
    return jnp.zeros((1, LANES), F32).at[0, start:start + v.shape[0]].set(v)


def _prepare(lw):
    (g_mix, w_in, conv_w, a_log, dt_bias, gdn_norm_g, mu_shift, w0, w2, a0, a2, g2,
     k_k, k_a, r_k, lnx_g, lnx_b, w_up_a, w_up_b, w_o, g_mlp, w_ff1, w_ff2,
     g_ple, w_ple_gate, w_ple) = lw
    o_alpha = CONV_CH
    o_gate_a = CONV_CH + 2 * H_A
    o_xb = o_gate_a + W_A
    o_gm = o_xb + SHIFT_W
    o_xa = o_xb + 3 * W_B + LORA_W
    o_xg = o_xa + LORA_A
    row = lambda v: v.reshape(1, -1)
    w_in_t = w_in.T
    zrows = lambda n: jnp.zeros((n, D_MODEL), F32)
    return dict(
        g_mix=row(g_mix),
        w_proj_t=jnp.concatenate([
            w_in_t[:CONV_CH], w_in_t[o_gate_a:o_xb], w_in_t[o_gm:],
            w_in_t[o_xb:o_xa], w_in_t[o_alpha:o_gate_a], zrows(LANES - LORA_W - 2 * H_A),
            w_in_t[o_xa:o_xg], zrows(LANES - LORA_A), w_in_t[o_xg:o_gm]], axis=0).astype(BF16),
        conv_w=conv_w,
        alog_vec=_lane_vec(a_log, ALPHA_LANE), dt_vec=_lane_vec(dt_bias, ALPHA_LANE),
        gdn_norm_g=row(gdn_norm_g),
        mu=row(_pad_xb(mu_shift)),
        w0=row(w0), a0=row(a0), k_k=row(k_k), k_a=row(k_a), r_k=row(r_k.reshape(-1)),
        lnx_g=row(lnx_g), lnx_b=row(lnx_b),
        w2p=_pad_rows(w2, LANES).astype(BF16), a2p=_pad_rows(a2, LANES).astype(BF16), g2=g2.astype(BF16),
        w_up_a=w_up_a.astype(BF16), w_up_b=w_up_b.astype(BF16), w_o=w_o.astype(BF16),
        g_mlp=row(g_mlp), w_ff1=w_ff1.astype(BF16), w_ff2=w_ff2.astype(BF16),
        g_ple=row(g_ple), w_ple_gate=w_ple_gate.astype(BF16), w_ple=w_ple.astype(BF16),
    )


def _run_layer(x, ple, s_gdn, buf_gdn, s_rwkv, shift_rwkv, pw, g_final):
    b, t, _ = x.shape
    m = b * t
    c = min(CHUNK, t)
    assert t % c == 0 and c % SUBLANES == 0 and t >= CONV_W - 1 and b % SEQS_PER_STEP == 0
    tm_proj = min(m, 2048)
    tm_stream = min(m, 1024)
    tm_res = min(m, 256)
    x2 = x.reshape(m, D_MODEL)

    few_rows = m < 2 * tm_stream
    p = norm_matmul(x2, pw["g_mix"], pw["w_proj_t"], tm_proj, N_PROJ // 4 if few_rows else 512)
    p3d = p.reshape(b, t, N_PROJ)

    hist8 = jnp.concatenate([jnp.zeros((b, SUBLANES - (CONV_W - 1), CONV_CH), F32), buf_gdn], axis=1)
    o_a, s_gdn_new = gdn_branch(p3d, hist8, pw["conv_w"], pw["alog_vec"], pw["dt_vec"],
                                pw["gdn_norm_g"], s_gdn, b, t, c, hb=GDN_HB, nseq=SEQS_PER_STEP)
    prev = _pad_xb(shift_rwkv).reshape(b, 1, XB_PAD)
    o_b, s_rwkv_new = rwkv_branch(p3d, prev, pw["mu"], pw["w0"], pw["a0"], pw["k_k"], pw["k_a"], pw["r_k"],
                                  pw["lnx_g"], pw["lnx_b"], pw["w2p"], pw["a2p"], pw["g2"], s_rwkv,
                                  b, t, c, pg=RWKV_PG, nseq=SEQS_PER_STEP)

    x1, h2 = merge_out(x2, o_a.reshape(m, W_A), o_b.reshape(m, W_B), p, P2_OFF // D_MODEL,
                       pw["w_up_a"], pw["w_up_b"], pw["w_o"], pw["g_mlp"], tm_res, 512)
    x2 = mlp(h2, x1, pw["w_ff1"], pw["w_ff2"], tm_stream, 2048 if few_rows else 1024, 512)
    y = ple_final(x2, ple.reshape(m, D_PLE), pw["g_ple"], pw["w_ple_gate"], pw["w_ple"], g_final,
                  min(m, 512))

    new_buf = p3d[:, t - (CONV_W - 1):, P1_OFF:P1_OFF + CONV_CH]
    new_shift = _unpad_xb(p3d[:, t - 1, P3_OFF:])
    return y.reshape(b, t, D_MODEL), s_gdn_new, new_buf, s_rwkv_new, new_shift


def kernel(x_prompt, x_sample, p_prompt, p_sample, state_gdn, cache_gdn_conv, state_rwkv, cache_rwkv_shift, g_mix, w_in, conv_w, a_log, dt_bias, gdn_norm_g, mu_shift, w0, w2, a0, a2, g2, k_k, k_a, r_k, lnx_g, lnx_b, w_up_a, w_up_b, w_o, g_mlp, w_ff1, w_ff2, g_ple, w_ple_gate, w_ple, g_final):
    params = (g_mix, w_in, conv_w, a_log, dt_bias, gdn_norm_g, mu_shift, w0, w2, a0, a2, g2,
              k_k, k_a, r_k, lnx_g, lnx_b, w_up_a, w_up_b, w_o, g_mlp, w_ff1, w_ff2,
              g_ple, w_ple_gate, w_ple)
    depth = w_in.shape[0]
    assert depth == 1
    pw = _prepare(tuple(w[0] for w in params))
    gf = g_final.reshape(1, -1)
    bp = x_prompt.shape[0]
    z_sg = jnp.zeros((bp, H_A, DK_A, DV_A), F32)
    z_buf = jnp.zeros((bp, CONV_W - 1, CONV_CH), F32)
    z_sr = jnp.zeros((bp, H_B, N_B, N_B), F32)
    z_sh = jnp.zeros((bp, SHIFT_W), F32)
    y_p, sg_p, buf_p, sr_p, sh_p = _run_layer(x_prompt, p_prompt[0], z_sg, z_buf, z_sr, z_sh, pw, gf)
    y_s, sg_s, buf_s, sr_s, sh_s = _run_layer(x_sample, p_sample[0], state_gdn[0], cache_gdn_conv[0],
                                              state_rwkv[0], cache_rwkv_shift[0], pw, gf)
    st = lambda a: a[None]
    return (y_p, y_s, st(sg_p), st(buf_p), st(sr_p), st(sh_p), st(sg_s), st(buf_s), st(sr_s), st(sh_s))
```
